```python
import jax, jax.numpy as jnp
from jax import lax
import numpy as np

D_MODEL = 1024
BATCH = 16
SEQ = 2048
DEPTH = 2

N_MIXERS = 2
EPS = 1e-6
D_FF = 2816
POOL_WINDOWS = (2, 4, 8, 16)
N_POOL_GROUPS = 4
POOL_GROUP = D_MODEL // N_POOL_GROUPS
N_HEADS = 16
HEAD_DIM = D_MODEL // N_HEADS
N_KV_GROUPS = 4
HEADS_PER_GROUP = N_HEADS // N_KV_GROUPS
ROPE_DIM = HEAD_DIM // 4
ROPE_THETA = 500000.0
CMP_BLOCK = 32
CMP_STRIDE = 16
CMP_HIDDEN = 256
SEL_BLOCK = 64
N_SELECT = 8
SEL_Q_BLOCK = 32
WINDOW = 256
WIN_Q_BLOCK = 128
Q_WIDTH = N_HEADS * HEAD_DIM
KV_WIDTH = N_KV_GROUPS * HEAD_DIM
GATE_WIDTH = 3 * N_HEADS
IN_WIDTH = Q_WIDTH + 6 * KV_WIDTH + GATE_WIDTH
SPLIT_POINTS = tuple(Q_WIDTH + i * KV_WIDTH for i in range(7))
N_POOL_LAYERS = (DEPTH + N_MIXERS - 1) // N_MIXERS
N_NSA_LAYERS = DEPTH // N_MIXERS
NEG_INF = -1e30

kernel_name = "hybrid_pool_nsa_macaron"


def rms_norm(x, g):
    xf = x.astype(jnp.float32)
    y = xf * lax.rsqrt(jnp.mean(xf * xf, axis=-1, keepdims=True) + EPS)
    return (y * g.astype(jnp.float32)).astype(x.dtype)


def swiglu(x, w_gate, w_up, w_down):
    return (jax.nn.silu(x @ w_gate) * (x @ w_up)) @ w_down


def rope_tables(seq):
    inv = 1.0 / (ROPE_THETA ** (jnp.arange(0, ROPE_DIM, 2, dtype=jnp.float32) / ROPE_DIM))
    ang = jnp.arange(seq, dtype=jnp.float32)[:, None] * inv[None, :]
    return jnp.cos(ang), jnp.sin(ang)


def partial_rope(t, cos, sin):
    half = ROPE_DIM // 2
    c = cos[None, :, None, :].astype(t.dtype)
    s = sin[None, :, None, :].astype(t.dtype)
    x1, x2, rest = t[..., :half], t[..., half:ROPE_DIM], t[..., ROPE_DIM:]
    return jnp.concatenate([x1 * c - x2 * s, x1 * s + x2 * c, rest], axis=-1)


def pool_mixer(h, w_grp, b_grp, scale):
    B, S, D = h.shape
    hf = h.astype(jnp.float32)
    cum = jnp.concatenate([jnp.zeros((B, 1, D), jnp.float32), jnp.cumsum(hf, axis=1)], axis=1)
    end = jnp.arange(1, S + 1)
    outs = []
    for g, w in enumerate(POOL_WINDOWS):
        sl = slice(g * POOL_GROUP, (g + 1) * POOL_GROUP)
        lo = jnp.maximum(end - w, 0)
        cnt = (end - lo).astype(jnp.float32)[None, :, None]
        mean = (cum[:, 1:, sl] - cum[:, lo, sl]) / cnt
        outs.append(mean - hf[:, :, sl])
    d = jnp.stack(outs, axis=2).astype(h.dtype)
    y = jnp.einsum('bsgc,gce->bsge', d, w_grp) + b_grp
    return y.reshape(B, S, D) * scale


def compress_blocks(blk, pos_emb, w1, w2):
    B, Nc = blk.shape[:2]
    z = (blk + pos_emb[None, None, :, None, :]).transpose(0, 1, 3, 2, 4)
    z = z.reshape(B, Nc, N_KV_GROUPS, CMP_BLOCK * HEAD_DIM)
    return jax.nn.silu(z @ w1) @ w2


def nsa_mixer(h, w_in, pos_k, pos_v, wk1, wk2, wv1, wv2, w_o, cos, sin):
    B, S, _ = h.shape
    G, R, dh = N_KV_GROUPS, HEADS_PER_GROUP, HEAD_DIM
    scale = HEAD_DIM ** -0.5
    proj = h @ w_in
    q, k_c, v_c, k_s, v_s, k_w, v_w, gate = jnp.split(proj, SPLIT_POINTS, axis=-1)
    q = q.reshape(B, S, N_HEADS, dh)
    q_nope = q.reshape(B, S, G, R, dh)
    q_rot = partial_rope(q, cos, sin).reshape(B, S, G, R, dh)
    k_c, v_c, k_s, v_s, k_w, v_w = [t.reshape(B, S, G, dh) for t in (k_c, v_c, k_s, v_s, k_w, v_w)]
    k_s = partial_rope(k_s, cos, sin)
    k_w = partial_rope(k_w, cos, sin)
    pos = jnp.arange(S)

    n_cmp = (S - CMP_BLOCK) // CMP_STRIDE + 1
    blk_tok = jnp.arange(n_cmp)[:, None] * CMP_STRIDE + jnp.arange(CMP_BLOCK)[None, :]
    k_cmp = compress_blocks(k_c[:, blk_tok], pos_k, wk1, wk2)
    v_cmp = compress_blocks(v_c[:, blk_tok], pos_v, wv1, wv2)
    blk_end = jnp.arange(n_cmp) * CMP_STRIDE + CMP_BLOCK - 1
    cmp_mask = blk_end[None, :] <= pos[:, None]
    s_c = jnp.einsum('bsgrd,bngd->bgrsn', q_nope, k_cmp).astype(jnp.float32) * scale
    p_c = jax.nn.softmax(jnp.where(cmp_mask, s_c, NEG_INF), axis=-1)
    p_c = p_c * jnp.any(cmp_mask, axis=-1)[:, None].astype(jnp.float32)
    o_cmp = jnp.einsum('bgrsn,bngd->bsgrd', p_c.astype(v_cmp.dtype), v_cmp)

    n_sel_blk = S // SEL_BLOCK
    n_pick = min(N_SELECT, n_sel_blk)
    ci = jnp.arange(n_cmp) * CMP_STRIDE
    sj = jnp.arange(n_sel_blk) * SEL_BLOCK
    overlap = ((ci[:, None] < sj[None, :] + SEL_BLOCK) &
               (ci[:, None] + CMP_BLOCK > sj[None, :])).astype(jnp.float32)
    imp = jnp.einsum('bgrsn,nj->bgsj', p_c, overlap)
    q_blk = pos // SEL_BLOCK
    j = jnp.arange(n_sel_blk)[None, :]
    future = j > q_blk[:, None]
    forced = (j == 0) | (j == q_blk[:, None]) | (j == q_blk[:, None] - 1)
    imp = jnp.where(forced, jnp.inf, jnp.where(future, -jnp.inf, imp))
    _, sel_idx = lax.top_k(imp, n_pick)

    k_blocks = k_s.reshape(B, n_sel_blk, SEL_BLOCK, G, dh).transpose(0, 3, 1, 2, 4)
    v_blocks = v_s.reshape(B, n_sel_blk, SEL_BLOCK, G, dh).transpose(0, 3, 1, 2, 4)
    n_qs = S // SEL_Q_BLOCK
    b_ix = jnp.arange(B)[:, None, None]
    g_ix = jnp.arange(G)[None, :, None]

    def sel_block(args):
        qb, idx, start = args
        flat = idx.reshape(B, G, SEL_Q_BLOCK * n_pick)
        kb = k_blocks[b_ix, g_ix, flat].reshape(B, G, SEL_Q_BLOCK, n_pick * SEL_BLOCK, dh)
        vb = v_blocks[b_ix, g_ix, flat].reshape(B, G, SEL_Q_BLOCK, n_pick * SEL_BLOCK, dh)
        tok = (idx[..., None] * SEL_BLOCK + jnp.arange(SEL_BLOCK)).reshape(B, G, SEL_Q_BLOCK, n_pick * SEL_BLOCK)
        qpos = start + jnp.arange(SEL_Q_BLOCK)
        mask = (tok <= qpos[None, None, :, None])[:, :, None]
        s = jnp.einsum('bqgrd,bgqtd->bgrqt', qb, kb).astype(jnp.float32) * scale
        p = jax.nn.softmax(jnp.where(mask, s, NEG_INF), axis=-1).astype(vb.dtype)
        return jnp.einsum('bgrqt,bgqtd->bqgrd', p, vb)

    q_sel = jnp.moveaxis(q_rot.reshape(B, n_qs, SEL_Q_BLOCK, G, R, dh), 1, 0)
    idx_sel = jnp.moveaxis(sel_idx.reshape(B, G, n_qs, SEL_Q_BLOCK, n_pick), 2, 0)
    o_sel = lax.map(sel_block, (q_sel, idx_sel, jnp.arange(n_qs) * SEL_Q_BLOCK))
    o_sel = jnp.moveaxis(o_sel, 0, 1).reshape(B, S, G, R, dh)

    n_qw = S // WIN_Q_BLOCK
    k_pad = jnp.pad(k_w, ((0, 0), (WINDOW, 0), (0, 0), (0, 0)))
    v_pad = jnp.pad(v_w, ((0, 0), (WINDOW, 0), (0, 0), (0, 0)))
    band = WIN_Q_BLOCK + WINDOW

    def win_block(args):
        qb, start = args
        kb = lax.dynamic_slice_in_dim(k_pad, start, band, axis=1)
        vb = lax.dynamic_slice_in_dim(v_pad, start, band, axis=1)
        qpos = start + jnp.arange(WIN_Q_BLOCK)
        kpos = start - WINDOW + jnp.arange(band)
        diff = qpos[:, None] - kpos[None, :]
        mask = (diff >= 0) & (diff < WINDOW) & (kpos[None, :] >= 0)
        s = jnp.einsum('bqgrd,btgd->bgrqt', qb, kb).astype(jnp.float32) * scale
        p = jax.nn.softmax(jnp.where(mask, s, NEG_INF), axis=-1).astype(vb.dtype)
        return jnp.einsum('bgrqt,btgd->bqgrd', p, vb)

    q_win = jnp.moveaxis(q_rot.reshape(B, n_qw, WIN_Q_BLOCK, G, R, dh), 1, 0)
    o_win = lax.map(win_block, (q_win, jnp.arange(n_qw) * WIN_Q_BLOCK))
    o_win = jnp.moveaxis(o_win, 0, 1).reshape(B, S, G, R, dh)

    gts = jax.nn.sigmoid(gate.reshape(B, S, 3, G, R, 1))
    o = gts[:, :, 0] * o_cmp + gts[:, :, 1] * o_sel + gts[:, :, 2] * o_win
    return o.reshape(B, S, Q_WIDTH) @ w_o


def setup_inputs(seed: int = 0) -> dict:
    key = jax.random.key(seed)
    ks = jax.random.split(key, 20)
    nrm = lambda k, shape, fan_in: jax.random.normal(k, shape, jnp.float32) * (fan_in ** -0.5)
    return {
        "x": jax.random.normal(ks[0], (BATCH, SEQ, D_MODEL), jnp.float32),
        "norm_gains": 1.0 + 0.05 * jax.random.normal(ks[1], (DEPTH, 6, D_MODEL), jnp.float32),
        "ffn_w_gate": nrm(ks[2], (DEPTH, 2, D_MODEL, D_FF), D_MODEL),
        "ffn_w_up": nrm(ks[3], (DEPTH, 2, D_MODEL, D_FF), D_MODEL),
        "ffn_w_down": nrm(ks[4], (DEPTH, 2, D_FF, D_MODEL), D_FF),
        "pool_w": nrm(ks[5], (N_POOL_LAYERS, N_POOL_GROUPS, POOL_GROUP, POOL_GROUP), POOL_GROUP),
        "pool_b": 0.01 * jax.random.normal(ks[6], (N_POOL_LAYERS, N_POOL_GROUPS, POOL_GROUP), jnp.float32),
        "pool_scale": 0.5 + 0.1 * jax.random.normal(ks[7], (N_POOL_LAYERS, D_MODEL), jnp.float32),
        "nsa_w_in": nrm(ks[8], (N_NSA_LAYERS, D_MODEL, IN_WIDTH), D_MODEL),
        "nsa_cmp_pos_k": 0.1 * jax.random.normal(ks[9], (N_NSA_LAYERS, CMP_BLOCK, HEAD_DIM), jnp.float32),
        "nsa_cmp_pos_v": 0.1 * jax.random.normal(ks[10], (N_NSA_LAYERS, CMP_BLOCK, HEAD_DIM), jnp.float32),
        "nsa_cmp_wk1": nrm(ks[11], (N_NSA_LAYERS, CMP_BLOCK * HEAD_DIM, CMP_HIDDEN), CMP_BLOCK * HEAD_DIM),
        "nsa_cmp_wk2": nrm(ks[12], (N_NSA_LAYERS, CMP_HIDDEN, HEAD_DIM), CMP_HIDDEN),
        "nsa_cmp_wv1": nrm(ks[13], (N_NSA_LAYERS, CMP_BLOCK * HEAD_DIM, CMP_HIDDEN), CMP_BLOCK * HEAD_DIM),
        "nsa_cmp_wv2": nrm(ks[14], (N_NSA_LAYERS, CMP_HIDDEN, HEAD_DIM), CMP_HIDDEN),
        "nsa_w_o": nrm(ks[15], (N_NSA_LAYERS, Q_WIDTH, D_MODEL), Q_WIDTH),
    }


def reference(x, norm_gains, ffn_w_gate, ffn_w_up, ffn_w_down, pool_w, pool_b, pool_scale,
              nsa_w_in, nsa_cmp_pos_k, nsa_cmp_pos_v, nsa_cmp_wk1, nsa_cmp_wk2,
              nsa_cmp_wv1, nsa_cmp_wv2, nsa_w_o):
    cos, sin = rope_tables(x.shape[1])
    for i in range(DEPTH):
        g = norm_gains[i]
        f = swiglu(rms_norm(x, g[0]), ffn_w_gate[i, 0], ffn_w_up[i, 0], ffn_w_down[i, 0])
        x = x + 0.5 * rms_norm(f, g[1])
        h = rms_norm(x, g[2])
        j = i // N_MIXERS
        if i % N_MIXERS == 0:
            m = pool_mixer(h, pool_w[j], pool_b[j], pool_scale[j])
        else:
            m = nsa_mixer(h, nsa_w_in[j], nsa_cmp_pos_k[j], nsa_cmp_pos_v[j], nsa_cmp_wk1[j],
                          nsa_cmp_wk2[j], nsa_cmp_wv1[j], nsa_cmp_wv2[j], nsa_w_o[j], cos, sin)
        x = x + rms_norm(m, g[3])
        f = swiglu(rms_norm(x, g[4]), ffn_w_gate[i, 1], ffn_w_up[i, 1], ffn_w_down[i, 1])
        x = x + 0.5 * rms_norm(f, g[5])
    return x
```

```python
import functools

import jax
import jax.numpy as jnp
from jax import lax
from jax.experimental import pallas as pl
from jax.experimental.pallas import tpu as pltpu

F32 = jnp.float32
BF16 = jnp.bfloat16

D_MODEL = 1024
EPS = 1e-6
D_FF = 2816
POOL_WINDOWS = (2, 4, 8, 16)
POOL_GROUP = D_MODEL // len(POOL_WINDOWS)
MAX_POOL_WINDOW = max(POOL_WINDOWS)
N_HEADS = 16
HEAD_DIM = 64
N_KV_GROUPS = 4
HEADS_PER_GROUP = N_HEADS // N_KV_GROUPS
ROPE_DIM = HEAD_DIM // 4
ROPE_THETA = 500000.0
CMP_BLOCK = 32
CMP_STRIDE = 16
CMP_HIDDEN = 256
SEL_BLOCK = 64
N_SELECT = 8
WINDOW = 256
Q_WIDTH = N_HEADS * HEAD_DIM
KV_WIDTH = N_KV_GROUPS * HEAD_DIM
GATE_WIDTH = 3 * N_HEADS
NEG_INF = -1e30

LANES = 128
VMEM_LIMIT = 56 * 1024 * 1024

FFN_TOKENS = 512
SEQ_TILE = 512
ATT_Q = 256
ATT_K = 256
N_CMP_PAD = 128

_Q_COLS = N_HEADS * LANES
_G_COLS = N_KV_GROUPS * LANES
_OFF_KS = _Q_COLS
_OFF_KW = _OFF_KS + _G_COLS
_OFF_VS = _OFF_KW + _G_COLS
_OFF_VW = _OFF_VS + _G_COLS
_OFF_KVC = _OFF_VW + _G_COLS
_OFF_GATE = _OFF_KVC + _G_COLS
_PROJ_COLS = _OFF_GATE + _G_COLS

_NT = (((1,), (1,)), ((), ()))


def _rms(x, g):
    return x * lax.rsqrt(jnp.mean(x * x, axis=-1, keepdims=True) + EPS) * g


def _const_spec(shape):
    nd = len(shape)
    return pl.BlockSpec(shape, lambda *_: (0,) * nd, pipeline_mode=pl.Buffered(1))


def _ffn_body(x_ref, gpre_ref, gpost_ref, wg_ref, wu_ref, wd_ref, o_ref):
    x = x_ref[...]
    xb = _rms(x, gpre_ref[...]).astype(BF16)
    hg = jnp.dot(xb, wg_ref[...], preferred_element_type=F32)
    hu = jnp.dot(xb, wu_ref[...], preferred_element_type=F32)
    act = (hg * jax.nn.sigmoid(hg) * hu).astype(BF16)
    f = jnp.dot(act, wd_ref[...], preferred_element_type=F32)
    o_ref[...] = x + 0.5 * _rms(f, gpost_ref[...])


def _ffn_block(x2d, g_pre, g_post, wg, wu, wd):
    t = x2d.shape[0]
    row = pl.BlockSpec((FFN_TOKENS, D_MODEL), lambda i: (i, 0))
    return pl.pallas_call(
        _ffn_body,
        grid=(t // FFN_TOKENS,),
        in_specs=[row, _const_spec((1, D_MODEL)), _const_spec((1, D_MODEL)),
                  _const_spec((D_MODEL, D_FF)), _const_spec((D_MODEL, D_FF)),
                  _const_spec((D_FF, D_MODEL))],
        out_specs=row,
        out_shape=jax.ShapeDtypeStruct((t, D_MODEL), F32),
        compiler_params=pltpu.CompilerParams(
            dimension_semantics=("parallel",), vmem_limit_bytes=VMEM_LIMIT),
        name="ffn_block",
    )(x2d, g_pre.reshape(1, -1), g_post.reshape(1, -1),
      wg.astype(BF16), wu.astype(BF16), wd.astype(BF16))


def _pool_body(x_ref, g2_ref, g3_ref, w_ref, b_ref, sc_ref, o_ref, carry_ref):
    si = pl.program_id(1)

    @pl.when(si == 0)
    def _():
        carry_ref[...] = jnp.zeros_like(carry_ref)

    x = x_ref[...]
    h = _rms(x, g2_ref[...])
    hp = jnp.concatenate([carry_ref[...], h], axis=0)
    carry_ref[...] = h[SEQ_TILE - MAX_POOL_WINDOW:, :]
    pos = si * SEQ_TILE + lax.broadcasted_iota(jnp.int32, (SEQ_TILE, 1), 0)
    ys = []
    for g, w in enumerate(POOL_WINDOWS):
        cols = slice(g * POOL_GROUP, (g + 1) * POOL_GROUP)
        acc = hp[:, cols]
        k = 1
        while k < w:
            acc = acc + pltpu.roll(acc, k, axis=0)
            k *= 2
        cnt = jnp.minimum(pos + 1, w).astype(F32)
        d = acc[MAX_POOL_WINDOW:, :] / cnt - h[:, cols]
        y = jnp.dot(d.astype(BF16), w_ref[g], preferred_element_type=F32) + b_ref[g:g + 1, :]
        ys.append(y)
    m = jnp.concatenate(ys, axis=1) * sc_ref[...]
    o_ref[...] = x + _rms(m, g3_ref[...])


def _pool_block(x, g2, g3, w, b, scale):
    bsz, seq, _ = x.shape
    row = pl.BlockSpec((None, SEQ_TILE, D_MODEL), lambda bi, si: (bi, si, 0))
    return pl.pallas_call(
        _pool_body,
        grid=(bsz, seq // SEQ_TILE),
        in_specs=[row, _const_spec((1, D_MODEL)), _const_spec((1, D_MODEL)),
                  _const_spec((len(POOL_WINDOWS), POOL_GROUP, POOL_GROUP)),
                  _const_spec((len(POOL_WINDOWS), POOL_GROUP)), _const_spec((1, D_MODEL))],
        out_specs=row,
        out_shape=jax.ShapeDtypeStruct(x.shape, F32),
        scratch_shapes=[pltpu.VMEM((MAX_POOL_WINDOW, D_MODEL), F32)],
        compiler_params=pltpu.CompilerParams(
            dimension_semantics=("parallel", "arbitrary"), vmem_limit_bytes=VMEM_LIMIT),
        name="pool_block",
    )(x, g2.reshape(1, -1), g3.reshape(1, -1), w.astype(BF16), b, scale.reshape(1, -1))


def _proj_body(x_ref, g2_ref, w_ref, c_ref, sa_ref, sb_ref,
               q_ref, ks_ref, kw_ref, vs_ref, vw_ref, kvc_ref, gate_ref):
    si = pl.program_id(1)
    hb = _rms(x_ref[...], g2_ref[...]).astype(BF16)
    proj = jnp.dot(hb, w_ref[...], preferred_element_type=F32)
    cos, sin_a, sin_b = c_ref[...], sa_ref[...], sb_ref[...]

    def slab(off, i):
        return proj[:, off + i * LANES: off + (i + 1) * LANES]

    def rope(t):
        return (t * cos + pltpu.roll(t, LANES - ROPE_DIM // 2, axis=1) * sin_a
                + pltpu.roll(t, ROPE_DIM // 2, axis=1) * sin_b)

    for h in range(N_HEADS):
        q_ref[:, h * LANES:(h + 1) * LANES] = rope(slab(0, h)).astype(BF16)
    pos = si * SEQ_TILE + lax.broadcasted_iota(jnp.int32, (SEQ_TILE, LANES), 0)
    lane = lax.broadcasted_iota(jnp.int32, (SEQ_TILE, LANES), 1)
    blk_onehot = jnp.where(lane - HEAD_DIM == pos // SEL_BLOCK, 1.0, 0.0)
    for g in range(N_KV_GROUPS):
        ks_ref[g] = (rope(slab(_OFF_KS, g)) + blk_onehot).astype(BF16)
        kw_ref[g] = rope(slab(_OFF_KW, g)).astype(BF16)
        vs_ref[g] = slab(_OFF_VS, g).astype(BF16)
        vw_ref[g] = slab(_OFF_VW, g).astype(BF16)
        kvc_ref[g] = slab(_OFF_KVC, g)
        gate_ref[g] = jax.nn.sigmoid(slab(_OFF_GATE, g))


def _nsa_proj_weights(w_in):
    d = w_in.shape[0]
    wq = (w_in[:, :Q_WIDTH] * (HEAD_DIM ** -0.5)).reshape(d, N_HEADS, 1, HEAD_DIM)
    qcols = jnp.broadcast_to(wq, (d, N_HEADS, 2, HEAD_DIM)).reshape(d, _Q_COLS)
    kv = w_in[:, Q_WIDTH:Q_WIDTH + 6 * KV_WIDTH].reshape(d, 6, N_KV_GROUPS, HEAD_DIM)
    wkc, wvc, wks, wvs, wkw, wvw = [kv[:, i] for i in range(6)]
    zero = jnp.zeros_like(wkc)

    def pair(a, b):
        return jnp.concatenate([a, b], axis=-1).reshape(d, _G_COLS)

    wgate = w_in[:, Q_WIDTH + 6 * KV_WIDTH:].reshape(d, 3, N_KV_GROUPS, HEADS_PER_GROUP)
    wgate = wgate.transpose(0, 2, 1, 3).reshape(d, N_KV_GROUPS, 3 * HEADS_PER_GROUP)
    wgate = jnp.pad(wgate, ((0, 0), (0, 0), (0, LANES - 3 * HEADS_PER_GROUP))).reshape(d, _G_COLS)
    return jnp.concatenate(
        [qcols, pair(wks, zero), pair(wkw, zero), pair(wvs, wvs), pair(wvw, wvw),
         pair(wkc, wvc), wgate], axis=1).astype(BF16)


def _rope_slab_tables(seq):
    half = ROPE_DIM // 2
    inv = 1.0 / (ROPE_THETA ** (jnp.arange(0, ROPE_DIM, 2, dtype=F32) / ROPE_DIM))
    ang = jnp.arange(seq, dtype=F32)[:, None] * inv[None, :]
    cos, sin = jnp.cos(ang), jnp.sin(ang)
    pad = LANES - ROPE_DIM
    c = jnp.concatenate([cos, cos, jnp.ones((seq, pad), F32)], axis=1)
    sa = jnp.concatenate([-sin, jnp.zeros((seq, half + pad), F32)], axis=1)
    sb = jnp.concatenate([jnp.zeros((seq, half), F32), sin, jnp.zeros((seq, pad), F32)], axis=1)
    return c, sa, sb


def _nsa_proj(x, g2, w_all, tables):
    bsz, seq, _ = x.shape
    grp = lambda dt: jax.ShapeDtypeStruct((bsz, N_KV_GROUPS, seq, LANES), dt)
    grp_spec = pl.BlockSpec((None, N_KV_GROUPS, SEQ_TILE, LANES), lambda bi, si: (bi, 0, si, 0))
    tab_spec = pl.BlockSpec((SEQ_TILE, LANES), lambda bi, si: (si, 0))
    return pl.pallas_call(
        _proj_body,
        grid=(bsz, seq // SEQ_TILE),
        in_specs=[pl.BlockSpec((None, SEQ_TILE, D_MODEL), lambda bi, si: (bi, si, 0)),
                  _const_spec((1, D_MODEL)), _const_spec((D_MODEL, _PROJ_COLS)),
                  tab_spec, tab_spec, tab_spec],
        out_specs=[pl.BlockSpec((None, SEQ_TILE, _Q_COLS), lambda bi, si: (bi, si, 0)),
                   grp_spec, grp_spec, grp_spec, grp_spec, grp_spec, grp_spec],
        out_shape=[jax.ShapeDtypeStruct((bsz, seq, _Q_COLS), BF16),
                   grp(BF16), grp(BF16), grp(BF16), grp(BF16), grp(F32), grp(F32)],
        compiler_params=pltpu.CompilerParams(
            dimension_semantics=("parallel", "parallel"), vmem_limit_bytes=VMEM_LIMIT),
        name="nsa_proj",
    )(x, g2.reshape(1, -1), w_all, *tables)


def _cmp_body(kvc_ref, pos_ref, w1_ref, w2k_ref, w2v_ref, kc_ref, vc_ref):
    first = jnp.zeros((N_CMP_PAD, 2 * CMP_HIDDEN), F32)
    second = jnp.zeros((N_CMP_PAD, 2 * CMP_HIDDEN), F32)
    for t in range(CMP_STRIDE):
        xt = kvc_ref[pl.ds(t, N_CMP_PAD, stride=CMP_STRIDE), :]
        za = (xt + pos_ref[t:t + 1, :]).astype(BF16)
        zb = (xt + pos_ref[CMP_STRIDE + t:CMP_STRIDE + t + 1, :]).astype(BF16)
        first = first + jnp.dot(za, w1_ref[t], preferred_element_type=F32)
        second = second + jnp.dot(zb, w1_ref[CMP_STRIDE + t], preferred_element_type=F32)
    row = lax.broadcasted_iota(jnp.int32, second.shape, 0)
    second = jnp.where(row < N_CMP_PAD - 1, pltpu.roll(second, N_CMP_PAD - 1, axis=0), 0.0)
    pre = first + second
    hid = (pre * jax.nn.sigmoid(pre)).astype(BF16)
    kc_ref[...] = jnp.dot(hid[:, :CMP_HIDDEN], w2k_ref[...], preferred_element_type=F32).astype(BF16)
    vc_ref[...] = jnp.dot(hid[:, CMP_HIDDEN:], w2v_ref[...], preferred_element_type=F32).astype(BF16)


def _compress(kvc, pos_k, pos_v, wk1, wk2, wv1, wv2):
    bsz, _, seq, _ = kvc.shape
    assert seq == N_CMP_PAD * CMP_STRIDE
    pos = jnp.concatenate([pos_k, pos_v], axis=1)
    k1 = wk1.reshape(CMP_BLOCK, HEAD_DIM, CMP_HIDDEN)
    v1 = wv1.reshape(CMP_BLOCK, HEAD_DIM, CMP_HIDDEN)
    z1 = jnp.zeros_like(k1)
    w1 = jnp.concatenate([jnp.concatenate([k1, z1], axis=2),
                          jnp.concatenate([z1, v1], axis=2)], axis=1).astype(BF16)
    w2k = jnp.concatenate([jnp.zeros_like(wk2), wk2], axis=1).astype(BF16)
    w2v = jnp.concatenate([wv2, wv2], axis=1).astype(BF16)
    out = jax.ShapeDtypeStruct((bsz, N_KV_GROUPS, N_CMP_PAD, LANES), BF16)
    out_spec = pl.BlockSpec((None, None, N_CMP_PAD, LANES), lambda bi, gi: (bi, gi, 0, 0))
    return pl.pallas_call(
        _cmp_body,
        grid=(bsz, N_KV_GROUPS),
        in_specs=[pl.BlockSpec((None, None, seq, LANES), lambda bi, gi: (bi, gi, 0, 0)),
                  _const_spec(pos.shape), _const_spec(w1.shape),
                  _const_spec(w2k.shape), _const_spec(w2v.shape)],
        out_specs=[out_spec, out_spec],
        out_shape=[out, out],
        compiler_params=pltpu.CompilerParams(dimension_semantics=("parallel", "parallel")),
        name="nsa_compress",
    )(kvc, pos, w1, w2k, w2v)


def _softmax_update(s, v, m, l, acc):
    m_new = jnp.maximum(m, jnp.max(s, axis=-1, keepdims=True))
    alpha = jnp.exp(m - m_new)
    p = jnp.exp(s - m_new)
    l = alpha * l + jnp.sum(p, axis=-1, keepdims=True)
    acc = alpha * acc + jnp.dot(p.astype(BF16), v, preferred_element_type=F32)
    return m_new, l, acc


def _attn_body(q_ref, kc_ref, vc_ref, ks_ref, kw_ref, vs_ref, vw_ref, gate_ref, ovt_ref,
               o_ref, lhs_ref):
    qi = pl.program_id(2)
    q0 = pl.multiple_of(qi * ATT_Q, ATT_Q)
    rows = HEADS_PER_GROUP * ATT_Q

    qpos = q0 + lax.broadcasted_iota(jnp.int32, (ATT_Q, N_CMP_PAD), 0)
    blk_end = lax.broadcasted_iota(jnp.int32, (ATT_Q, N_CMP_PAD), 1) * CMP_STRIDE + CMP_BLOCK - 1
    cmp_mask = blk_end <= qpos
    any_cmp = qpos >= CMP_BLOCK - 1
    kc = kc_ref[...]
    vc = vc_ref[...]
    p_sum = jnp.zeros((ATT_Q, N_CMP_PAD), F32)
    o_cmp = []
    for r in range(HEADS_PER_GROUP):
        s = lax.dot_general(q_ref[:, r * LANES:(r + 1) * LANES], kc, _NT, preferred_element_type=F32)
        s = jnp.where(cmp_mask, s, NEG_INF)
        e = jnp.exp(s - jnp.max(s, axis=-1, keepdims=True))
        p = jnp.where(any_cmp, e / jnp.sum(e, axis=-1, keepdims=True), 0.0)
        p_sum = p_sum + p
        o_cmp.append(jnp.dot(p.astype(BF16), vc, preferred_element_type=F32))

    hi = p_sum.astype(BF16)
    rem = p_sum - hi.astype(F32)
    mid = rem.astype(BF16)
    lo = (rem - mid.astype(F32)).astype(BF16)
    ovt = ovt_ref[...]
    imp = (lax.dot_general(ovt, hi, _NT, preferred_element_type=F32)
           + lax.dot_general(ovt, mid, _NT, preferred_element_type=F32)
           + lax.dot_general(ovt, lo, _NT, preferred_element_type=F32))
    n_blk = ovt.shape[0] // 4
    imp = imp[2 * n_blk:3 * n_blk, :]
    j = lax.broadcasted_iota(jnp.int32, (n_blk, ATT_Q), 0)
    q_blk = (q0 + lax.broadcasted_iota(jnp.int32, (n_blk, ATT_Q), 1)) // SEL_BLOCK
    forced = (j == 0) | (j == q_blk) | (j == q_blk - 1)
    imp = jnp.where(forced, jnp.inf, jnp.where(j > q_blk, -jnp.inf, imp))
    rank = jnp.zeros((n_blk, ATT_Q), jnp.int32)
    for jp in range(n_blk):
        other = imp[jp:jp + 1, :]
        beats = (other > imp) | ((other == imp) & (j > jp))
        rank = rank + beats.astype(jnp.int32)
    sel_bias = jnp.where(rank < N_SELECT, 0.0, NEG_INF)
    bias_rows = jnp.concatenate(
        [jnp.zeros((2 * n_blk, ATT_Q), F32), sel_bias, jnp.zeros((n_blk, ATT_Q), F32)], axis=0)
    bias_lanes = bias_rows.T.astype(BF16)
    lane = lax.broadcasted_iota(jnp.int32, (ATT_Q, LANES), 1)
    for r in range(HEADS_PER_GROUP):
        lhs_ref[r * ATT_Q:(r + 1) * ATT_Q, :] = jnp.where(
            lane < HEAD_DIM, q_ref[:, r * LANES:(r + 1) * LANES], bias_lanes)
    lhs = lhs_ref[...]

    def sel_step(kt, carry):
        k0 = pl.multiple_of(kt * ATT_K, ATT_K)
        s = lax.dot_general(lhs, ks_ref[pl.ds(k0, ATT_K), :], _NT, preferred_element_type=F32)
        return _softmax_update(s, vs_ref[pl.ds(k0, ATT_K), :], *carry)

    init = (jnp.full((rows, 1), NEG_INF, F32), jnp.zeros((rows, 1), F32), jnp.zeros((rows, LANES), F32))
    carry = lax.fori_loop(0, qi, sel_step, init)
    row_in = lax.broadcasted_iota(jnp.int32, (rows, ATT_K), 0) & (ATT_Q - 1)
    col_in = lax.broadcasted_iota(jnp.int32, (rows, ATT_K), 1)
    causal = col_in <= row_in
    s = lax.dot_general(lhs, ks_ref[pl.ds(q0, ATT_K), :], _NT, preferred_element_type=F32)
    _, l_sel, acc_sel = _softmax_update(jnp.where(causal, s, NEG_INF), vs_ref[pl.ds(q0, ATT_K), :], *carry)
    o_sel = acc_sel / l_sel

    p0 = pl.multiple_of(jnp.maximum(qi - 1, 0) * ATT_K, ATT_K)
    s_prev = lax.dot_general(lhs, kw_ref[pl.ds(p0, ATT_K), :], _NT, preferred_element_type=F32)
    s_prev = jnp.where((col_in > row_in) & (qi > 0), s_prev, NEG_INF)
    s_diag = lax.dot_general(lhs, kw_ref[pl.ds(q0, ATT_K), :], _NT, preferred_element_type=F32)
    s_diag = jnp.where(causal, s_diag, NEG_INF)
    m_w = jnp.maximum(jnp.max(s_prev, axis=-1, keepdims=True), jnp.max(s_diag, axis=-1, keepdims=True))
    p_prev = jnp.exp(s_prev - m_w)
    p_diag = jnp.exp(s_diag - m_w)
    l_w = jnp.sum(p_prev, axis=-1, keepdims=True) + jnp.sum(p_diag, axis=-1, keepdims=True)
    o_win = (jnp.dot(p_prev.astype(BF16), vw_ref[pl.ds(p0, ATT_K), :], preferred_element_type=F32)
             + jnp.dot(p_diag.astype(BF16), vw_ref[pl.ds(q0, ATT_K), :], preferred_element_type=F32)) / l_w

    gate = gate_ref[...]
    heads = []
    for r in range(HEADS_PER_GROUP):
        rs = slice(r * ATT_Q, (r + 1) * ATT_Q)
        g_cmp = gate[:, r:r + 1]
        g_sel = gate[:, HEADS_PER_GROUP + r:HEADS_PER_GROUP + r + 1]
        g_win = gate[:, 2 * HEADS_PER_GROUP + r:2 * HEADS_PER_GROUP + r + 1]
        heads.append(g_cmp * o_cmp[r] + g_sel * o_sel[rs] + g_win * o_win[rs])
    for pr in range(HEADS_PER_GROUP // 2):
        o_ref[:, pr * LANES:(pr + 1) * LANES] = jnp.where(
            lane < HEAD_DIM, heads[2 * pr], heads[2 * pr + 1]).astype(BF16)


def _overlap_rows(seq):
    n_sel = seq // SEL_BLOCK
    ci = jnp.arange(N_CMP_PAD) * CMP_STRIDE
    sj = jnp.arange(n_sel) * SEL_BLOCK
    n_cmp = (seq - CMP_BLOCK) // CMP_STRIDE + 1
    ov = ((ci[None, :] < sj[:, None] + SEL_BLOCK) & (ci[None, :] + CMP_BLOCK > sj[:, None])
          & (jnp.arange(N_CMP_PAD)[None, :] < n_cmp))
    return jnp.pad(ov.astype(BF16), ((2 * n_sel, n_sel), (0, 0)))


def _attention(q_ext, kc, vc, ks, kw, vs, vw, gate):
    bsz, seq, _ = q_ext.shape
    assert seq // SEL_BLOCK == LANES // 4 and ATT_Q == WINDOW == ATT_K
    gw = HEADS_PER_GROUP * LANES
    kv_spec = pl.BlockSpec((None, None, seq, LANES), lambda bi, gi, qi: (bi, gi, 0, 0))
    cmp_spec = pl.BlockSpec((None, None, N_CMP_PAD, LANES), lambda bi, gi, qi: (bi, gi, 0, 0))
    return pl.pallas_call(
        _attn_body,
        grid=(bsz, N_KV_GROUPS, seq // ATT_Q),
        in_specs=[pl.BlockSpec((None, ATT_Q, gw), lambda bi, gi, qi: (bi, qi, gi)),
                  cmp_spec, cmp_spec, kv_spec, kv_spec, kv_spec, kv_spec,
                  pl.BlockSpec((None, None, ATT_Q, LANES), lambda bi, gi, qi: (bi, gi, qi, 0)),
                  _const_spec((LANES, N_CMP_PAD))],
        out_specs=pl.BlockSpec((None, ATT_Q, HEADS_PER_GROUP * HEAD_DIM), lambda bi, gi, qi: (bi, qi, gi)),
        out_shape=jax.ShapeDtypeStruct((bsz, seq, Q_WIDTH), BF16),
        scratch_shapes=[pltpu.VMEM((HEADS_PER_GROUP * ATT_Q, LANES), BF16)],
        compiler_params=pltpu.CompilerParams(
            dimension_semantics=("parallel", "parallel", "arbitrary"), vmem_limit_bytes=VMEM_LIMIT),
        name="nsa_attention",
    )(q_ext, kc, vc, ks, kw, vs, vw, gate, _overlap_rows(seq))


def _oproj_body(o_ref_in, x_ref, w_ref, g_ref, out_ref):
    m = jnp.dot(o_ref_in[...], w_ref[...], preferred_element_type=F32)
    out_ref[...] = x_ref[...] + _rms(m, g_ref[...])


def _out_proj(o2d, x2d, w_o, g3):
    t = x2d.shape[0]
    row = pl.BlockSpec((FFN_TOKENS, D_MODEL), lambda i: (i, 0))
    return pl.pallas_call(
        _oproj_body,
        grid=(t // FFN_TOKENS,),
        in_specs=[row, row, _const_spec((Q_WIDTH, D_MODEL)), _const_spec((1, D_MODEL))],
        out_specs=row,
        out_shape=jax.ShapeDtypeStruct((t, D_MODEL), F32),
        compiler_params=pltpu.CompilerParams(dimension_semantics=("parallel",)),
        name="nsa_out_proj",
    )(o2d, x2d, w_o.astype(BF16), g3.reshape(1, -1))


def _nsa_block(x, g2, g3, w_in, pos_k, pos_v, wk1, wk2, wv1, wv2, w_o):
    bsz, seq, d = x.shape
    q_ext, ks, kw, vs, vw, kvc, gate = _nsa_proj(x, g2, _nsa_proj_weights(w_in), _rope_slab_tables(seq))
    kc, vc = _compress(kvc, pos_k, pos_v, wk1, wk2, wv1, wv2)
    o = _attention(q_ext, kc, vc, ks, kw, vs, vw, gate)
    return _out_proj(o.reshape(bsz * seq, Q_WIDTH), x.reshape(bsz * seq, d), w_o, g3).reshape(x.shape)


def kernel(x, norm_gains, ffn_w_gate, ffn_w_up, ffn_w_down, pool_w, pool_b, pool_scale, nsa_w_in, nsa_cmp_pos_k, nsa_cmp_pos_v, nsa_cmp_wk1, nsa_cmp_wk2, nsa_cmp_wv1, nsa_cmp_wv2, nsa_w_o):
    bsz, seq, d = x.shape
    depth = norm_gains.shape[0]

    def ffn(x, i, half):
        g = norm_gains[i]
        y = _ffn_block(x.reshape(bsz * seq, d), g[4 * half], g[4 * half + 1],
                       ffn_w_gate[i, half], ffn_w_up[i, half], ffn_w_down[i, half])
        return y.reshape(bsz, seq, d)

    for i in range(depth):
        g = norm_gains[i]
        x = ffn(x, i, 0)
        j = i // 2
        if i % 2 == 0:
            x = _pool_block(x, g[2], g[3], pool_w[j], pool_b[j], pool_scale[j])
        else:
            x = _nsa_block(x, g[2], g[3], nsa_w_in[j], nsa_cmp_pos_k[j], nsa_cmp_pos_v[j],
                           nsa_cmp_wk1[j], nsa_cmp_wk2[j], nsa_cmp_wv1[j], nsa_cmp_wv2[j], nsa_w_o[j])
        x = ffn(x, i, 1)
    return x
```

```python
import functools

import jax
import jax.numpy as jnp
from jax import lax
from jax.experimental import pallas as pl
from jax.experimental.pallas import tpu as pltpu

F32 = jnp.float32
BF16 = jnp.bfloat16

D_MODEL = 1024
EPS = 1e-6
D_FF = 2816
POOL_WINDOWS = (2, 4, 8, 16)
POOL_GROUP = D_MODEL // len(POOL_WINDOWS)
MAX_POOL_WINDOW = max(POOL_WINDOWS)
N_HEADS = 16
HEAD_DIM = 64
N_KV_GROUPS = 4
HEADS_PER_GROUP = N_HEADS // N_KV_GROUPS
ROPE_DIM = HEAD_DIM // 4
ROPE_THETA = 500000.0
CMP_BLOCK = 32
CMP_STRIDE = 16
CMP_HIDDEN = 256
SEL_BLOCK = 64
N_SELECT = 8
WINDOW = 256
Q_WIDTH = N_HEADS * HEAD_DIM
KV_WIDTH = N_KV_GROUPS * HEAD_DIM
GATE_WIDTH = 3 * N_HEADS
NEG_INF = -1e30

LANES = 128
VMEM_LIMIT = 56 * 1024 * 1024

FFN_TOKENS = 512
SEQ_TILE = 512
ATT_Q = 256
ATT_K = 256
N_CMP_PAD = 128
GROUPS_PER_STEP = 2

N_SEL_BLOCKS = 32
GATE_ROWS = 16
ROPE_HALF = ROPE_DIM // 2
V_ROWS = HEAD_DIM + 16
LOG2_E = 1.4426950408889634
ATT_COLS = HEADS_PER_GROUP * ATT_Q
N_CHUNKS = ATT_COLS // LANES

_ROW_VS = Q_WIDTH
_ROW_VW = _ROW_VS + KV_WIDTH
_ROW_GATE = _ROW_VW + KV_WIDTH
_PROJ_ROWS = _ROW_GATE + N_KV_GROUPS * GATE_ROWS
_G_COLS = N_KV_GROUPS * LANES
_OFF_KW = _G_COLS
_OFF_KVC = 2 * _G_COLS
_PROJ_COLS = 3 * _G_COLS

_NT = (((1,), (1,)), ((), ()))


def _rms(x, g):
    return x * lax.rsqrt(jnp.mean(x * x, axis=-1, keepdims=True) + EPS) * g


def _const_spec(shape):
    nd = len(shape)
    return pl.BlockSpec(shape, lambda *_: (0,) * nd, pipeline_mode=pl.Buffered(1))


def _ffn_body(x_ref, gpre_ref, gpost_ref, wg_ref, wu_ref, wd_ref, o_ref):
    x = x_ref[...]
    xb = _rms(x, gpre_ref[...]).astype(BF16)
    hg = jnp.dot(xb, wg_ref[...], preferred_element_type=F32)
    hu = jnp.dot(xb, wu_ref[...], preferred_element_type=F32)
    act = (hg * jax.nn.sigmoid(hg) * hu).astype(BF16)
    f = jnp.dot(act, wd_ref[...], preferred_element_type=F32)
    o_ref[...] = x + 0.5 * _rms(f, gpost_ref[...])


def _ffn_block(x2d, g_pre, g_post, wg, wu, wd):
    t = x2d.shape[0]
    row = pl.BlockSpec((FFN_TOKENS, D_MODEL), lambda i: (i, 0))
    return pl.pallas_call(
        _ffn_body,
        grid=(t // FFN_TOKENS,),
        in_specs=[row, _const_spec((1, D_MODEL)), _const_spec((1, D_MODEL)),
                  _const_spec((D_MODEL, D_FF)), _const_spec((D_MODEL, D_FF)),
                  _const_spec((D_FF, D_MODEL))],
        out_specs=row,
        out_shape=jax.ShapeDtypeStruct((t, D_MODEL), F32),
        compiler_params=pltpu.CompilerParams(
            dimension_semantics=("parallel",), vmem_limit_bytes=VMEM_LIMIT),
        name="ffn_block",
    )(x2d, g_pre.reshape(1, -1), g_post.reshape(1, -1),
      wg.astype(BF16), wu.astype(BF16), wd.astype(BF16))


def _pool_body(x_ref, g2_ref, g3_ref, w_ref, b_ref, sc_ref, o_ref, carry_ref):
    si = pl.program_id(1)

    @pl.when(si == 0)
    def _():
        carry_ref[...] = jnp.zeros_like(carry_ref)

    x = x_ref[...]
    h = _rms(x, g2_ref[...])
    hp = jnp.concatenate([carry_ref[...], h], axis=0)
    carry_ref[...] = h[SEQ_TILE - MAX_POOL_WINDOW:, :]
    pos = si * SEQ_TILE + lax.broadcasted_iota(jnp.int32, (SEQ_TILE, 1), 0)
    ys = []
    for g, w in enumerate(POOL_WINDOWS):
        cols = slice(g * POOL_GROUP, (g + 1) * POOL_GROUP)
        acc = hp[:, cols]
        k = 1
        while k < w:
            acc = acc + pltpu.roll(acc, k, axis=0)
            k *= 2
        cnt = jnp.minimum(pos + 1, w).astype(F32)
        d = acc[MAX_POOL_WINDOW:, :] / cnt - h[:, cols]
        y = jnp.dot(d.astype(BF16), w_ref[g], preferred_element_type=F32) + b_ref[g:g + 1, :]
        ys.append(y)
    m = jnp.concatenate(ys, axis=1) * sc_ref[...]
    o_ref[...] = x + _rms(m, g3_ref[...])


def _pool_block(x, g2, g3, w, b, scale):
    bsz, seq, _ = x.shape
    row = pl.BlockSpec((None, SEQ_TILE, D_MODEL), lambda bi, si: (bi, si, 0))
    return pl.pallas_call(
        _pool_body,
        grid=(bsz, seq // SEQ_TILE),
        in_specs=[row, _const_spec((1, D_MODEL)), _const_spec((1, D_MODEL)),
                  _const_spec((len(POOL_WINDOWS), POOL_GROUP, POOL_GROUP)),
                  _const_spec((len(POOL_WINDOWS), POOL_GROUP)), _const_spec((1, D_MODEL))],
        out_specs=row,
        out_shape=jax.ShapeDtypeStruct(x.shape, F32),
        scratch_shapes=[pltpu.VMEM((MAX_POOL_WINDOW, D_MODEL), F32)],
        compiler_params=pltpu.CompilerParams(
            dimension_semantics=("parallel", "arbitrary"), vmem_limit_bytes=VMEM_LIMIT),
        name="pool_block",
    )(x, g2.reshape(1, -1), g3.reshape(1, -1), w.astype(BF16), b, scale.reshape(1, -1))


def _proj_body(x_ref, g2_ref, wt_ref, wn_ref, c_ref, sa_ref, sb_ref, ct_ref, st_ref,
               qn_ref, qr_ref, vs_ref, vw_ref, gate_ref, ks_ref, kw_ref, kvc_ref):
    si = pl.program_id(1)
    hb = _rms(x_ref[...], g2_ref[...]).astype(BF16)

    pt = lax.dot_general(wt_ref[...], hb, _NT, preferred_element_type=F32)
    qn_ref[...] = pt[:Q_WIDTH].astype(BF16)
    cos_t, sin_t = ct_ref[...], st_ref[...]
    for h in range(N_HEADS):
        x1 = pt[h * HEAD_DIM:h * HEAD_DIM + ROPE_HALF]
        x2 = pt[h * HEAD_DIM + ROPE_HALF:h * HEAD_DIM + ROPE_DIM]
        rot = jnp.concatenate([x1 * cos_t - x2 * sin_t, x1 * sin_t + x2 * cos_t], axis=0)
        qr_ref[h * ROPE_DIM:(h + 1) * ROPE_DIM, :] = rot.astype(BF16)
    ones_rows = jnp.ones((V_ROWS - HEAD_DIM, SEQ_TILE), F32)
    for g in range(N_KV_GROUPS):
        vs_ref[g] = jnp.concatenate(
            [pt[_ROW_VS + g * HEAD_DIM:_ROW_VS + (g + 1) * HEAD_DIM], ones_rows], axis=0).astype(BF16)
        vw_ref[g] = jnp.concatenate(
            [pt[_ROW_VW + g * HEAD_DIM:_ROW_VW + (g + 1) * HEAD_DIM], ones_rows], axis=0).astype(BF16)
        gate_ref[g] = jax.nn.sigmoid(pt[_ROW_GATE + g * GATE_ROWS:_ROW_GATE + (g + 1) * GATE_ROWS])

    pn = jnp.dot(hb, wn_ref[...], preferred_element_type=F32)
    cos, sin_a, sin_b = c_ref[...], sa_ref[...], sb_ref[...]

    def slab(off, i):
        return pn[:, off + i * LANES: off + (i + 1) * LANES]

    def rope(t):
        return (t * cos + pltpu.roll(t, LANES - ROPE_HALF, axis=1) * sin_a
                + pltpu.roll(t, ROPE_HALF, axis=1) * sin_b)

    pos = si * SEQ_TILE + lax.broadcasted_iota(jnp.int32, (SEQ_TILE, LANES), 0)
    lane = lax.broadcasted_iota(jnp.int32, (SEQ_TILE, LANES), 1)
    blk_onehot = jnp.where(lane - HEAD_DIM == pos // SEL_BLOCK, 1.0, 0.0)
    for g in range(N_KV_GROUPS):
        ks_ref[g] = (rope(slab(0, g)) + blk_onehot).astype(BF16)
        kw_ref[g] = rope(slab(_OFF_KW, g)).astype(BF16)
        kvc_ref[g] = slab(_OFF_KVC, g)


def _nsa_proj_weights(w_in):
    d = w_in.shape[0]
    wq = w_in[:, :Q_WIDTH] * (HEAD_DIM ** -0.5)
    kv = w_in[:, Q_WIDTH:Q_WIDTH + 6 * KV_WIDTH].reshape(d, 6, N_KV_GROUPS, HEAD_DIM)
    wkc, wvc, wks, wvs, wkw, wvw = [kv[:, i] for i in range(6)]
    wgate = w_in[:, Q_WIDTH + 6 * KV_WIDTH:].reshape(d, 3, N_KV_GROUPS, HEADS_PER_GROUP)
    wgate = wgate.transpose(0, 2, 1, 3).reshape(d, N_KV_GROUPS, 3 * HEADS_PER_GROUP)
    wgate = jnp.pad(wgate, ((0, 0), (0, 0), (0, GATE_ROWS - 3 * HEADS_PER_GROUP)))
    w_t = jnp.concatenate([wq, wvs.reshape(d, KV_WIDTH), wvw.reshape(d, KV_WIDTH),
                           wgate.reshape(d, N_KV_GROUPS * GATE_ROWS)], axis=1).T.astype(BF16)
    zero = jnp.zeros_like(wkc)

    def pair(a, b):
        return jnp.concatenate([a, b], axis=-1).reshape(d, _G_COLS)

    w_n = jnp.concatenate([pair(wks * LOG2_E, zero), pair(wkw * LOG2_E, zero), pair(wkc, wvc)],
                          axis=1).astype(BF16)
    return w_t, w_n


def _rope_tables(seq):
    inv = 1.0 / (ROPE_THETA ** (jnp.arange(0, ROPE_DIM, 2, dtype=F32) / ROPE_DIM))
    ang = jnp.arange(seq, dtype=F32)[:, None] * inv[None, :]
    cos, sin = jnp.cos(ang), jnp.sin(ang)
    pad = LANES - ROPE_DIM
    c = jnp.concatenate([cos, cos, jnp.ones((seq, pad), F32)], axis=1)
    sa = jnp.concatenate([-sin, jnp.zeros((seq, ROPE_HALF + pad), F32)], axis=1)
    sb = jnp.concatenate([jnp.zeros((seq, ROPE_HALF), F32), sin, jnp.zeros((seq, pad), F32)], axis=1)
    return c, sa, sb, cos.T, sin.T


def _nsa_proj(x, g2, w_t, w_n, tables):
    bsz, seq, _ = x.shape
    feat = lambda rows, dt: jax.ShapeDtypeStruct((bsz, N_KV_GROUPS, rows, seq), dt)
    feat_spec = lambda rows: pl.BlockSpec((None, N_KV_GROUPS, rows, SEQ_TILE), lambda bi, si: (bi, 0, 0, si))
    tok = lambda dt: jax.ShapeDtypeStruct((bsz, N_KV_GROUPS, seq, LANES), dt)
    tok_spec = pl.BlockSpec((None, N_KV_GROUPS, SEQ_TILE, LANES), lambda bi, si: (bi, 0, si, 0))
    tab_spec = pl.BlockSpec((SEQ_TILE, LANES), lambda bi, si: (si, 0))
    tab_t_spec = pl.BlockSpec((ROPE_HALF, SEQ_TILE), lambda bi, si: (0, si))
    n_rot = N_HEADS * ROPE_DIM
    return pl.pallas_call(
        _proj_body,
        grid=(bsz, seq // SEQ_TILE),
        in_specs=[pl.BlockSpec((None, SEQ_TILE, D_MODEL), lambda bi, si: (bi, si, 0)),
                  _const_spec((1, D_MODEL)), _const_spec((_PROJ_ROWS, D_MODEL)),
                  _const_spec((D_MODEL, _PROJ_COLS)),
                  tab_spec, tab_spec, tab_spec, tab_t_spec, tab_t_spec],
        out_specs=[pl.BlockSpec((None, Q_WIDTH, SEQ_TILE), lambda bi, si: (bi, 0, si)),
                   pl.BlockSpec((None, n_rot, SEQ_TILE), lambda bi, si: (bi, 0, si)),
                   feat_spec(V_ROWS), feat_spec(V_ROWS), feat_spec(GATE_ROWS),
                   tok_spec, tok_spec, tok_spec],
        out_shape=[jax.ShapeDtypeStruct((bsz, Q_WIDTH, seq), BF16),
                   jax.ShapeDtypeStruct((bsz, n_rot, seq), BF16),
                   feat(V_ROWS, BF16), feat(V_ROWS, BF16), feat(GATE_ROWS, F32),
                   tok(BF16), tok(BF16), tok(F32)],
        compiler_params=pltpu.CompilerParams(
            dimension_semantics=("parallel", "parallel"), vmem_limit_bytes=VMEM_LIMIT),
        name="nsa_proj",
    )(x, g2.reshape(1, -1), w_t, w_n, *tables)


def _cmp_body(kvc_ref, pos_ref, w1_ref, w2k_ref, w2vt_ref, kc_ref, vct_ref):
    first = jnp.zeros((N_CMP_PAD, 2 * CMP_HIDDEN), F32)
    second = jnp.zeros((N_CMP_PAD, 2 * CMP_HIDDEN), F32)
    for t in range(CMP_STRIDE):
        xt = kvc_ref[pl.ds(t, N_CMP_PAD, stride=CMP_STRIDE), :]
        za = (xt + pos_ref[t:t + 1, :]).astype(BF16)
        zb = (xt + pos_ref[CMP_STRIDE + t:CMP_STRIDE + t + 1, :]).astype(BF16)
        first = first + jnp.dot(za, w1_ref[t], preferred_element_type=F32)
        second = second + jnp.dot(zb, w1_ref[CMP_STRIDE + t], preferred_element_type=F32)
    row = lax.broadcasted_iota(jnp.int32, second.shape, 0)
    second = jnp.where(row < N_CMP_PAD - 1, pltpu.roll(second, N_CMP_PAD - 1, axis=0), 0.0)
    pre = first + second
    hid = (pre * jax.nn.sigmoid(pre)).astype(BF16)
    kc_ref[...] = jnp.dot(hid[:, :CMP_HIDDEN], w2k_ref[...], preferred_element_type=F32).astype(BF16)
    vct_ref[...] = lax.dot_general(w2vt_ref[...], hid[:, CMP_HIDDEN:], _NT,
                                   preferred_element_type=F32).astype(BF16)


def _compress(kvc, pos_k, pos_v, wk1, wk2, wv1, wv2):
    bsz, _, seq, _ = kvc.shape
    assert seq == N_CMP_PAD * CMP_STRIDE
    pos = jnp.concatenate([pos_k, pos_v], axis=1)
    k1 = wk1.reshape(CMP_BLOCK, HEAD_DIM, CMP_HIDDEN)
    v1 = wv1.reshape(CMP_BLOCK, HEAD_DIM, CMP_HIDDEN)
    z1 = jnp.zeros_like(k1)
    w1 = jnp.concatenate([jnp.concatenate([k1, z1], axis=2),
                          jnp.concatenate([z1, v1], axis=2)], axis=1).astype(BF16)
    w2k = wk2.astype(BF16)
    w2vt = wv2.T.astype(BF16)
    return pl.pallas_call(
        _cmp_body,
        grid=(bsz, N_KV_GROUPS),
        in_specs=[pl.BlockSpec((None, None, seq, LANES), lambda bi, gi: (bi, gi, 0, 0)),
                  _const_spec(pos.shape), _const_spec(w1.shape),
                  _const_spec(w2k.shape), _const_spec(w2vt.shape)],
        out_specs=[pl.BlockSpec((None, None, N_CMP_PAD, HEAD_DIM), lambda bi, gi: (bi, gi, 0, 0)),
                   pl.BlockSpec((None, None, HEAD_DIM, N_CMP_PAD), lambda bi, gi: (bi, gi, 0, 0))],
        out_shape=[jax.ShapeDtypeStruct((bsz, N_KV_GROUPS, N_CMP_PAD, HEAD_DIM), BF16),
                   jax.ShapeDtypeStruct((bsz, N_KV_GROUPS, HEAD_DIM, N_CMP_PAD), BF16)],
        compiler_params=pltpu.CompilerParams(dimension_semantics=("parallel", "parallel")),
        name="nsa_compress",
    )(kvc, pos, w1, w2k, w2vt)


_HEAD_COLS = [slice(r * ATT_Q, (r + 1) * ATT_Q) for r in range(HEADS_PER_GROUP)]


def _score_stage(k_ref, k0, lhs_ref, s_out, t_out=None):
    k_tile = k_ref[pl.ds(k0, ATT_K), :]
    t_max = []
    for cs in _HEAD_COLS:
        s = jnp.dot(k_tile, lhs_ref[:, cs], preferred_element_type=F32)
        s_out[:, cs] = s
        if t_out is not None:
            t_max.append(jnp.max(s, axis=0, keepdims=True))
    if t_out is not None:
        t_out[...] = jnp.concatenate(t_max, axis=1)


def _softmax_stage(s_in, t_in, m_ref, p_out, a_out, mask=None):
    m_all = m_ref[...]
    t_all = t_in[...]
    m_new, alpha = [], []
    for cs in _HEAD_COLS:
        s = s_in[:, cs]
        if mask is not None:
            s = jnp.where(mask, s, NEG_INF)
            t_c = jnp.max(s, axis=0, keepdims=True)
        else:
            t_c = t_all[:, cs]
        m_c = jnp.maximum(m_all[:, cs], t_c)
        p_out[:, cs] = jnp.exp2((s - m_c).astype(BF16))
        alpha.append(jnp.exp2(m_all[:, cs] - m_c))
        m_new.append(m_c)
    m_ref[...] = jnp.concatenate(m_new, axis=1)
    a_out[...] = jnp.concatenate(alpha, axis=1)


def _value_stage(vt_ref, k0, p_in, a_in, acc_ref):
    vt = vt_ref[:, pl.ds(k0, ATT_K)]
    acc_all, a_all = acc_ref[...], a_in[...]
    acc_ref[...] = jnp.concatenate(
        [a_all[:, cs] * acc_all[:, cs] + jnp.dot(vt, p_in[:, cs], preferred_element_type=F32)
         for cs in _HEAD_COLS], axis=1)


def _compress_select(q0, qn_ref, qr_ref, kc_ref, vct_ref, ov_ref, lhs_ref):
    blk_end = lax.broadcasted_iota(jnp.int32, (N_CMP_PAD, ATT_Q), 0) * CMP_STRIDE + CMP_BLOCK - 1
    qpos = q0 + lax.broadcasted_iota(jnp.int32, (N_CMP_PAD, ATT_Q), 1)
    cmp_mask = blk_end <= qpos
    any_cmp = qpos >= CMP_BLOCK - 1
    kc = kc_ref[...]
    vct = vct_ref[...]
    p_sum = jnp.zeros((N_CMP_PAD, ATT_Q), F32)
    o_cmp = []
    for r in range(HEADS_PER_GROUP):
        s = jnp.dot(kc, qn_ref[r * HEAD_DIM:(r + 1) * HEAD_DIM, :], preferred_element_type=F32)
        s = jnp.where(cmp_mask, s, NEG_INF)
        e = jnp.exp(s - jnp.max(s, axis=0, keepdims=True))
        p = jnp.where(any_cmp, e / jnp.sum(e, axis=0, keepdims=True), 0.0)
        p_sum = p_sum + p
        o_cmp.append(jnp.dot(vct, p.astype(BF16), preferred_element_type=F32))

    hi = p_sum.astype(BF16)
    rem = p_sum - hi.astype(F32)
    mid = rem.astype(BF16)
    lo = (rem - mid.astype(F32)).astype(BF16)
    ov = ov_ref[...]
    imp = (jnp.dot(ov, hi, preferred_element_type=F32) + jnp.dot(ov, mid, preferred_element_type=F32)
           + jnp.dot(ov, lo, preferred_element_type=F32))
    j = lax.broadcasted_iota(jnp.int32, (N_SEL_BLOCKS, ATT_Q), 0)
    q_blk = (q0 + lax.broadcasted_iota(jnp.int32, (N_SEL_BLOCKS, ATT_Q), 1)) // SEL_BLOCK
    forced = (j == 0) | (j == q_blk) | (j == q_blk - 1)
    imp = jnp.where(forced, jnp.inf, jnp.where(j > q_blk, -jnp.inf, imp))
    rank = jnp.zeros((N_SEL_BLOCKS, ATT_Q), jnp.int32)
    for jp in range(N_SEL_BLOCKS):
        other = imp[jp:jp + 1, :]
        beats = (other > imp) | ((other == imp) & (j > jp))
        rank = rank + beats.astype(jnp.int32)
    sel_bias = jnp.where(rank < N_SELECT, 0.0, NEG_INF).astype(BF16)
    pad = jnp.zeros((LANES - HEAD_DIM - N_SEL_BLOCKS, ATT_Q), BF16)
    for r in range(HEADS_PER_GROUP):
        lhs_ref[:, r * ATT_Q:(r + 1) * ATT_Q] = jnp.concatenate(
            [qr_ref[r * ROPE_DIM:(r + 1) * ROPE_DIM, :],
             qn_ref[r * HEAD_DIM + ROPE_DIM:(r + 1) * HEAD_DIM, :], sel_bias, pad], axis=0)

    return o_cmp


def _window_branch(qi, q0, prev0, lhs_ref, kw_ref, vwt_ref, sw_ref, pw_ref, accw_ref):
    key_in = lax.broadcasted_iota(jnp.int32, (ATT_K, ATT_Q), 0)
    col_in = lax.broadcasted_iota(jnp.int32, (ATT_K, ATT_Q), 1)
    causal = key_in <= col_in
    prev = (key_in > col_in) & (qi > 0)
    for cs in _HEAD_COLS:
        s_diag = jnp.where(causal, sw_ref[0, :, cs], NEG_INF)
        s_prev = jnp.where(prev, sw_ref[1, :, cs], NEG_INF)
        m_w = jnp.maximum(jnp.max(s_diag, axis=0, keepdims=True), jnp.max(s_prev, axis=0, keepdims=True))
        pw_ref[0, :, cs] = jnp.exp2((s_diag - m_w).astype(BF16))
        pw_ref[1, :, cs] = jnp.exp2((s_prev - m_w).astype(BF16))
    vw_diag = vwt_ref[:, pl.ds(q0, ATT_K)]
    vw_prev = vwt_ref[:, pl.ds(prev0, ATT_K)]
    for cs in _HEAD_COLS:
        accw_ref[:, cs] = (jnp.dot(vw_diag, pw_ref[0, :, cs], preferred_element_type=F32)
                           + jnp.dot(vw_prev, pw_ref[1, :, cs], preferred_element_type=F32))


def _attn_body(qn_ref, qr_ref, kc_ref, vct_ref, ks_ref, kw_ref, vst_ref, vwt_ref, gate_ref, ov_ref,
               o_ref, lhs_ref, s_ref, sw_ref, p_ref, pw_ref, t_ref, a_ref, m_ref, acc_ref, accw_ref):
    qi = pl.program_id(2)
    q0 = pl.multiple_of(qi * ATT_Q, ATT_Q)
    prev0 = pl.multiple_of(jnp.maximum(qi - 1, 0) * ATT_K, ATT_K)
    groups = range(GROUPS_PER_STEP)
    gh = HEADS_PER_GROUP * HEAD_DIM
    gr = HEADS_PER_GROUP * ROPE_DIM

    o_cmp = [_compress_select(q0, qn_ref.at[g * gh:(g + 1) * gh], qr_ref.at[g * gr:(g + 1) * gr],
                              kc_ref.at[g], vct_ref.at[g], ov_ref, lhs_ref.at[g]) for g in groups]

    for g in groups:
        m_ref[g] = jnp.full(m_ref.shape[1:], NEG_INF, F32)
        acc_ref[g] = jnp.zeros(acc_ref.shape[1:], F32)
        p_ref[g, 1] = jnp.zeros(p_ref.shape[2:], BF16)
        a_ref[g, 1] = jnp.ones(a_ref.shape[2:], F32)
    for g in groups:
        _score_stage(ks_ref.at[g], 0, lhs_ref.at[g], s_ref.at[g, 0], t_ref.at[g, 0])
        _score_stage(kw_ref.at[g], q0, lhs_ref.at[g], sw_ref.at[g, 0])
        _score_stage(kw_ref.at[g], prev0, lhs_ref.at[g], sw_ref.at[g, 1])
    for g in groups:
        _window_branch(qi, q0, prev0, lhs_ref.at[g], kw_ref.at[g], vwt_ref.at[g],
                       sw_ref.at[g], pw_ref.at[g], accw_ref.at[g])

    def trip(j, cur):
        nxt = 1 - cur
        for g in groups:
            _softmax_stage(s_ref.at[g, cur], t_ref.at[g, cur], m_ref.at[g], p_ref.at[g, cur], a_ref.at[g, cur])
        for g in groups:
            _score_stage(ks_ref.at[g], pl.multiple_of((j + 1) * ATT_K, ATT_K), lhs_ref.at[g],
                         s_ref.at[g, nxt], t_ref.at[g, nxt])
        for g in groups:
            _value_stage(vst_ref.at[g], pl.multiple_of(jnp.maximum(j - 1, 0) * ATT_K, ATT_K),
                         p_ref.at[g, nxt], a_ref.at[g, nxt], acc_ref.at[g])

    def trip_pair(jj, carry):
        trip(2 * jj, 0)
        trip(2 * jj + 1, 1)
        return carry

    lax.fori_loop(0, qi // 2, trip_pair, 0)

    @pl.when(qi % 2 == 1)
    def _():
        trip(qi - 1, 0)

    key_in = lax.broadcasted_iota(jnp.int32, (ATT_K, ATT_Q), 0)
    col_in = lax.broadcasted_iota(jnp.int32, (ATT_K, ATT_Q), 1)
    causal = key_in <= col_in

    def drain(last):
        for g in groups:
            _softmax_stage(s_ref.at[g, last], t_ref.at[g, last], m_ref.at[g], p_ref.at[g, last],
                           a_ref.at[g, last], mask=causal)
        for g in groups:
            _value_stage(vst_ref.at[g], prev0, p_ref.at[g, 1 - last], a_ref.at[g, 1 - last], acc_ref.at[g])
        for g in groups:
            _value_stage(vst_ref.at[g], q0, p_ref.at[g, last], a_ref.at[g, last], acc_ref.at[g])

    pl.when(qi % 2 == 0)(lambda: drain(0))
    pl.when(qi % 2 == 1)(lambda: drain(1))

    for g in groups:
        gate = gate_ref[g]
        heads = []
        for r, cs in enumerate(_HEAD_COLS):
            g_cmp = gate[r:r + 1, :]
            g_sel = gate[HEADS_PER_GROUP + r:HEADS_PER_GROUP + r + 1, :]
            g_win = gate[2 * HEADS_PER_GROUP + r:2 * HEADS_PER_GROUP + r + 1, :]
            o_sel = acc_ref[g, :HEAD_DIM, cs] / acc_ref[g, HEAD_DIM:HEAD_DIM + 1, cs]
            o_win = accw_ref[g, :HEAD_DIM, cs] / accw_ref[g, HEAD_DIM:HEAD_DIM + 1, cs]
            heads.append(g_cmp * o_cmp[g][r] + g_sel * o_sel + g_win * o_win)
        o_ref[:, g * gh:(g + 1) * gh] = jnp.concatenate(heads, axis=0).T.astype(BF16)


def _overlap(seq):
    ci = jnp.arange(N_CMP_PAD) * CMP_STRIDE
    sj = jnp.arange(seq // SEL_BLOCK) * SEL_BLOCK
    n_cmp = (seq - CMP_BLOCK) // CMP_STRIDE + 1
    ov = ((ci[None, :] < sj[:, None] + SEL_BLOCK) & (ci[None, :] + CMP_BLOCK > sj[:, None])
          & (jnp.arange(N_CMP_PAD)[None, :] < n_cmp))
    return ov.astype(BF16)


def _attention(qn, qr, kc, vct, ks, kw, vst, vwt, gate):
    bsz, _, seq = qn.shape
    assert seq // SEL_BLOCK == N_SEL_BLOCKS and ATT_Q == WINDOW == ATT_K
    gps = GROUPS_PER_STEP
    gh = gps * HEADS_PER_GROUP * HEAD_DIM
    grp = lambda *blk: pl.BlockSpec((None, gps) + blk, lambda bi, gi, qi: (bi, gi, 0, 0))
    return pl.pallas_call(
        _attn_body,
        grid=(bsz, N_KV_GROUPS // gps, seq // ATT_Q),
        in_specs=[pl.BlockSpec((None, gh, ATT_Q), lambda bi, gi, qi: (bi, gi, qi)),
                  pl.BlockSpec((None, gps * HEADS_PER_GROUP * ROPE_DIM, ATT_Q), lambda bi, gi, qi: (bi, gi, qi)),
                  grp(N_CMP_PAD, HEAD_DIM), grp(HEAD_DIM, N_CMP_PAD),
                  grp(seq, LANES), grp(seq, LANES), grp(V_ROWS, seq), grp(V_ROWS, seq),
                  pl.BlockSpec((None, gps, GATE_ROWS, ATT_Q), lambda bi, gi, qi: (bi, gi, 0, qi)),
                  _const_spec((N_SEL_BLOCKS, N_CMP_PAD))],
        out_specs=pl.BlockSpec((None, ATT_Q, gh), lambda bi, gi, qi: (bi, qi, gi)),
        out_shape=jax.ShapeDtypeStruct((bsz, seq, Q_WIDTH), BF16),
        scratch_shapes=[pltpu.VMEM((gps, LANES, ATT_COLS), BF16),
                        pltpu.VMEM((gps, 2, ATT_K, ATT_COLS), F32),
                        pltpu.VMEM((gps, 2, ATT_K, ATT_COLS), F32),
                        pltpu.VMEM((gps, 2, ATT_K, ATT_COLS), BF16),
                        pltpu.VMEM((gps, 2, ATT_K, ATT_COLS), BF16),
                        pltpu.VMEM((gps, 2, 1, ATT_COLS), F32),
                        pltpu.VMEM((gps, 2, 1, ATT_COLS), F32),
                        pltpu.VMEM((gps, 1, ATT_COLS), F32),
                        pltpu.VMEM((gps, V_ROWS, ATT_COLS), F32),
                        pltpu.VMEM((gps, V_ROWS, ATT_COLS), F32)],
        compiler_params=pltpu.CompilerParams(
            dimension_semantics=("parallel", "parallel", "arbitrary"), vmem_limit_bytes=VMEM_LIMIT),
        name="nsa_attention",
    )(qn, qr, kc, vct, ks, kw, vst, vwt, gate, _overlap(seq))


def _oproj_body(o_ref_in, x_ref, w_ref, g_ref, out_ref):
    m = jnp.dot(o_ref_in[...], w_ref[...], preferred_element_type=F32)
    out_ref[...] = x_ref[...] + _rms(m, g_ref[...])


def _out_proj(o2d, x2d, w_o, g3):
    t = x2d.shape[0]
    row = pl.BlockSpec((FFN_TOKENS, D_MODEL), lambda i: (i, 0))
    return pl.pallas_call(
        _oproj_body,
        grid=(t // FFN_TOKENS,),
        in_specs=[row, row, _const_spec((Q_WIDTH, D_MODEL)), _const_spec((1, D_MODEL))],
        out_specs=row,
        out_shape=jax.ShapeDtypeStruct((t, D_MODEL), F32),
        compiler_params=pltpu.CompilerParams(dimension_semantics=("parallel",)),
        name="nsa_out_proj",
    )(o2d, x2d, w_o.astype(BF16), g3.reshape(1, -1))


def _nsa_block(x, g2, g3, w_in, pos_k, pos_v, wk1, wk2, wv1, wv2, w_o):
    bsz, seq, d = x.shape
    w_t, w_n = _nsa_proj_weights(w_in)
    qn, qr, vst, vwt, gate, ks, kw, kvc = _nsa_proj(x, g2, w_t, w_n, _rope_tables(seq))
    kc, vct = _compress(kvc, pos_k, pos_v, wk1, wk2, wv1, wv2)
    o = _attention(qn, qr, kc, vct, ks, kw, vst, vwt, gate)
    return _out_proj(o.reshape(bsz * seq, Q_WIDTH), x.reshape(bsz * seq, d), w_o, g3).reshape(x.shape)


def kernel(x, norm_gains, ffn_w_gate, ffn_w_up, ffn_w_down, pool_w, pool_b, pool_scale, nsa_w_in, nsa_cmp_pos_k, nsa_cmp_pos_v, nsa_cmp_wk1, nsa_cmp_wk2, nsa_cmp_wv1, nsa_cmp_wv2, nsa_w_o):
    bsz, seq, d = x.shape
    depth = norm_gains.shape[0]

    def ffn(x, i, half):
        g = norm_gains[i]
        y = _ffn_block(x.reshape(bsz * seq, d), g[4 * half], g[4 * half + 1],
                       ffn_w_gate[i, half], ffn_w_up[i, half], ffn_w_down[i, half])
        return y.reshape(bsz, seq, d)

    for i in range(depth):
        g = norm_gains[i]
        x = ffn(x, i, 0)
        j = i // 2
        if i % 2 == 0:
            x = _pool_block(x, g[2], g[3], pool_w[j], pool_b[j], pool_scale[j])
        else:
            x = _nsa_block(x, g[2], g[3], nsa_w_in[j], nsa_cmp_pos_k[j], nsa_cmp_pos_v[j],
                           nsa_cmp_wk1[j], nsa_cmp_wk2[j], nsa_cmp_wv1[j], nsa_cmp_wv2[j], nsa_w_o[j])
        x = ffn(x, i, 1)
    return x
```

```python
import functools

import jax
import jax.numpy as jnp
from jax import lax
from jax.experimental import pallas as pl
from jax.experimental.pallas import tpu as pltpu

F32 = jnp.float32
BF16 = jnp.bfloat16

D_MODEL = 1024
EPS = 1e-6
D_FF = 2816
POOL_WINDOWS = (2, 4, 8, 16)
POOL_GROUP = D_MODEL // len(POOL_WINDOWS)
MAX_POOL_WINDOW = max(POOL_WINDOWS)
N_HEADS = 16
HEAD_DIM = 64
N_KV_GROUPS = 4
HEADS_PER_GROUP = N_HEADS // N_KV_GROUPS
ROPE_DIM = HEAD_DIM // 4
ROPE_THETA = 500000.0
CMP_BLOCK = 32
CMP_STRIDE = 16
CMP_HIDDEN = 256
SEL_BLOCK = 64
N_SELECT = 8
WINDOW = 256
Q_WIDTH = N_HEADS * HEAD_DIM
KV_WIDTH = N_KV_GROUPS * HEAD_DIM
GATE_WIDTH = 3 * N_HEADS
NEG_INF = -1e30

LANES = 128
VMEM_LIMIT = 56 * 1024 * 1024

FFN_TOKENS = 512
SEQ_TILE = 512
ATT_Q = 256
ATT_K = 256
N_CMP_PAD = 128
GROUPS_PER_STEP = 2

N_SEL_BLOCKS = 32
GATE_ROWS = 16
ROPE_HALF = ROPE_DIM // 2
V_ROWS = HEAD_DIM + 16
LOG2_E = 1.4426950408889634
ATT_COLS = HEADS_PER_GROUP * ATT_Q
N_CHUNKS = ATT_COLS // LANES

_ROW_VS = Q_WIDTH
_ROW_VW = _ROW_VS + KV_WIDTH
_ROW_GATE = _ROW_VW + KV_WIDTH
_PROJ_ROWS = _ROW_GATE + N_KV_GROUPS * GATE_ROWS
_G_COLS = N_KV_GROUPS * LANES
_OFF_KW = _G_COLS
_OFF_KVC = 2 * _G_COLS
_PROJ_COLS = 3 * _G_COLS

_NT = (((1,), (1,)), ((), ()))


def _rms(x, g):
    return x * lax.rsqrt(jnp.mean(x * x, axis=-1, keepdims=True) + EPS) * g


def _const_spec(shape):
    nd = len(shape)
    return pl.BlockSpec(shape, lambda *_: (0,) * nd, pipeline_mode=pl.Buffered(1))


def _ffn_body(x_ref, gpre_ref, gpost_ref, wg_ref, wu_ref, wd_ref, o_ref):
    x = x_ref[...]
    xb = _rms(x, gpre_ref[...]).astype(BF16)
    hg = jnp.dot(xb, wg_ref[...], preferred_element_type=F32)
    hu = jnp.dot(xb, wu_ref[...], preferred_element_type=F32)
    act = (hg * jax.nn.sigmoid(hg) * hu).astype(BF16)
    f = jnp.dot(act, wd_ref[...], preferred_element_type=F32)
    o_ref[...] = x + 0.5 * _rms(f, gpost_ref[...])


def _ffn_block(x2d, g_pre, g_post, wg, wu, wd):
    t = x2d.shape[0]
    row = pl.BlockSpec((FFN_TOKENS, D_MODEL), lambda i: (i, 0))
    return pl.pallas_call(
        _ffn_body,
        grid=(t // FFN_TOKENS,),
        in_specs=[row, _const_spec((1, D_MODEL)), _const_spec((1, D_MODEL)),
                  _const_spec((D_MODEL, D_FF)), _const_spec((D_MODEL, D_FF)),
                  _const_spec((D_FF, D_MODEL))],
        out_specs=row,
        out_shape=jax.ShapeDtypeStruct((t, D_MODEL), F32),
        compiler_params=pltpu.CompilerParams(
            dimension_semantics=("parallel",), vmem_limit_bytes=VMEM_LIMIT),
        name="ffn_block",
    )(x2d, g_pre.reshape(1, -1), g_post.reshape(1, -1),
      wg.astype(BF16), wu.astype(BF16), wd.astype(BF16))


def _pool_body(x_ref, g2_ref, g3_ref, w_ref, b_ref, sc_ref, o_ref, carry_ref):
    si = pl.program_id(1)

    @pl.when(si == 0)
    def _():
        carry_ref[...] = jnp.zeros_like(carry_ref)

    x = x_ref[...]
    h = _rms(x, g2_ref[...])
    hp = jnp.concatenate([carry_ref[...], h], axis=0)
    carry_ref[...] = h[SEQ_TILE - MAX_POOL_WINDOW:, :]
    pos = si * SEQ_TILE + lax.broadcasted_iota(jnp.int32, (SEQ_TILE, 1), 0)
    ys = []
    for g, w in enumerate(POOL_WINDOWS):
        cols = slice(g * POOL_GROUP, (g + 1) * POOL_GROUP)
        acc = hp[:, cols]
        k = 1
        while k < w:
            acc = acc + pltpu.roll(acc, k, axis=0)
            k *= 2
        cnt = jnp.minimum(pos + 1, w).astype(F32)
        d = acc[MAX_POOL_WINDOW:, :] / cnt - h[:, cols]
        y = jnp.dot(d.astype(BF16), w_ref[g], preferred_element_type=F32) + b_ref[g:g + 1, :]
        ys.append(y)
    m = jnp.concatenate(ys, axis=1) * sc_ref[...]
    o_ref[...] = x + _rms(m, g3_ref[...])


def _pool_block(x, g2, g3, w, b, scale):
    bsz, seq, _ = x.shape
    row = pl.BlockSpec((None, SEQ_TILE, D_MODEL), lambda bi, si: (bi, si, 0))
    return pl.pallas_call(
        _pool_body,
        grid=(bsz, seq // SEQ_TILE),
        in_specs=[row, _const_spec((1, D_MODEL)), _const_spec((1, D_MODEL)),
                  _const_spec((len(POOL_WINDOWS), POOL_GROUP, POOL_GROUP)),
                  _const_spec((len(POOL_WINDOWS), POOL_GROUP)), _const_spec((1, D_MODEL))],
        out_specs=row,
        out_shape=jax.ShapeDtypeStruct(x.shape, F32),
        scratch_shapes=[pltpu.VMEM((MAX_POOL_WINDOW, D_MODEL), F32)],
        compiler_params=pltpu.CompilerParams(
            dimension_semantics=("parallel", "arbitrary"), vmem_limit_bytes=VMEM_LIMIT),
        name="pool_block",
    )(x, g2.reshape(1, -1), g3.reshape(1, -1), w.astype(BF16), b, scale.reshape(1, -1))


def _proj_body(x_ref, g2_ref, wt_ref, wn_ref, c_ref, sa_ref, sb_ref, ct_ref, st_ref,
               qn_ref, qr_ref, vs_ref, vw_ref, gate_ref, ks_ref, kw_ref, kvc_ref):
    si = pl.program_id(1)
    hb = _rms(x_ref[...], g2_ref[...]).astype(BF16)

    pt = lax.dot_general(wt_ref[...], hb, _NT, preferred_element_type=F32)
    qn_ref[...] = pt[:Q_WIDTH].astype(BF16)
    cos_t, sin_t = ct_ref[...], st_ref[...]
    for h in range(N_HEADS):
        x1 = pt[h * HEAD_DIM:h * HEAD_DIM + ROPE_HALF]
        x2 = pt[h * HEAD_DIM + ROPE_HALF:h * HEAD_DIM + ROPE_DIM]
        rot = jnp.concatenate([x1 * cos_t - x2 * sin_t, x1 * sin_t + x2 * cos_t], axis=0)
        qr_ref[h * ROPE_DIM:(h + 1) * ROPE_DIM, :] = rot.astype(BF16)
    ones_rows = jnp.ones((V_ROWS - HEAD_DIM, SEQ_TILE), F32)
    for g in range(N_KV_GROUPS):
        vs_ref[g] = jnp.concatenate(
            [pt[_ROW_VS + g * HEAD_DIM:_ROW_VS + (g + 1) * HEAD_DIM], ones_rows], axis=0).astype(BF16)
        vw_ref[g] = jnp.concatenate(
            [pt[_ROW_VW + g * HEAD_DIM:_ROW_VW + (g + 1) * HEAD_DIM], ones_rows], axis=0).astype(BF16)
        gate_ref[g] = jax.nn.sigmoid(pt[_ROW_GATE + g * GATE_ROWS:_ROW_GATE + (g + 1) * GATE_ROWS])

    pn = jnp.dot(hb, wn_ref[...], preferred_element_type=F32)
    cos, sin_a, sin_b = c_ref[...], sa_ref[...], sb_ref[...]

    def slab(off, i):
        return pn[:, off + i * LANES: off + (i + 1) * LANES]

    def rope(t):
        return (t * cos + pltpu.roll(t, LANES - ROPE_HALF, axis=1) * sin_a
                + pltpu.roll(t, ROPE_HALF, axis=1) * sin_b)

    pos = si * SEQ_TILE + lax.broadcasted_iota(jnp.int32, (SEQ_TILE, LANES), 0)
    lane = lax.broadcasted_iota(jnp.int32, (SEQ_TILE, LANES), 1)
    blk_onehot = jnp.where(lane - HEAD_DIM == pos // SEL_BLOCK, 1.0, 0.0)
    for g in range(N_KV_GROUPS):
        ks_ref[g] = (rope(slab(0, g)) + blk_onehot).astype(BF16)
        kw_ref[g] = rope(slab(_OFF_KW, g)).astype(BF16)
        kvc_ref[g] = slab(_OFF_KVC, g)


def _nsa_proj_weights(w_in):
    d = w_in.shape[0]
    wq = w_in[:, :Q_WIDTH] * (HEAD_DIM ** -0.5)
    kv = w_in[:, Q_WIDTH:Q_WIDTH + 6 * KV_WIDTH].reshape(d, 6, N_KV_GROUPS, HEAD_DIM)
    wkc, wvc, wks, wvs, wkw, wvw = [kv[:, i] for i in range(6)]
    wgate = w_in[:, Q_WIDTH + 6 * KV_WIDTH:].reshape(d, 3, N_KV_GROUPS, HEADS_PER_GROUP)
    wgate = wgate.transpose(0, 2, 1, 3).reshape(d, N_KV_GROUPS, 3 * HEADS_PER_GROUP)
    wgate = jnp.pad(wgate, ((0, 0), (0, 0), (0, GATE_ROWS - 3 * HEADS_PER_GROUP)))
    w_t = jnp.concatenate([wq, wvs.reshape(d, KV_WIDTH), wvw.reshape(d, KV_WIDTH),
                           wgate.reshape(d, N_KV_GROUPS * GATE_ROWS)], axis=1).T.astype(BF16)
    zero = jnp.zeros_like(wkc)

    def pair(a, b):
        return jnp.concatenate([a, b], axis=-1).reshape(d, _G_COLS)

    w_n = jnp.concatenate([pair(wks * LOG2_E, zero), pair(wkw * LOG2_E, zero), pair(wkc, wvc)],
                          axis=1).astype(BF16)
    return w_t, w_n


def _rope_tables(seq):
    inv = 1.0 / (ROPE_THETA ** (jnp.arange(0, ROPE_DIM, 2, dtype=F32) / ROPE_DIM))
    ang = jnp.arange(seq, dtype=F32)[:, None] * inv[None, :]
    cos, sin = jnp.cos(ang), jnp.sin(ang)
    pad = LANES - ROPE_DIM
    c = jnp.concatenate([cos, cos, jnp.ones((seq, pad), F32)], axis=1)
    sa = jnp.concatenate([-sin, jnp.zeros((seq, ROPE_HALF + pad), F32)], axis=1)
    sb = jnp.concatenate([jnp.zeros((seq, ROPE_HALF), F32), sin, jnp.zeros((seq, pad), F32)], axis=1)
    return c, sa, sb, cos.T, sin.T


def _nsa_proj(x, g2, w_t, w_n, tables):
    bsz, seq, _ = x.shape
    feat = lambda rows, dt: jax.ShapeDtypeStruct((bsz, N_KV_GROUPS, rows, seq), dt)
    feat_spec = lambda rows: pl.BlockSpec((None, N_KV_GROUPS, rows, SEQ_TILE), lambda bi, si: (bi, 0, 0, si))
    tok = lambda dt: jax.ShapeDtypeStruct((bsz, N_KV_GROUPS, seq, LANES), dt)
    tok_spec = pl.BlockSpec((None, N_KV_GROUPS, SEQ_TILE, LANES), lambda bi, si: (bi, 0, si, 0))
    tab_spec = pl.BlockSpec((SEQ_TILE, LANES), lambda bi, si: (si, 0))
    tab_t_spec = pl.BlockSpec((ROPE_HALF, SEQ_TILE), lambda bi, si: (0, si))
    n_rot = N_HEADS * ROPE_DIM
    return pl.pallas_call(
        _proj_body,
        grid=(bsz, seq // SEQ_TILE),
        in_specs=[pl.BlockSpec((None, SEQ_TILE, D_MODEL), lambda bi, si: (bi, si, 0)),
                  _const_spec((1, D_MODEL)), _const_spec((_PROJ_ROWS, D_MODEL)),
                  _const_spec((D_MODEL, _PROJ_COLS)),
                  tab_spec, tab_spec, tab_spec, tab_t_spec, tab_t_spec],
        out_specs=[pl.BlockSpec((None, Q_WIDTH, SEQ_TILE), lambda bi, si: (bi, 0, si)),
                   pl.BlockSpec((None, n_rot, SEQ_TILE), lambda bi, si: (bi, 0, si)),
                   feat_spec(V_ROWS), feat_spec(V_ROWS), feat_spec(GATE_ROWS),
                   tok_spec, tok_spec, tok_spec],
        out_shape=[jax.ShapeDtypeStruct((bsz, Q_WIDTH, seq), BF16),
                   jax.ShapeDtypeStruct((bsz, n_rot, seq), BF16),
                   feat(V_ROWS, BF16), feat(V_ROWS, BF16), feat(GATE_ROWS, F32),
                   tok(BF16), tok(BF16), tok(F32)],
        compiler_params=pltpu.CompilerParams(
            dimension_semantics=("parallel", "parallel"), vmem_limit_bytes=VMEM_LIMIT),
        name="nsa_proj",
    )(x, g2.reshape(1, -1), w_t, w_n, *tables)


def _cmp_body(kvc_ref, pos_ref, w1_ref, w2k_ref, w2vt_ref, kc_ref, vct_ref):
    first = jnp.zeros((N_CMP_PAD, 2 * CMP_HIDDEN), F32)
    second = jnp.zeros((N_CMP_PAD, 2 * CMP_HIDDEN), F32)
    for t in range(CMP_STRIDE):
        xt = kvc_ref[pl.ds(t, N_CMP_PAD, stride=CMP_STRIDE), :]
        za = (xt + pos_ref[t:t + 1, :]).astype(BF16)
        zb = (xt + pos_ref[CMP_STRIDE + t:CMP_STRIDE + t + 1, :]).astype(BF16)
        first = first + jnp.dot(za, w1_ref[t], preferred_element_type=F32)
        second = second + jnp.dot(zb, w1_ref[CMP_STRIDE + t], preferred_element_type=F32)
    row = lax.broadcasted_iota(jnp.int32, second.shape, 0)
    second = jnp.where(row < N_CMP_PAD - 1, pltpu.roll(second, N_CMP_PAD - 1, axis=0), 0.0)
    pre = first + second
    hid = (pre * jax.nn.sigmoid(pre)).astype(BF16)
    kc_ref[...] = jnp.dot(hid[:, :CMP_HIDDEN], w2k_ref[...], preferred_element_type=F32).astype(BF16)
    vct_ref[...] = lax.dot_general(w2vt_ref[...], hid[:, CMP_HIDDEN:], _NT,
                                   preferred_element_type=F32).astype(BF16)


def _compress(kvc, pos_k, pos_v, wk1, wk2, wv1, wv2):
    bsz, _, seq, _ = kvc.shape
    assert seq == N_CMP_PAD * CMP_STRIDE
    pos = jnp.concatenate([pos_k, pos_v], axis=1)
    k1 = wk1.reshape(CMP_BLOCK, HEAD_DIM, CMP_HIDDEN)
    v1 = wv1.reshape(CMP_BLOCK, HEAD_DIM, CMP_HIDDEN)
    z1 = jnp.zeros_like(k1)
    w1 = jnp.concatenate([jnp.concatenate([k1, z1], axis=2),
                          jnp.concatenate([z1, v1], axis=2)], axis=1).astype(BF16)
    w2k = (wk2 * LOG2_E).astype(BF16)
    w2vt = wv2.T.astype(BF16)
    return pl.pallas_call(
        _cmp_body,
        grid=(bsz, N_KV_GROUPS),
        in_specs=[pl.BlockSpec((None, None, seq, LANES), lambda bi, gi: (bi, gi, 0, 0)),
                  _const_spec(pos.shape), _const_spec(w1.shape),
                  _const_spec(w2k.shape), _const_spec(w2vt.shape)],
        out_specs=[pl.BlockSpec((None, None, N_CMP_PAD, HEAD_DIM), lambda bi, gi: (bi, gi, 0, 0)),
                   pl.BlockSpec((None, None, HEAD_DIM, N_CMP_PAD), lambda bi, gi: (bi, gi, 0, 0))],
        out_shape=[jax.ShapeDtypeStruct((bsz, N_KV_GROUPS, N_CMP_PAD, HEAD_DIM), BF16),
                   jax.ShapeDtypeStruct((bsz, N_KV_GROUPS, HEAD_DIM, N_CMP_PAD), BF16)],
        compiler_params=pltpu.CompilerParams(dimension_semantics=("parallel", "parallel")),
        name="nsa_compress",
    )(kvc, pos, w1, w2k, w2vt)


_HEAD_COLS = [slice(r * ATT_Q, (r + 1) * ATT_Q) for r in range(HEADS_PER_GROUP)]


def _score_stage(k_ref, k0, lhs_ref, s_out, t_out=None):
    k_tile = k_ref[pl.ds(k0, ATT_K), :]
    t_max = []
    for cs in _HEAD_COLS:
        s = jnp.dot(k_tile, lhs_ref[:, cs], preferred_element_type=F32)
        s_out[:, cs] = s
        if t_out is not None:
            t_max.append(jnp.max(s, axis=0, keepdims=True))
    if t_out is not None:
        t_out[...] = jnp.concatenate(t_max, axis=1)


def _softmax_stage(s_in, t_in, m_ref, p_out, a_out, mask=None):
    m_all = m_ref[...]
    t_all = t_in[...]
    m_new, alpha = [], []
    for cs in _HEAD_COLS:
        s = s_in[:, cs]
        if mask is not None:
            s = jnp.where(mask, s, NEG_INF)
            t_c = jnp.max(s, axis=0, keepdims=True)
        else:
            t_c = t_all[:, cs]
        m_c = jnp.maximum(m_all[:, cs], t_c)
        p_out[:, cs] = jnp.exp2((s - m_c).astype(BF16))
        alpha.append(jnp.exp2(m_all[:, cs] - m_c))
        m_new.append(m_c)
    m_ref[...] = jnp.concatenate(m_new, axis=1)
    a_out[...] = jnp.concatenate(alpha, axis=1)


def _value_stage(vt_ref, k0, p_in, a_in, acc_ref):
    vt = vt_ref[:, pl.ds(k0, ATT_K)]
    acc_all, a_all = acc_ref[...], a_in[...]
    acc_ref[...] = jnp.concatenate(
        [a_all[:, cs] * acc_all[:, cs] + jnp.dot(vt, p_in[:, cs], preferred_element_type=F32)
         for cs in _HEAD_COLS], axis=1)


def _tree_sum(terms):
    while len(terms) > 1:
        terms = [a + b for a, b in zip(terms[::2], terms[1::2])] + terms[len(terms) & ~1:]
    return terms[0]


def _select_body(qn_ref, kc_ref, vct_ref, ov_ref, bias_ref, ocmp_ref):
    q0 = pl.program_id(1) * ATT_Q
    heads = [(g, r) for g in range(N_KV_GROUPS) for r in range(HEADS_PER_GROUP)]
    rows = lambda g, r: slice((g * HEADS_PER_GROUP + r) * HEAD_DIM, (g * HEADS_PER_GROUP + r + 1) * HEAD_DIM)

    blk_end = lax.broadcasted_iota(jnp.int32, (N_CMP_PAD, ATT_Q), 0) * CMP_STRIDE + CMP_BLOCK - 1
    qpos = q0 + lax.broadcasted_iota(jnp.int32, (N_CMP_PAD, ATT_Q), 1)
    cmp_mask = blk_end <= qpos
    any_cmp = q0 + lax.broadcasted_iota(jnp.int32, (1, ATT_Q), 1) >= CMP_BLOCK - 1
    scores = [jnp.dot(kc_ref[g], qn_ref[rows(g, r), :], preferred_element_type=F32) for g, r in heads]
    probs = []
    for s in scores:
        s = jnp.where(cmp_mask, s, NEG_INF)
        e = jnp.exp2(s - jnp.max(s, axis=0, keepdims=True))
        probs.append(e * jnp.where(any_cmp, 1.0 / jnp.sum(e, axis=0, keepdims=True), 0.0))
    for (g, r), p in zip(heads, probs):
        ocmp_ref[rows(g, r), :] = jnp.dot(vct_ref[g], p.astype(BF16), preferred_element_type=F32)

    p_sum = jnp.concatenate(
        [_tree_sum(probs[g * HEADS_PER_GROUP:(g + 1) * HEADS_PER_GROUP]) for g in range(N_KV_GROUPS)], axis=1)
    hi = p_sum.astype(BF16)
    rem = p_sum - hi.astype(F32)
    mid = rem.astype(BF16)
    lo = (rem - mid.astype(F32)).astype(BF16)
    ov = ov_ref[...]
    imp = (jnp.dot(ov, hi, preferred_element_type=F32) + jnp.dot(ov, mid, preferred_element_type=F32)
           + jnp.dot(ov, lo, preferred_element_type=F32))
    cols = N_KV_GROUPS * ATT_Q
    j = lax.broadcasted_iota(jnp.int32, (N_SEL_BLOCKS, cols), 0)
    q_in = lax.broadcasted_iota(jnp.int32, (N_SEL_BLOCKS, cols), 1) & (ATT_Q - 1)
    q_blk = (q0 + q_in) // SEL_BLOCK
    forced = (j == 0) | (j == q_blk) | (j == q_blk - 1)
    imp = jnp.where(forced, jnp.inf, jnp.where(j > q_blk, -jnp.inf, imp))
    sub = 8
    row_in = lax.broadcasted_iota(jnp.int32, (sub, cols), 0)
    bias = []
    for b0 in range(0, N_SEL_BLOCKS, sub):
        mine = imp[b0:b0 + sub, :]
        beats = []
        for jp in range(N_SEL_BLOCKS):
            other = imp[jp:jp + 1, :]
            ge = jnp.where(other >= mine, 1.0, 0.0)
            gt = jnp.where(other > mine, 1.0, 0.0)
            if jp < b0:
                beats.append(ge)
            elif jp >= b0 + sub:
                beats.append(gt)
            else:
                beats.append(jnp.where(row_in > jp - b0, ge, gt))
        bias.append(jnp.where(_tree_sum(beats) < N_SELECT, 0.0, NEG_INF))
    bias = jnp.concatenate(bias, axis=0).astype(BF16)
    for g in range(N_KV_GROUPS):
        bias_ref[g] = bias[:, g * ATT_Q:(g + 1) * ATT_Q]


def _select(qn, kc, vct):
    bsz, _, seq = qn.shape
    assert seq // SEL_BLOCK == N_SEL_BLOCKS
    grp = lambda *blk: pl.BlockSpec((None, N_KV_GROUPS) + blk, lambda bi, qi: (bi, 0, 0, 0))
    q_spec = pl.BlockSpec((None, Q_WIDTH, ATT_Q), lambda bi, qi: (bi, 0, qi))
    return pl.pallas_call(
        _select_body,
        grid=(bsz, seq // ATT_Q),
        in_specs=[q_spec, grp(N_CMP_PAD, HEAD_DIM), grp(HEAD_DIM, N_CMP_PAD),
                  _const_spec((N_SEL_BLOCKS, N_CMP_PAD))],
        out_specs=[pl.BlockSpec((None, N_KV_GROUPS, N_SEL_BLOCKS, ATT_Q), lambda bi, qi: (bi, 0, 0, qi)),
                   q_spec],
        out_shape=[jax.ShapeDtypeStruct((bsz, N_KV_GROUPS, N_SEL_BLOCKS, seq), BF16),
                   jax.ShapeDtypeStruct((bsz, Q_WIDTH, seq), F32)],
        compiler_params=pltpu.CompilerParams(dimension_semantics=("parallel", "parallel")),
        name="nsa_select",
    )(qn, kc, vct, _overlap(seq))


def _window_branch(qi, q0, prev0, kw_ref, vwt_ref, sw_ref, pw_ref, accw_ref):
    key_in = lax.broadcasted_iota(jnp.int32, (ATT_K, ATT_Q), 0)
    col_in = lax.broadcasted_iota(jnp.int32, (ATT_K, ATT_Q), 1)
    causal = key_in <= col_in
    prev = (key_in > col_in) & (qi > 0)
    for cs in _HEAD_COLS:
        s_diag = jnp.where(causal, sw_ref[0, :, cs], NEG_INF)
        s_prev = jnp.where(prev, sw_ref[1, :, cs], NEG_INF)
        m_w = jnp.maximum(jnp.max(s_diag, axis=0, keepdims=True), jnp.max(s_prev, axis=0, keepdims=True))
        pw_ref[0, :, cs] = jnp.exp2((s_diag - m_w).astype(BF16))
        pw_ref[1, :, cs] = jnp.exp2((s_prev - m_w).astype(BF16))
    vw_diag = vwt_ref[:, pl.ds(q0, ATT_K)]
    vw_prev = vwt_ref[:, pl.ds(prev0, ATT_K)]
    for cs in _HEAD_COLS:
        accw_ref[:, cs] = (jnp.dot(vw_diag, pw_ref[0, :, cs], preferred_element_type=F32)
                           + jnp.dot(vw_prev, pw_ref[1, :, cs], preferred_element_type=F32))


def _attn_body(qn_ref, qr_ref, bias_ref, ocmp_ref, ks_ref, kw_ref, vst_ref, vwt_ref, gate_ref,
               o_ref, lhs_ref, s_ref, sw_ref, p_ref, pw_ref, t_ref, a_ref, m_ref, acc_ref, accw_ref):
    qi = pl.program_id(2)
    q0 = pl.multiple_of(qi * ATT_Q, ATT_Q)
    prev0 = pl.multiple_of(jnp.maximum(qi - 1, 0) * ATT_K, ATT_K)
    groups = range(GROUPS_PER_STEP)
    gh = HEADS_PER_GROUP * HEAD_DIM

    pad = jnp.zeros((LANES - HEAD_DIM - N_SEL_BLOCKS, ATT_Q), BF16)
    for g in groups:
        for r in range(HEADS_PER_GROUP):
            h = g * HEADS_PER_GROUP + r
            lhs_ref[g, :, r * ATT_Q:(r + 1) * ATT_Q] = jnp.concatenate(
                [qr_ref[h * ROPE_DIM:(h + 1) * ROPE_DIM, :],
                 qn_ref[h * HEAD_DIM + ROPE_DIM:(h + 1) * HEAD_DIM, :], bias_ref[g], pad], axis=0)

    for g in groups:
        _score_stage(kw_ref.at[g], q0, lhs_ref.at[g], sw_ref.at[g, 0])
        _score_stage(kw_ref.at[g], prev0, lhs_ref.at[g], sw_ref.at[g, 1])
    for g in groups:
        _window_branch(qi, q0, prev0, kw_ref.at[g], vwt_ref.at[g], sw_ref.at[g], pw_ref.at[g], accw_ref.at[g])

    for g in groups:
        m_ref[g] = jnp.full(m_ref.shape[1:], NEG_INF, F32)
        acc_ref[g] = jnp.zeros(acc_ref.shape[1:], F32)
        p_ref[g, 1] = jnp.zeros(p_ref.shape[2:], BF16)
        a_ref[g, 1] = jnp.ones(a_ref.shape[2:], F32)
    for g in groups:
        _score_stage(ks_ref.at[g], 0, lhs_ref.at[g], s_ref.at[g, 0], t_ref.at[g, 0])

    def trip(j, cur):
        nxt = 1 - cur
        for g in groups:
            _softmax_stage(s_ref.at[g, cur], t_ref.at[g, cur], m_ref.at[g], p_ref.at[g, cur], a_ref.at[g, cur])
        for g in groups:
            _score_stage(ks_ref.at[g], pl.multiple_of((j + 1) * ATT_K, ATT_K), lhs_ref.at[g],
                         s_ref.at[g, nxt], t_ref.at[g, nxt])
        for g in groups:
            _value_stage(vst_ref.at[g], pl.multiple_of(jnp.maximum(j - 1, 0) * ATT_K, ATT_K),
                         p_ref.at[g, nxt], a_ref.at[g, nxt], acc_ref.at[g])

    def trip_pair(jj, carry):
        trip(2 * jj, 0)
        trip(2 * jj + 1, 1)
        return carry

    lax.fori_loop(0, qi // 2, trip_pair, 0)

    @pl.when(qi % 2 == 1)
    def _():
        trip(qi - 1, 0)

    key_in = lax.broadcasted_iota(jnp.int32, (ATT_K, ATT_Q), 0)
    col_in = lax.broadcasted_iota(jnp.int32, (ATT_K, ATT_Q), 1)
    causal = key_in <= col_in

    def drain(last):
        for g in groups:
            _softmax_stage(s_ref.at[g, last], t_ref.at[g, last], m_ref.at[g], p_ref.at[g, last],
                           a_ref.at[g, last], mask=causal)
        for g in groups:
            _value_stage(vst_ref.at[g], prev0, p_ref.at[g, 1 - last], a_ref.at[g, 1 - last], acc_ref.at[g])
        for g in groups:
            _value_stage(vst_ref.at[g], q0, p_ref.at[g, last], a_ref.at[g, last], acc_ref.at[g])

    pl.when(qi % 2 == 0)(lambda: drain(0))
    pl.when(qi % 2 == 1)(lambda: drain(1))

    for g in groups:
        gate = gate_ref[g]
        heads = []
        for r, cs in enumerate(_HEAD_COLS):
            g_cmp = gate[r:r + 1, :]
            g_sel = gate[HEADS_PER_GROUP + r:HEADS_PER_GROUP + r + 1, :]
            g_win = gate[2 * HEADS_PER_GROUP + r:2 * HEADS_PER_GROUP + r + 1, :]
            o_sel = acc_ref[g, :HEAD_DIM, cs] / acc_ref[g, HEAD_DIM:HEAD_DIM + 1, cs]
            o_win = accw_ref[g, :HEAD_DIM, cs] / accw_ref[g, HEAD_DIM:HEAD_DIM + 1, cs]
            h = g * HEADS_PER_GROUP + r
            o_cmp = ocmp_ref[h * HEAD_DIM:(h + 1) * HEAD_DIM, :]
            heads.append(g_cmp * o_cmp + g_sel * o_sel + g_win * o_win)
        o_ref[:, g * gh:(g + 1) * gh] = jnp.concatenate(heads, axis=0).T.astype(BF16)


def _overlap(seq):
    ci = jnp.arange(N_CMP_PAD) * CMP_STRIDE
    sj = jnp.arange(seq // SEL_BLOCK) * SEL_BLOCK
    n_cmp = (seq - CMP_BLOCK) // CMP_STRIDE + 1
    ov = ((ci[None, :] < sj[:, None] + SEL_BLOCK) & (ci[None, :] + CMP_BLOCK > sj[:, None])
          & (jnp.arange(N_CMP_PAD)[None, :] < n_cmp))
    return ov.astype(BF16)


def _attention(qn, qr, bias, ocmp, ks, kw, vst, vwt, gate):
    bsz, _, seq = qn.shape
    assert ATT_Q == WINDOW == ATT_K
    gps = GROUPS_PER_STEP
    gh = gps * HEADS_PER_GROUP * HEAD_DIM
    grp = lambda *blk: pl.BlockSpec((None, gps) + blk, lambda bi, gi, qi: (bi, gi, 0, 0))
    q_spec = pl.BlockSpec((None, gh, ATT_Q), lambda bi, gi, qi: (bi, gi, qi))
    return pl.pallas_call(
        _attn_body,
        grid=(bsz, N_KV_GROUPS // gps, seq // ATT_Q),
        in_specs=[q_spec,
                  pl.BlockSpec((None, gps * HEADS_PER_GROUP * ROPE_DIM, ATT_Q), lambda bi, gi, qi: (bi, gi, qi)),
                  pl.BlockSpec((None, gps, N_SEL_BLOCKS, ATT_Q), lambda bi, gi, qi: (bi, gi, 0, qi)),
                  q_spec,
                  grp(seq, LANES), grp(seq, LANES), grp(V_ROWS, seq), grp(V_ROWS, seq),
                  pl.BlockSpec((None, gps, GATE_ROWS, ATT_Q), lambda bi, gi, qi: (bi, gi, 0, qi))],
        out_specs=pl.BlockSpec((None, ATT_Q, gh), lambda bi, gi, qi: (bi, qi, gi)),
        out_shape=jax.ShapeDtypeStruct((bsz, seq, Q_WIDTH), BF16),
        scratch_shapes=[pltpu.VMEM((gps, LANES, ATT_COLS), BF16),
                        pltpu.VMEM((gps, 2, ATT_K, ATT_COLS), F32),
                        pltpu.VMEM((gps, 2, ATT_K, ATT_COLS), F32),
                        pltpu.VMEM((gps, 2, ATT_K, ATT_COLS), BF16),
                        pltpu.VMEM((gps, 2, ATT_K, ATT_COLS), BF16),
                        pltpu.VMEM((gps, 2, 1, ATT_COLS), F32),
                        pltpu.VMEM((gps, 2, 1, ATT_COLS), F32),
                        pltpu.VMEM((gps, 1, ATT_COLS), F32),
                        pltpu.VMEM((gps, V_ROWS, ATT_COLS), F32),
                        pltpu.VMEM((gps, V_ROWS, ATT_COLS), F32)],
        compiler_params=pltpu.CompilerParams(
            dimension_semantics=("parallel", "parallel", "arbitrary"), vmem_limit_bytes=VMEM_LIMIT),
        name="nsa_attention",
    )(qn, qr, bias, ocmp, ks, kw, vst, vwt, gate)


def _oproj_body(o_ref_in, x_ref, w_ref, g_ref, out_ref):
    m = jnp.dot(o_ref_in[...], w_ref[...], preferred_element_type=F32)
    out_ref[...] = x_ref[...] + _rms(m, g_ref[...])


def _out_proj(o2d, x2d, w_o, g3):
    t = x2d.shape[0]
    row = pl.BlockSpec((FFN_TOKENS, D_MODEL), lambda i: (i, 0))
    return pl.pallas_call(
        _oproj_body,
        grid=(t // FFN_TOKENS,),
        in_specs=[row, row, _const_spec((Q_WIDTH, D_MODEL)), _const_spec((1, D_MODEL))],
        out_specs=row,
        out_shape=jax.ShapeDtypeStruct((t, D_MODEL), F32),
        compiler_params=pltpu.CompilerParams(dimension_semantics=("parallel",)),
        name="nsa_out_proj",
    )(o2d, x2d, w_o.astype(BF16), g3.reshape(1, -1))


def _nsa_block(x, g2, g3, w_in, pos_k, pos_v, wk1, wk2, wv1, wv2, w_o):
    bsz, seq, d = x.shape
    w_t, w_n = _nsa_proj_weights(w_in)
    qn, qr, vst, vwt, gate, ks, kw, kvc = _nsa_proj(x, g2, w_t, w_n, _rope_tables(seq))
    kc, vct = _compress(kvc, pos_k, pos_v, wk1, wk2, wv1, wv2)
    bias, ocmp = _select(qn, kc, vct)
    o = _attention(qn, qr, bias, ocmp, ks, kw, vst, vwt, gate)
    return _out_proj(o.reshape(bsz * seq, Q_WIDTH), x.reshape(bsz * seq, d), w_o, g3).reshape(x.shape)


def kernel(x, norm_gains, ffn_w_gate, ffn_w_up, ffn_w_down, pool_w, pool_b, pool_scale, nsa_w_in, nsa_cmp_pos_k, nsa_cmp_pos_v, nsa_cmp_wk1, nsa_cmp_wk2, nsa_cmp_wv1, nsa_cmp_wv2, nsa_w_o):
    bsz, seq, d = x.shape
    depth = norm_gains.shape[0]

    def ffn(x, i, half):
        g = norm_gains[i]
        y = _ffn_block(x.reshape(bsz * seq, d), g[4 * half], g[4 * half + 1],
                       ffn_w_gate[i, half], ffn_w_up[i, half], ffn_w_down[i, half])
        return y.reshape(bsz, seq, d)

    for i in range(depth):
        g = norm_gains[i]
        x = ffn(x, i, 0)
        j = i // 2
        if i % 2 == 0:
            x = _pool_block(x, g[2], g[3], pool_w[j], pool_b[j], pool_scale[j])
        else:
            x = _nsa_block(x, g[2], g[3], nsa_w_in[j], nsa_cmp_pos_k[j], nsa_cmp_pos_v[j],
                           nsa_cmp_wk1[j], nsa_cmp_wk2[j], nsa_cmp_wv1[j], nsa_cmp_wv2[j], nsa_w_o[j])
        x = ffn(x, i, 1)
    return x
```

```python
import functools

import jax
import jax.numpy as jnp
from jax import lax
from jax.experimental import pallas as pl
from jax.experimental.pallas import tpu as pltpu

F32 = jnp.float32
BF16 = jnp.bfloat16

D_MODEL = 1024
EPS = 1e-6
D_FF = 2816
POOL_WINDOWS = (2, 4, 8, 16)
POOL_GROUP = D_MODEL // len(POOL_WINDOWS)
MAX_POOL_WINDOW = max(POOL_WINDOWS)
N_HEADS = 16
HEAD_DIM = 64
N_KV_GROUPS = 4
HEADS_PER_GROUP = N_HEADS // N_KV_GROUPS
ROPE_DIM = HEAD_DIM // 4
ROPE_THETA = 500000.0
CMP_BLOCK = 32
CMP_STRIDE = 16
CMP_HIDDEN = 256
SEL_BLOCK = 64
N_SELECT = 8
WINDOW = 256
Q_WIDTH = N_HEADS * HEAD_DIM
KV_WIDTH = N_KV_GROUPS * HEAD_DIM
GATE_WIDTH = 3 * N_HEADS
NEG_INF = -1e30

LANES = 128
VMEM_LIMIT = 56 * 1024 * 1024

FFN_TOKENS = 512
SEQ_TILE = 512
ATT_Q = 256
ATT_K = 256
N_CMP_PAD = 128
GROUPS_PER_STEP = 4

N_SEL_BLOCKS = 32
GATE_ROWS = 16
ROPE_HALF = ROPE_DIM // 2
V_ROWS = HEAD_DIM + 16
LOG2_E = 1.4426950408889634
ATT_COLS = HEADS_PER_GROUP * ATT_Q
N_CHUNKS = ATT_COLS // LANES

_ROW_VS = Q_WIDTH
_ROW_VW = _ROW_VS + KV_WIDTH
_ROW_GATE = _ROW_VW + KV_WIDTH
_PROJ_ROWS = _ROW_GATE + N_KV_GROUPS * GATE_ROWS
_G_COLS = N_KV_GROUPS * LANES
_OFF_KW = _G_COLS
_OFF_KVC = 2 * _G_COLS
_PROJ_COLS = 3 * _G_COLS

_NT = (((1,), (1,)), ((), ()))


def _rms(x, g):
    return x * lax.rsqrt(jnp.mean(x * x, axis=-1, keepdims=True) + EPS) * g


def _const_spec(shape):
    nd = len(shape)
    return pl.BlockSpec(shape, lambda *_: (0,) * nd, pipeline_mode=pl.Buffered(1))


def _ffn_body(x_ref, gpre_ref, gpost_ref, wg_ref, wu_ref, wd_ref, o_ref):
    x = x_ref[...]
    xb = _rms(x, gpre_ref[...]).astype(BF16)
    hg = jnp.dot(xb, wg_ref[...], preferred_element_type=F32)
    hu = jnp.dot(xb, wu_ref[...], preferred_element_type=F32)
    act = (hg * jax.nn.sigmoid(hg) * hu).astype(BF16)
    f = jnp.dot(act, wd_ref[...], preferred_element_type=F32)
    o_ref[...] = x + 0.5 * _rms(f, gpost_ref[...])


def _ffn_block(x2d, g_pre, g_post, wg, wu, wd):
    t = x2d.shape[0]
    row = pl.BlockSpec((FFN_TOKENS, D_MODEL), lambda i: (i, 0))
    return pl.pallas_call(
        _ffn_body,
        grid=(t // FFN_TOKENS,),
        in_specs=[row, _const_spec((1, D_MODEL)), _const_spec((1, D_MODEL)),
                  _const_spec((D_MODEL, D_FF)), _const_spec((D_MODEL, D_FF)),
                  _const_spec((D_FF, D_MODEL))],
        out_specs=row,
        out_shape=jax.ShapeDtypeStruct((t, D_MODEL), F32),
        compiler_params=pltpu.CompilerParams(
            dimension_semantics=("parallel",), vmem_limit_bytes=VMEM_LIMIT),
        name="ffn_block",
    )(x2d, g_pre.reshape(1, -1), g_post.reshape(1, -1),
      wg.astype(BF16), wu.astype(BF16), wd.astype(BF16))


def _pool_body(x_ref, g2_ref, g3_ref, w_ref, b_ref, sc_ref, o_ref, carry_ref):
    si = pl.program_id(1)

    @pl.when(si == 0)
    def _():
        carry_ref[...] = jnp.zeros_like(carry_ref)

    x = x_ref[...]
    h = _rms(x, g2_ref[...])
    hp = jnp.concatenate([carry_ref[...], h], axis=0)
    carry_ref[...] = h[SEQ_TILE - MAX_POOL_WINDOW:, :]
    pos = si * SEQ_TILE + lax.broadcasted_iota(jnp.int32, (SEQ_TILE, 1), 0)
    ys = []
    for g, w in enumerate(POOL_WINDOWS):
        cols = slice(g * POOL_GROUP, (g + 1) * POOL_GROUP)
        acc = hp[:, cols]
        k = 1
        while k < w:
            acc = acc + pltpu.roll(acc, k, axis=0)
            k *= 2
        cnt = jnp.minimum(pos + 1, w).astype(F32)
        d = acc[MAX_POOL_WINDOW:, :] / cnt - h[:, cols]
        y = jnp.dot(d.astype(BF16), w_ref[g], preferred_element_type=F32) + b_ref[g:g + 1, :]
        ys.append(y)
    m = jnp.concatenate(ys, axis=1) * sc_ref[...]
    o_ref[...] = x + _rms(m, g3_ref[...])


def _pool_block(x, g2, g3, w, b, scale):
    bsz, seq, _ = x.shape
    row = pl.BlockSpec((None, SEQ_TILE, D_MODEL), lambda bi, si: (bi, si, 0))
    return pl.pallas_call(
        _pool_body,
        grid=(bsz, seq // SEQ_TILE),
        in_specs=[row, _const_spec((1, D_MODEL)), _const_spec((1, D_MODEL)),
                  _const_spec((len(POOL_WINDOWS), POOL_GROUP, POOL_GROUP)),
                  _const_spec((len(POOL_WINDOWS), POOL_GROUP)), _const_spec((1, D_MODEL))],
        out_specs=row,
        out_shape=jax.ShapeDtypeStruct(x.shape, F32),
        scratch_shapes=[pltpu.VMEM((MAX_POOL_WINDOW, D_MODEL), F32)],
        compiler_params=pltpu.CompilerParams(
            dimension_semantics=("parallel", "arbitrary"), vmem_limit_bytes=VMEM_LIMIT),
        name="pool_block",
    )(x, g2.reshape(1, -1), g3.reshape(1, -1), w.astype(BF16), b, scale.reshape(1, -1))


def _proj_body(x_ref, g2_ref, wt_ref, wn_ref, c_ref, sa_ref, sb_ref, ct_ref, st_ref,
               qn_ref, qr_ref, vs_ref, vw_ref, gate_ref, ks_ref, kw_ref, kvc_ref):
    si = pl.program_id(1)
    hb = _rms(x_ref[...], g2_ref[...]).astype(BF16)

    pt = lax.dot_general(wt_ref[...], hb, _NT, preferred_element_type=F32)
    qn_ref[...] = pt[:Q_WIDTH].astype(BF16)
    cos_t, sin_t = ct_ref[...], st_ref[...]
    for h in range(N_HEADS):
        x1 = pt[h * HEAD_DIM:h * HEAD_DIM + ROPE_HALF]
        x2 = pt[h * HEAD_DIM + ROPE_HALF:h * HEAD_DIM + ROPE_DIM]
        rot = jnp.concatenate([x1 * cos_t - x2 * sin_t, x1 * sin_t + x2 * cos_t], axis=0)
        qr_ref[h * ROPE_DIM:(h + 1) * ROPE_DIM, :] = rot.astype(BF16)
    ones_rows = jnp.ones((V_ROWS - HEAD_DIM, SEQ_TILE), F32)
    for g in range(N_KV_GROUPS):
        vs_ref[g] = jnp.concatenate(
            [pt[_ROW_VS + g * HEAD_DIM:_ROW_VS + (g + 1) * HEAD_DIM], ones_rows], axis=0).astype(BF16)
        vw_ref[g] = jnp.concatenate(
            [pt[_ROW_VW + g * HEAD_DIM:_ROW_VW + (g + 1) * HEAD_DIM], ones_rows], axis=0).astype(BF16)
        gate_ref[g] = jax.nn.sigmoid(pt[_ROW_GATE + g * GATE_ROWS:_ROW_GATE + (g + 1) * GATE_ROWS])

    pn = jnp.dot(hb, wn_ref[...], preferred_element_type=F32)
    cos, sin_a, sin_b = c_ref[...], sa_ref[...], sb_ref[...]

    def slab(off, i):
        return pn[:, off + i * LANES: off + (i + 1) * LANES]

    def rope(t):
        return (t * cos + pltpu.roll(t, LANES - ROPE_HALF, axis=1) * sin_a
                + pltpu.roll(t, ROPE_HALF, axis=1) * sin_b)

    pos = si * SEQ_TILE + lax.broadcasted_iota(jnp.int32, (SEQ_TILE, LANES), 0)
    lane = lax.broadcasted_iota(jnp.int32, (SEQ_TILE, LANES), 1)
    blk_onehot = jnp.where(lane - HEAD_DIM == pos // SEL_BLOCK, 1.0, 0.0)
    for g in range(N_KV_GROUPS):
        ks_ref[g] = (rope(slab(0, g)) + blk_onehot).astype(BF16)
        kw_ref[g] = rope(slab(_OFF_KW, g)).astype(BF16)
        kvc_ref[g] = slab(_OFF_KVC, g)


def _nsa_proj_weights(w_in):
    d = w_in.shape[0]
    wq = w_in[:, :Q_WIDTH] * (HEAD_DIM ** -0.5)
    kv = w_in[:, Q_WIDTH:Q_WIDTH + 6 * KV_WIDTH].reshape(d, 6, N_KV_GROUPS, HEAD_DIM)
    wkc, wvc, wks, wvs, wkw, wvw = [kv[:, i] for i in range(6)]
    wgate = w_in[:, Q_WIDTH + 6 * KV_WIDTH:].reshape(d, 3, N_KV_GROUPS, HEADS_PER_GROUP)
    wgate = wgate.transpose(0, 2, 1, 3).reshape(d, N_KV_GROUPS, 3 * HEADS_PER_GROUP)
    wgate = jnp.pad(wgate, ((0, 0), (0, 0), (0, GATE_ROWS - 3 * HEADS_PER_GROUP)))
    w_t = jnp.concatenate([wq, wvs.reshape(d, KV_WIDTH), wvw.reshape(d, KV_WIDTH),
                           wgate.reshape(d, N_KV_GROUPS * GATE_ROWS)], axis=1).T.astype(BF16)
    zero = jnp.zeros_like(wkc)

    def pair(a, b):
        return jnp.concatenate([a, b], axis=-1).reshape(d, _G_COLS)

    w_n = jnp.concatenate([pair(wks * LOG2_E, zero), pair(wkw * LOG2_E, zero), pair(wkc, wvc)],
                          axis=1).astype(BF16)
    return w_t, w_n


def _rope_tables(seq):
    inv = 1.0 / (ROPE_THETA ** (jnp.arange(0, ROPE_DIM, 2, dtype=F32) / ROPE_DIM))
    ang = jnp.arange(seq, dtype=F32)[:, None] * inv[None, :]
    cos, sin = jnp.cos(ang), jnp.sin(ang)
    pad = LANES - ROPE_DIM
    c = jnp.concatenate([cos, cos, jnp.ones((seq, pad), F32)], axis=1)
    sa = jnp.concatenate([-sin, jnp.zeros((seq, ROPE_HALF + pad), F32)], axis=1)
    sb = jnp.concatenate([jnp.zeros((seq, ROPE_HALF), F32), sin, jnp.zeros((seq, pad), F32)], axis=1)
    return c, sa, sb, cos.T, sin.T


def _nsa_proj(x, g2, w_t, w_n, tables):
    bsz, seq, _ = x.shape
    feat = lambda rows, dt: jax.ShapeDtypeStruct((bsz, N_KV_GROUPS, rows, seq), dt)
    feat_spec = lambda rows: pl.BlockSpec((None, N_KV_GROUPS, rows, SEQ_TILE), lambda bi, si: (bi, 0, 0, si))
    tok = lambda dt: jax.ShapeDtypeStruct((bsz, N_KV_GROUPS, seq, LANES), dt)
    tok_spec = pl.BlockSpec((None, N_KV_GROUPS, SEQ_TILE, LANES), lambda bi, si: (bi, 0, si, 0))
    tab_spec = pl.BlockSpec((SEQ_TILE, LANES), lambda bi, si: (si, 0))
    tab_t_spec = pl.BlockSpec((ROPE_HALF, SEQ_TILE), lambda bi, si: (0, si))
    n_rot = N_HEADS * ROPE_DIM
    return pl.pallas_call(
        _proj_body,
        grid=(bsz, seq // SEQ_TILE),
        in_specs=[pl.BlockSpec((None, SEQ_TILE, D_MODEL), lambda bi, si: (bi, si, 0)),
                  _const_spec((1, D_MODEL)), _const_spec((_PROJ_ROWS, D_MODEL)),
                  _const_spec((D_MODEL, _PROJ_COLS)),
                  tab_spec, tab_spec, tab_spec, tab_t_spec, tab_t_spec],
        out_specs=[pl.BlockSpec((None, Q_WIDTH, SEQ_TILE), lambda bi, si: (bi, 0, si)),
                   pl.BlockSpec((None, n_rot, SEQ_TILE), lambda bi, si: (bi, 0, si)),
                   feat_spec(V_ROWS), feat_spec(V_ROWS), feat_spec(GATE_ROWS),
                   tok_spec, tok_spec, tok_spec],
        out_shape=[jax.ShapeDtypeStruct((bsz, Q_WIDTH, seq), BF16),
                   jax.ShapeDtypeStruct((bsz, n_rot, seq), BF16),
                   feat(V_ROWS, BF16), feat(V_ROWS, BF16), feat(GATE_ROWS, F32),
                   tok(BF16), tok(BF16), tok(F32)],
        compiler_params=pltpu.CompilerParams(
            dimension_semantics=("parallel", "parallel"), vmem_limit_bytes=VMEM_LIMIT),
        name="nsa_proj",
    )(x, g2.reshape(1, -1), w_t, w_n, *tables)


def _cmp_body(kvc_ref, pos_ref, w1_ref, w2k_ref, w2vt_ref, kc_ref, vct_ref):
    first = jnp.zeros((N_CMP_PAD, 2 * CMP_HIDDEN), F32)
    second = jnp.zeros((N_CMP_PAD, 2 * CMP_HIDDEN), F32)
    for t in range(CMP_STRIDE):
        xt = kvc_ref[pl.ds(t, N_CMP_PAD, stride=CMP_STRIDE), :]
        za = (xt + pos_ref[t:t + 1, :]).astype(BF16)
        zb = (xt + pos_ref[CMP_STRIDE + t:CMP_STRIDE + t + 1, :]).astype(BF16)
        first = first + jnp.dot(za, w1_ref[t], preferred_element_type=F32)
        second = second + jnp.dot(zb, w1_ref[CMP_STRIDE + t], preferred_element_type=F32)
    row = lax.broadcasted_iota(jnp.int32, second.shape, 0)
    second = jnp.where(row < N_CMP_PAD - 1, pltpu.roll(second, N_CMP_PAD - 1, axis=0), 0.0)
    pre = first + second
    hid = (pre * jax.nn.sigmoid(pre)).astype(BF16)
    kc_ref[...] = jnp.dot(hid[:, :CMP_HIDDEN], w2k_ref[...], preferred_element_type=F32).astype(BF16)
    vct_ref[...] = lax.dot_general(w2vt_ref[...], hid[:, CMP_HIDDEN:], _NT,
                                   preferred_element_type=F32).astype(BF16)


def _compress(kvc, pos_k, pos_v, wk1, wk2, wv1, wv2):
    bsz, _, seq, _ = kvc.shape
    assert seq == N_CMP_PAD * CMP_STRIDE
    pos = jnp.concatenate([pos_k, pos_v], axis=1)
    k1 = wk1.reshape(CMP_BLOCK, HEAD_DIM, CMP_HIDDEN)
    v1 = wv1.reshape(CMP_BLOCK, HEAD_DIM, CMP_HIDDEN)
    z1 = jnp.zeros_like(k1)
    w1 = jnp.concatenate([jnp.concatenate([k1, z1], axis=2),
                          jnp.concatenate([z1, v1], axis=2)], axis=1).astype(BF16)
    w2k = (wk2 * LOG2_E).astype(BF16)
    w2vt = wv2.T.astype(BF16)
    return pl.pallas_call(
        _cmp_body,
        grid=(bsz, N_KV_GROUPS),
        in_specs=[pl.BlockSpec((None, None, seq, LANES), lambda bi, gi: (bi, gi, 0, 0)),
                  _const_spec(pos.shape), _const_spec(w1.shape),
                  _const_spec(w2k.shape), _const_spec(w2vt.shape)],
        out_specs=[pl.BlockSpec((None, None, N_CMP_PAD, HEAD_DIM), lambda bi, gi: (bi, gi, 0, 0)),
                   pl.BlockSpec((None, None, HEAD_DIM, N_CMP_PAD), lambda bi, gi: (bi, gi, 0, 0))],
        out_shape=[jax.ShapeDtypeStruct((bsz, N_KV_GROUPS, N_CMP_PAD, HEAD_DIM), BF16),
                   jax.ShapeDtypeStruct((bsz, N_KV_GROUPS, HEAD_DIM, N_CMP_PAD), BF16)],
        compiler_params=pltpu.CompilerParams(dimension_semantics=("parallel", "parallel")),
        name="nsa_compress",
    )(kvc, pos, w1, w2k, w2vt)


_HEAD_COLS = [slice(r * ATT_Q, (r + 1) * ATT_Q) for r in range(HEADS_PER_GROUP)]


def _score_stage(k_ref, k0, lhs_ref, s_out, t_out=None):
    k_tile = k_ref[pl.ds(k0, ATT_K), :]
    t_max = []
    for cs in _HEAD_COLS:
        s = jnp.dot(k_tile, lhs_ref[:, cs], preferred_element_type=F32)
        s_out[:, cs] = s
        if t_out is not None:
            t_max.append(jnp.max(s, axis=0, keepdims=True))
    if t_out is not None:
        t_out[...] = jnp.concatenate(t_max, axis=1)


def _softmax_stage(s_in, t_in, m_ref, p_out, a_out, mask=None):
    m_all = m_ref[...]
    t_all = t_in[...]
    m_new, alpha = [], []
    for cs in _HEAD_COLS:
        s = s_in[:, cs]
        if mask is not None:
            s = jnp.where(mask, s, NEG_INF)
            t_c = jnp.max(s, axis=0, keepdims=True)
        else:
            t_c = t_all[:, cs]
        m_c = jnp.maximum(m_all[:, cs], t_c)
        p_out[:, cs] = jnp.exp2((s - m_c).astype(BF16))
        alpha.append(jnp.exp2(m_all[:, cs] - m_c))
        m_new.append(m_c)
    m_ref[...] = jnp.concatenate(m_new, axis=1)
    a_out[...] = jnp.concatenate(alpha, axis=1)


def _value_stage(vt_ref, k0, p_in, a_in, acc_ref):
    vt = vt_ref[:, pl.ds(k0, ATT_K)]
    acc_all, a_all = acc_ref[...], a_in[...]
    acc_ref[...] = jnp.concatenate(
        [a_all[:, cs] * acc_all[:, cs] + jnp.dot(vt, p_in[:, cs], preferred_element_type=F32)
         for cs in _HEAD_COLS], axis=1)


def _tree_sum(terms):
    while len(terms) > 1:
        terms = [a + b for a, b in zip(terms[::2], terms[1::2])] + terms[len(terms) & ~1:]
    return terms[0]


def _select_body(qn_ref, kc_ref, vct_ref, ov_ref, bias_ref, ocmp_ref):
    q0 = pl.program_id(1) * ATT_Q
    heads = [(g, r) for g in range(N_KV_GROUPS) for r in range(HEADS_PER_GROUP)]
    rows = lambda g, r: slice((g * HEADS_PER_GROUP + r) * HEAD_DIM, (g * HEADS_PER_GROUP + r + 1) * HEAD_DIM)

    blk_end = lax.broadcasted_iota(jnp.int32, (N_CMP_PAD, ATT_Q), 0) * CMP_STRIDE + CMP_BLOCK - 1
    qpos = q0 + lax.broadcasted_iota(jnp.int32, (N_CMP_PAD, ATT_Q), 1)
    cmp_mask = blk_end <= qpos
    any_cmp = q0 + lax.broadcasted_iota(jnp.int32, (1, ATT_Q), 1) >= CMP_BLOCK - 1
    scores = [jnp.dot(kc_ref[g], qn_ref[rows(g, r), :], preferred_element_type=F32) for g, r in heads]
    probs = []
    for s in scores:
        s = jnp.where(cmp_mask, s, NEG_INF)
        e = jnp.exp2(s - jnp.max(s, axis=0, keepdims=True))
        probs.append(e * jnp.where(any_cmp, 1.0 / jnp.sum(e, axis=0, keepdims=True), 0.0))
    for (g, r), p in zip(heads, probs):
        ocmp_ref[rows(g, r), :] = jnp.dot(vct_ref[g], p.astype(BF16), preferred_element_type=F32)

    p_sum = jnp.concatenate(
        [_tree_sum(probs[g * HEADS_PER_GROUP:(g + 1) * HEADS_PER_GROUP]) for g in range(N_KV_GROUPS)], axis=1)
    hi = p_sum.astype(BF16)
    rem = p_sum - hi.astype(F32)
    mid = rem.astype(BF16)
    lo = (rem - mid.astype(F32)).astype(BF16)
    ov = ov_ref[...]
    imp = (jnp.dot(ov, hi, preferred_element_type=F32) + jnp.dot(ov, mid, preferred_element_type=F32)
           + jnp.dot(ov, lo, preferred_element_type=F32))
    cols = N_KV_GROUPS * ATT_Q
    j = lax.broadcasted_iota(jnp.int32, (N_SEL_BLOCKS, cols), 0)
    q_in = lax.broadcasted_iota(jnp.int32, (N_SEL_BLOCKS, cols), 1) & (ATT_Q - 1)
    q_blk = (q0 + q_in) // SEL_BLOCK
    forced = (j == 0) | (j == q_blk) | (j == q_blk - 1)
    imp = jnp.where(forced, jnp.inf, jnp.where(j > q_blk, -jnp.inf, imp))
    sub = 8
    row_in = lax.broadcasted_iota(jnp.int32, (sub, cols), 0)
    bias = []
    for b0 in range(0, N_SEL_BLOCKS, sub):
        mine = imp[b0:b0 + sub, :]
        beats = []
        for jp in range(N_SEL_BLOCKS):
            other = imp[jp:jp + 1, :]
            ge = jnp.where(other >= mine, 1.0, 0.0)
            gt = jnp.where(other > mine, 1.0, 0.0)
            if jp < b0:
                beats.append(ge)
            elif jp >= b0 + sub:
                beats.append(gt)
            else:
                beats.append(jnp.where(row_in > jp - b0, ge, gt))
        bias.append(jnp.where(_tree_sum(beats) < N_SELECT, 0.0, NEG_INF))
    bias = jnp.concatenate(bias, axis=0).astype(BF16)
    for g in range(N_KV_GROUPS):
        bias_ref[g] = bias[:, g * ATT_Q:(g + 1) * ATT_Q]


def _select(qn, kc, vct):
    bsz, _, seq = qn.shape
    assert seq // SEL_BLOCK == N_SEL_BLOCKS
    grp = lambda *blk: pl.BlockSpec((None, N_KV_GROUPS) + blk, lambda bi, qi: (bi, 0, 0, 0))
    q_spec = pl.BlockSpec((None, Q_WIDTH, ATT_Q), lambda bi, qi: (bi, 0, qi))
    return pl.pallas_call(
        _select_body,
        grid=(bsz, seq // ATT_Q),
        in_specs=[q_spec, grp(N_CMP_PAD, HEAD_DIM), grp(HEAD_DIM, N_CMP_PAD),
                  _const_spec((N_SEL_BLOCKS, N_CMP_PAD))],
        out_specs=[pl.BlockSpec((None, N_KV_GROUPS, N_SEL_BLOCKS, ATT_Q), lambda bi, qi: (bi, 0, 0, qi)),
                   q_spec],
        out_shape=[jax.ShapeDtypeStruct((bsz, N_KV_GROUPS, N_SEL_BLOCKS, seq), BF16),
                   jax.ShapeDtypeStruct((bsz, Q_WIDTH, seq), F32)],
        compiler_params=pltpu.CompilerParams(dimension_semantics=("parallel", "parallel")),
        name="nsa_select",
    )(qn, kc, vct, _overlap(seq))


def _window_branch(qi, q0, prev0, kw_ref, vwt_ref, sw_ref, pw_ref, accw_ref):
    key_in = lax.broadcasted_iota(jnp.int32, (ATT_K, ATT_Q), 0)
    col_in = lax.broadcasted_iota(jnp.int32, (ATT_K, ATT_Q), 1)
    causal = key_in <= col_in
    prev = (key_in > col_in) & (qi > 0)
    for cs in _HEAD_COLS:
        s_diag = jnp.where(causal, sw_ref[0, :, cs], NEG_INF)
        s_prev = jnp.where(prev, sw_ref[1, :, cs], NEG_INF)
        m_w = jnp.maximum(jnp.max(s_diag, axis=0, keepdims=True), jnp.max(s_prev, axis=0, keepdims=True))
        pw_ref[0, :, cs] = jnp.exp2((s_diag - m_w).astype(BF16))
        pw_ref[1, :, cs] = jnp.exp2((s_prev - m_w).astype(BF16))
    vw_diag = vwt_ref[:, pl.ds(q0, ATT_K)]
    vw_prev = vwt_ref[:, pl.ds(prev0, ATT_K)]
    for cs in _HEAD_COLS:
        accw_ref[:, cs] = (jnp.dot(vw_diag, pw_ref[0, :, cs], preferred_element_type=F32)
                           + jnp.dot(vw_prev, pw_ref[1, :, cs], preferred_element_type=F32))


def _attn_body(qn_ref, qr_ref, bias_ref, ocmp_ref, ks_ref, kw_ref, vst_ref, vwt_ref, gate_ref,
               o_ref, lhs_ref, s_ref, sw_ref, p_ref, pw_ref, t_ref, a_ref, m_ref, acc_ref, accw_ref):
    qi = pl.program_id(2)
    q0 = pl.multiple_of(qi * ATT_Q, ATT_Q)
    prev0 = pl.multiple_of(jnp.maximum(qi - 1, 0) * ATT_K, ATT_K)
    groups = range(GROUPS_PER_STEP)
    gh = HEADS_PER_GROUP * HEAD_DIM

    pad = jnp.zeros((LANES - HEAD_DIM - N_SEL_BLOCKS, ATT_Q), BF16)
    for g in groups:
        for r in range(HEADS_PER_GROUP):
            h = g * HEADS_PER_GROUP + r
            lhs_ref[g, :, r * ATT_Q:(r + 1) * ATT_Q] = jnp.concatenate(
                [qr_ref[h * ROPE_DIM:(h + 1) * ROPE_DIM, :],
                 qn_ref[h * HEAD_DIM + ROPE_DIM:(h + 1) * HEAD_DIM, :], bias_ref[g], pad], axis=0)

    for g in groups:
        _score_stage(kw_ref.at[g], q0, lhs_ref.at[g], sw_ref.at[g, 0])
        _score_stage(kw_ref.at[g], prev0, lhs_ref.at[g], sw_ref.at[g, 1])
    for g in groups:
        _window_branch(qi, q0, prev0, kw_ref.at[g], vwt_ref.at[g], sw_ref.at[g], pw_ref.at[g], accw_ref.at[g])

    for g in groups:
        m_ref[g] = jnp.full(m_ref.shape[1:], NEG_INF, F32)
        acc_ref[g] = jnp.zeros(acc_ref.shape[1:], F32)
        p_ref[g, 1] = jnp.zeros(p_ref.shape[2:], BF16)
        a_ref[g, 1] = jnp.ones(a_ref.shape[2:], F32)
    for g in groups:
        _score_stage(ks_ref.at[g], 0, lhs_ref.at[g], s_ref.at[g, 0], t_ref.at[g, 0])

    def trip(j, cur):
        nxt = 1 - cur
        for g in groups:
            _softmax_stage(s_ref.at[g, cur], t_ref.at[g, cur], m_ref.at[g], p_ref.at[g, cur], a_ref.at[g, cur])
        for g in groups:
            _score_stage(ks_ref.at[g], pl.multiple_of((j + 1) * ATT_K, ATT_K), lhs_ref.at[g],
                         s_ref.at[g, nxt], t_ref.at[g, nxt])
        for g in groups:
            _value_stage(vst_ref.at[g], pl.multiple_of(jnp.maximum(j - 1, 0) * ATT_K, ATT_K),
                         p_ref.at[g, nxt], a_ref.at[g, nxt], acc_ref.at[g])

    def trip_pair(jj, carry):
        trip(2 * jj, 0)
        trip(2 * jj + 1, 1)
        return carry

    lax.fori_loop(0, qi // 2, trip_pair, 0)

    @pl.when(qi % 2 == 1)
    def _():
        trip(qi - 1, 0)

    key_in = lax.broadcasted_iota(jnp.int32, (ATT_K, ATT_Q), 0)
    col_in = lax.broadcasted_iota(jnp.int32, (ATT_K, ATT_Q), 1)
    causal = key_in <= col_in

    def drain(last):
        for g in groups:
            _softmax_stage(s_ref.at[g, last], t_ref.at[g, last], m_ref.at[g], p_ref.at[g, last],
                           a_ref.at[g, last], mask=causal)
        for g in groups:
            _value_stage(vst_ref.at[g], prev0, p_ref.at[g, 1 - last], a_ref.at[g, 1 - last], acc_ref.at[g])
        for g in groups:
            _value_stage(vst_ref.at[g], q0, p_ref.at[g, last], a_ref.at[g, last], acc_ref.at[g])

    pl.when(qi % 2 == 0)(lambda: drain(0))
    pl.when(qi % 2 == 1)(lambda: drain(1))

    for g in groups:
        gate = gate_ref[g]
        heads = []
        for r, cs in enumerate(_HEAD_COLS):
            g_cmp = gate[r:r + 1, :]
            g_sel = gate[HEADS_PER_GROUP + r:HEADS_PER_GROUP + r + 1, :]
            g_win = gate[2 * HEADS_PER_GROUP + r:2 * HEADS_PER_GROUP + r + 1, :]
            o_sel = acc_ref[g, :HEAD_DIM, cs] / acc_ref[g, HEAD_DIM:HEAD_DIM + 1, cs]
            o_win = accw_ref[g, :HEAD_DIM, cs] / accw_ref[g, HEAD_DIM:HEAD_DIM + 1, cs]
            h = g * HEADS_PER_GROUP + r
            o_cmp = ocmp_ref[h * HEAD_DIM:(h + 1) * HEAD_DIM, :]
            heads.append(g_cmp * o_cmp + g_sel * o_sel + g_win * o_win)
        o_ref[:, g * gh:(g + 1) * gh] = jnp.concatenate(heads, axis=0).T.astype(BF16)


def _overlap(seq):
    ci = jnp.arange(N_CMP_PAD) * CMP_STRIDE
    sj = jnp.arange(seq // SEL_BLOCK) * SEL_BLOCK
    n_cmp = (seq - CMP_BLOCK) // CMP_STRIDE + 1
    ov = ((ci[None, :] < sj[:, None] + SEL_BLOCK) & (ci[None, :] + CMP_BLOCK > sj[:, None])
          & (jnp.arange(N_CMP_PAD)[None, :] < n_cmp))
    return ov.astype(BF16)


def _attention(qn, qr, bias, ocmp, ks, kw, vst, vwt, gate):
    bsz, _, seq = qn.shape
    assert ATT_Q == WINDOW == ATT_K
    gps = GROUPS_PER_STEP
    gh = gps * HEADS_PER_GROUP * HEAD_DIM
    grp = lambda *blk: pl.BlockSpec((None, gps) + blk, lambda bi, gi, qi: (bi, gi, 0, 0))
    q_spec = pl.BlockSpec((None, gh, ATT_Q), lambda bi, gi, qi: (bi, gi, qi))
    return pl.pallas_call(
        _attn_body,
        grid=(bsz, N_KV_GROUPS // gps, seq // ATT_Q),
        in_specs=[q_spec,
                  pl.BlockSpec((None, gps * HEADS_PER_GROUP * ROPE_DIM, ATT_Q), lambda bi, gi, qi: (bi, gi, qi)),
                  pl.BlockSpec((None, gps, N_SEL_BLOCKS, ATT_Q), lambda bi, gi, qi: (bi, gi, 0, qi)),
                  q_spec,
                  grp(seq, LANES), grp(seq, LANES), grp(V_ROWS, seq), grp(V_ROWS, seq),
                  pl.BlockSpec((None, gps, GATE_ROWS, ATT_Q), lambda bi, gi, qi: (bi, gi, 0, qi))],
        out_specs=pl.BlockSpec((None, ATT_Q, gh), lambda bi, gi, qi: (bi, qi, gi)),
        out_shape=jax.ShapeDtypeStruct((bsz, seq, Q_WIDTH), BF16),
        scratch_shapes=[pltpu.VMEM((gps, LANES, ATT_COLS), BF16),
                        pltpu.VMEM((gps, 2, ATT_K, ATT_COLS), F32),
                        pltpu.VMEM((gps, 2, ATT_K, ATT_COLS), F32),
                        pltpu.VMEM((gps, 2, ATT_K, ATT_COLS), BF16),
                        pltpu.VMEM((gps, 2, ATT_K, ATT_COLS), BF16),
                        pltpu.VMEM((gps, 2, 1, ATT_COLS), F32),
                        pltpu.VMEM((gps, 2, 1, ATT_COLS), F32),
                        pltpu.VMEM((gps, 1, ATT_COLS), F32),
                        pltpu.VMEM((gps, V_ROWS, ATT_COLS), F32),
                        pltpu.VMEM((gps, V_ROWS, ATT_COLS), F32)],
        compiler_params=pltpu.CompilerParams(
            dimension_semantics=("parallel", "parallel", "arbitrary"), vmem_limit_bytes=VMEM_LIMIT),
        name="nsa_attention",
    )(qn, qr, bias, ocmp, ks, kw, vst, vwt, gate)


def _oproj_body(o_ref_in, x_ref, w_ref, g_ref, out_ref):
    m = jnp.dot(o_ref_in[...], w_ref[...], preferred_element_type=F32)
    out_ref[...] = x_ref[...] + _rms(m, g_ref[...])


def _out_proj(o2d, x2d, w_o, g3):
    t = x2d.shape[0]
    row = pl.BlockSpec((FFN_TOKENS, D_MODEL), lambda i: (i, 0))
    return pl.pallas_call(
        _oproj_body,
        grid=(t // FFN_TOKENS,),
        in_specs=[row, row, _const_spec((Q_WIDTH, D_MODEL)), _const_spec((1, D_MODEL))],
        out_specs=row,
        out_shape=jax.ShapeDtypeStruct((t, D_MODEL), F32),
        compiler_params=pltpu.CompilerParams(dimension_semantics=("parallel",)),
        name="nsa_out_proj",
    )(o2d, x2d, w_o.astype(BF16), g3.reshape(1, -1))


def _nsa_block(x, g2, g3, w_in, pos_k, pos_v, wk1, wk2, wv1, wv2, w_o):
    bsz, seq, d = x.shape
    w_t, w_n = _nsa_proj_weights(w_in)
    qn, qr, vst, vwt, gate, ks, kw, kvc = _nsa_proj(x, g2, w_t, w_n, _rope_tables(seq))
    kc, vct = _compress(kvc, pos_k, pos_v, wk1, wk2, wv1, wv2)
    bias, ocmp = _select(qn, kc, vct)
    o = _attention(qn, qr, bias, ocmp, ks, kw, vst, vwt, gate)
    return _out_proj(o.reshape(bsz * seq, Q_WIDTH), x.reshape(bsz * seq, d), w_o, g3).reshape(x.shape)


def kernel(x, norm_gains, ffn_w_gate, ffn_w_up, ffn_w_down, pool_w, pool_b, pool_scale, nsa_w_in, nsa_cmp_pos_k, nsa_cmp_pos_v, nsa_cmp_wk1, nsa_cmp_wk2, nsa_cmp_wv1, nsa_cmp_wv2, nsa_w_o):
    bsz, seq, d = x.shape
    depth = norm_gains.shape[0]

    def ffn(x, i, half):
        g = norm_gains[i]
        y = _ffn_block(x.reshape(bsz * seq, d), g[4 * half], g[4 * half + 1],
                       ffn_w_gate[i, half], ffn_w_up[i, half], ffn_w_down[i, half])
        return y.reshape(bsz, seq, d)

    for i in range(depth):
        g = norm_gains[i]
        x = ffn(x, i, 0)
        j = i // 2
        if i % 2 == 0:
            x = _pool_block(x, g[2], g[3], pool_w[j], pool_b[j], pool_scale[j])
        else:
            x = _nsa_block(x, g[2], g[3], nsa_w_in[j], nsa_cmp_pos_k[j], nsa_cmp_pos_v[j],
                           nsa_cmp_wk1[j], nsa_cmp_wk2[j], nsa_cmp_wv1[j], nsa_cmp_wv2[j], nsa_w_o[j])
        x = ffn(x, i, 1)
    return x
```

```python
import functools

import jax
import jax.numpy as jnp
from jax import lax
from jax.experimental import pallas as pl
from jax.experimental.pallas import tpu as pltpu

F32 = jnp.float32
BF16 = jnp.bfloat16

D_MODEL = 1024
EPS = 1e-6
D_FF = 2816
POOL_WINDOWS = (2, 4, 8, 16)
POOL_GROUP = D_MODEL // len(POOL_WINDOWS)
MAX_POOL_WINDOW = max(POOL_WINDOWS)
N_HEADS = 16
HEAD_DIM = 64
N_KV_GROUPS = 4
HEADS_PER_GROUP = N_HEADS // N_KV_GROUPS
ROPE_DIM = HEAD_DIM // 4
ROPE_THETA = 500000.0
CMP_BLOCK = 32
CMP_STRIDE = 16
CMP_HIDDEN = 256
SEL_BLOCK = 64
N_SELECT = 8
WINDOW = 256
Q_WIDTH = N_HEADS * HEAD_DIM
KV_WIDTH = N_KV_GROUPS * HEAD_DIM
GATE_WIDTH = 3 * N_HEADS
NEG_INF = -1e30

LANES = 128
VMEM_LIMIT = 56 * 1024 * 1024

FFN_TOKENS = 512
SEQ_TILE = 512
ATT_Q = 256
ATT_K = 256
N_CMP_PAD = 128
GROUPS_PER_STEP = 4

N_SEL_BLOCKS = 32
GATE_ROWS = 16
ROPE_HALF = ROPE_DIM // 2
V_ROWS = HEAD_DIM + 16
LOG2_E = 1.4426950408889634
ATT_COLS = HEADS_PER_GROUP * ATT_Q
N_CHUNKS = ATT_COLS // LANES

_ROW_VS = Q_WIDTH
_ROW_VW = _ROW_VS + KV_WIDTH
_ROW_GATE = _ROW_VW + KV_WIDTH
_PROJ_ROWS = _ROW_GATE + N_KV_GROUPS * GATE_ROWS
_G_COLS = N_KV_GROUPS * LANES
_OFF_KW = _G_COLS
_OFF_KVC = 2 * _G_COLS
_PROJ_COLS = 3 * _G_COLS

_NT = (((1,), (1,)), ((), ()))


def _rms(x, g):
    return x * lax.rsqrt(jnp.mean(x * x, axis=-1, keepdims=True) + EPS) * g


def _const_spec(shape):
    nd = len(shape)
    return pl.BlockSpec(shape, lambda *_: (0,) * nd, pipeline_mode=pl.Buffered(1))


def _ffn_body(x_ref, gpre_ref, gpost_ref, wg_ref, wu_ref, wd_ref, o_ref):
    x = x_ref[...]
    xb = _rms(x, gpre_ref[...]).astype(BF16)
    hg = jnp.dot(xb, wg_ref[...], preferred_element_type=F32)
    hu = jnp.dot(xb, wu_ref[...], preferred_element_type=F32)
    act = (hg * jax.nn.sigmoid(hg) * hu).astype(BF16)
    f = jnp.dot(act, wd_ref[...], preferred_element_type=F32)
    o_ref[...] = x + 0.5 * _rms(f, gpost_ref[...])


def _ffn_block(x2d, g_pre, g_post, wg, wu, wd):
    t = x2d.shape[0]
    row = pl.BlockSpec((FFN_TOKENS, D_MODEL), lambda i: (i, 0))
    return pl.pallas_call(
        _ffn_body,
        grid=(t // FFN_TOKENS,),
        in_specs=[row, _const_spec((1, D_MODEL)), _const_spec((1, D_MODEL)),
                  _const_spec((D_MODEL, D_FF)), _const_spec((D_MODEL, D_FF)),
                  _const_spec((D_FF, D_MODEL))],
        out_specs=row,
        out_shape=jax.ShapeDtypeStruct((t, D_MODEL), F32),
        compiler_params=pltpu.CompilerParams(
            dimension_semantics=("parallel",), vmem_limit_bytes=VMEM_LIMIT),
        name="ffn_block",
    )(x2d, g_pre.reshape(1, -1), g_post.reshape(1, -1),
      wg.astype(BF16), wu.astype(BF16), wd.astype(BF16))


def _pool_body(x_ref, g2_ref, g3_ref, w_ref, b_ref, sc_ref, o_ref, carry_ref):
    si = pl.program_id(1)

    @pl.when(si == 0)
    def _():
        carry_ref[...] = jnp.zeros_like(carry_ref)

    x = x_ref[...]
    h = _rms(x, g2_ref[...])
    hp = jnp.concatenate([carry_ref[...], h], axis=0)
    carry_ref[...] = h[SEQ_TILE - MAX_POOL_WINDOW:, :]
    pos = si * SEQ_TILE + lax.broadcasted_iota(jnp.int32, (SEQ_TILE, 1), 0)
    ys = []
    for g, w in enumerate(POOL_WINDOWS):
        cols = slice(g * POOL_GROUP, (g + 1) * POOL_GROUP)
        acc = hp[:, cols]
        k = 1
        while k < w:
            acc = acc + pltpu.roll(acc, k, axis=0)
            k *= 2
        cnt = jnp.minimum(pos + 1, w).astype(F32)
        d = acc[MAX_POOL_WINDOW:, :] / cnt - h[:, cols]
        y = jnp.dot(d.astype(BF16), w_ref[g], preferred_element_type=F32) + b_ref[g:g + 1, :]
        ys.append(y)
    m = jnp.concatenate(ys, axis=1) * sc_ref[...]
    o_ref[...] = x + _rms(m, g3_ref[...])


def _pool_block(x, g2, g3, w, b, scale):
    bsz, seq, _ = x.shape
    row = pl.BlockSpec((None, SEQ_TILE, D_MODEL), lambda bi, si: (bi, si, 0))
    return pl.pallas_call(
        _pool_body,
        grid=(bsz, seq // SEQ_TILE),
        in_specs=[row, _const_spec((1, D_MODEL)), _const_spec((1, D_MODEL)),
                  _const_spec((len(POOL_WINDOWS), POOL_GROUP, POOL_GROUP)),
                  _const_spec((len(POOL_WINDOWS), POOL_GROUP)), _const_spec((1, D_MODEL))],
        out_specs=row,
        out_shape=jax.ShapeDtypeStruct(x.shape, F32),
        scratch_shapes=[pltpu.VMEM((MAX_POOL_WINDOW, D_MODEL), F32)],
        compiler_params=pltpu.CompilerParams(
            dimension_semantics=("parallel", "arbitrary"), vmem_limit_bytes=VMEM_LIMIT),
        name="pool_block",
    )(x, g2.reshape(1, -1), g3.reshape(1, -1), w.astype(BF16), b, scale.reshape(1, -1))


def _proj_body(x_ref, g2_ref, wt_ref, wn_ref, c_ref, sa_ref, sb_ref, ct_ref, st_ref,
               qn_ref, qr_ref, vs_ref, vw_ref, gate_ref, ks_ref, kw_ref, kvc_ref):
    si = pl.program_id(1)
    hb = _rms(x_ref[...], g2_ref[...]).astype(BF16)

    pt = lax.dot_general(wt_ref[...], hb, _NT, preferred_element_type=F32)
    qn_ref[...] = pt[:Q_WIDTH].astype(BF16)
    cos_t, sin_t = ct_ref[...], st_ref[...]
    for h in range(N_HEADS):
        x1 = pt[h * HEAD_DIM:h * HEAD_DIM + ROPE_HALF]
        x2 = pt[h * HEAD_DIM + ROPE_HALF:h * HEAD_DIM + ROPE_DIM]
        rot = jnp.concatenate([x1 * cos_t - x2 * sin_t, x1 * sin_t + x2 * cos_t], axis=0)
        qr_ref[h * ROPE_DIM:(h + 1) * ROPE_DIM, :] = rot.astype(BF16)
    ones_rows = jnp.ones((V_ROWS - HEAD_DIM, SEQ_TILE), F32)
    for g in range(N_KV_GROUPS):
        vs_ref[g] = jnp.concatenate(
            [pt[_ROW_VS + g * HEAD_DIM:_ROW_VS + (g + 1) * HEAD_DIM], ones_rows], axis=0).astype(BF16)
        vw_ref[g] = jnp.concatenate(
            [pt[_ROW_VW + g * HEAD_DIM:_ROW_VW + (g + 1) * HEAD_DIM], ones_rows], axis=0).astype(BF16)
        gate_ref[g] = jax.nn.sigmoid(pt[_ROW_GATE + g * GATE_ROWS:_ROW_GATE + (g + 1) * GATE_ROWS])

    pn = jnp.dot(hb, wn_ref[...], preferred_element_type=F32)
    cos, sin_a, sin_b = c_ref[...], sa_ref[...], sb_ref[...]

    def slab(off, i):
        return pn[:, off + i * LANES: off + (i + 1) * LANES]

    def rope(t):
        return (t * cos + pltpu.roll(t, LANES - ROPE_HALF, axis=1) * sin_a
                + pltpu.roll(t, ROPE_HALF, axis=1) * sin_b)

    pos = si * SEQ_TILE + lax.broadcasted_iota(jnp.int32, (SEQ_TILE, LANES), 0)
    lane = lax.broadcasted_iota(jnp.int32, (SEQ_TILE, LANES), 1)
    blk_onehot = jnp.where(lane - HEAD_DIM == pos // SEL_BLOCK, 1.0, 0.0)
    for g in range(N_KV_GROUPS):
        ks_ref[g] = (rope(slab(0, g)) + blk_onehot).astype(BF16)
        kw_ref[g] = rope(slab(_OFF_KW, g)).astype(BF16)
        kvc_ref[g] = slab(_OFF_KVC, g)


def _nsa_proj_weights(w_in):
    d = w_in.shape[0]
    wq = w_in[:, :Q_WIDTH] * (HEAD_DIM ** -0.5)
    kv = w_in[:, Q_WIDTH:Q_WIDTH + 6 * KV_WIDTH].reshape(d, 6, N_KV_GROUPS, HEAD_DIM)
    wkc, wvc, wks, wvs, wkw, wvw = [kv[:, i] for i in range(6)]
    wgate = w_in[:, Q_WIDTH + 6 * KV_WIDTH:].reshape(d, 3, N_KV_GROUPS, HEADS_PER_GROUP)
    wgate = wgate.transpose(0, 2, 1, 3).reshape(d, N_KV_GROUPS, 3 * HEADS_PER_GROUP)
    wgate = jnp.pad(wgate, ((0, 0), (0, 0), (0, GATE_ROWS - 3 * HEADS_PER_GROUP)))
    w_t = jnp.concatenate([wq, wvs.reshape(d, KV_WIDTH), wvw.reshape(d, KV_WIDTH),
                           wgate.reshape(d, N_KV_GROUPS * GATE_ROWS)], axis=1).T.astype(BF16)
    zero = jnp.zeros_like(wkc)

    def pair(a, b):
        return jnp.concatenate([a, b], axis=-1).reshape(d, _G_COLS)

    w_n = jnp.concatenate([pair(wks * LOG2_E, zero), pair(wkw * LOG2_E, zero), pair(wkc, wvc)],
                          axis=1).astype(BF16)
    return w_t, w_n


def _rope_tables(seq):
    inv = 1.0 / (ROPE_THETA ** (jnp.arange(0, ROPE_DIM, 2, dtype=F32) / ROPE_DIM))
    ang = jnp.arange(seq, dtype=F32)[:, None] * inv[None, :]
    cos, sin = jnp.cos(ang), jnp.sin(ang)
    pad = LANES - ROPE_DIM
    c = jnp.concatenate([cos, cos, jnp.ones((seq, pad), F32)], axis=1)
    sa = jnp.concatenate([-sin, jnp.zeros((seq, ROPE_HALF + pad), F32)], axis=1)
    sb = jnp.concatenate([jnp.zeros((seq, ROPE_HALF), F32), sin, jnp.zeros((seq, pad), F32)], axis=1)
    return c, sa, sb, cos.T, sin.T


def _nsa_proj(x, g2, w_t, w_n, tables):
    bsz, seq, _ = x.shape
    feat = lambda rows, dt: jax.ShapeDtypeStruct((bsz, N_KV_GROUPS, rows, seq), dt)
    feat_spec = lambda rows: pl.BlockSpec((None, N_KV_GROUPS, rows, SEQ_TILE), lambda bi, si: (bi, 0, 0, si))
    tok = lambda dt: jax.ShapeDtypeStruct((bsz, N_KV_GROUPS, seq, LANES), dt)
    tok_spec = pl.BlockSpec((None, N_KV_GROUPS, SEQ_TILE, LANES), lambda bi, si: (bi, 0, si, 0))
    tab_spec = pl.BlockSpec((SEQ_TILE, LANES), lambda bi, si: (si, 0))
    tab_t_spec = pl.BlockSpec((ROPE_HALF, SEQ_TILE), lambda bi, si: (0, si))
    n_rot = N_HEADS * ROPE_DIM
    return pl.pallas_call(
        _proj_body,
        grid=(bsz, seq // SEQ_TILE),
        in_specs=[pl.BlockSpec((None, SEQ_TILE, D_MODEL), lambda bi, si: (bi, si, 0)),
                  _const_spec((1, D_MODEL)), _const_spec((_PROJ_ROWS, D_MODEL)),
                  _const_spec((D_MODEL, _PROJ_COLS)),
                  tab_spec, tab_spec, tab_spec, tab_t_spec, tab_t_spec],
        out_specs=[pl.BlockSpec((None, Q_WIDTH, SEQ_TILE), lambda bi, si: (bi, 0, si)),
                   pl.BlockSpec((None, n_rot, SEQ_TILE), lambda bi, si: (bi, 0, si)),
                   feat_spec(V_ROWS), feat_spec(V_ROWS), feat_spec(GATE_ROWS),
                   tok_spec, tok_spec, tok_spec],
        out_shape=[jax.ShapeDtypeStruct((bsz, Q_WIDTH, seq), BF16),
                   jax.ShapeDtypeStruct((bsz, n_rot, seq), BF16),
                   feat(V_ROWS, BF16), feat(V_ROWS, BF16), feat(GATE_ROWS, F32),
                   tok(BF16), tok(BF16), tok(F32)],
        compiler_params=pltpu.CompilerParams(
            dimension_semantics=("parallel", "parallel"), vmem_limit_bytes=VMEM_LIMIT),
        name="nsa_proj",
    )(x, g2.reshape(1, -1), w_t, w_n, *tables)


def _cmp_body(kvc_ref, pos_ref, w1_ref, w2k_ref, w2vt_ref, kc_ref, vct_ref):
    first = jnp.zeros((N_CMP_PAD, 2 * CMP_HIDDEN), F32)
    second = jnp.zeros((N_CMP_PAD, 2 * CMP_HIDDEN), F32)
    for t in range(CMP_STRIDE):
        xt = kvc_ref[pl.ds(t, N_CMP_PAD, stride=CMP_STRIDE), :]
        za = (xt + pos_ref[t:t + 1, :]).astype(BF16)
        zb = (xt + pos_ref[CMP_STRIDE + t:CMP_STRIDE + t + 1, :]).astype(BF16)
        first = first + jnp.dot(za, w1_ref[t], preferred_element_type=F32)
        second = second + jnp.dot(zb, w1_ref[CMP_STRIDE + t], preferred_element_type=F32)
    row = lax.broadcasted_iota(jnp.int32, second.shape, 0)
    second = jnp.where(row < N_CMP_PAD - 1, pltpu.roll(second, N_CMP_PAD - 1, axis=0), 0.0)
    pre = first + second
    hid = (pre * jax.nn.sigmoid(pre)).astype(BF16)
    kc_ref[...] = jnp.dot(hid[:, :CMP_HIDDEN], w2k_ref[...], preferred_element_type=F32).astype(BF16)
    vct_ref[...] = lax.dot_general(w2vt_ref[...], hid[:, CMP_HIDDEN:], _NT,
                                   preferred_element_type=F32).astype(BF16)


def _compress(kvc, pos_k, pos_v, wk1, wk2, wv1, wv2):
    bsz, _, seq, _ = kvc.shape
    assert seq == N_CMP_PAD * CMP_STRIDE
    pos = jnp.concatenate([pos_k, pos_v], axis=1)
    k1 = wk1.reshape(CMP_BLOCK, HEAD_DIM, CMP_HIDDEN)
    v1 = wv1.reshape(CMP_BLOCK, HEAD_DIM, CMP_HIDDEN)
    z1 = jnp.zeros_like(k1)
    w1 = jnp.concatenate([jnp.concatenate([k1, z1], axis=2),
                          jnp.concatenate([z1, v1], axis=2)], axis=1).astype(BF16)
    w2k = (wk2 * LOG2_E).astype(BF16)
    w2vt = wv2.T.astype(BF16)
    return pl.pallas_call(
        _cmp_body,
        grid=(bsz, N_KV_GROUPS),
        in_specs=[pl.BlockSpec((None, None, seq, LANES), lambda bi, gi: (bi, gi, 0, 0)),
                  _const_spec(pos.shape), _const_spec(w1.shape),
                  _const_spec(w2k.shape), _const_spec(w2vt.shape)],
        out_specs=[pl.BlockSpec((None, None, N_CMP_PAD, HEAD_DIM), lambda bi, gi: (bi, gi, 0, 0)),
                   pl.BlockSpec((None, None, HEAD_DIM, N_CMP_PAD), lambda bi, gi: (bi, gi, 0, 0))],
        out_shape=[jax.ShapeDtypeStruct((bsz, N_KV_GROUPS, N_CMP_PAD, HEAD_DIM), BF16),
                   jax.ShapeDtypeStruct((bsz, N_KV_GROUPS, HEAD_DIM, N_CMP_PAD), BF16)],
        compiler_params=pltpu.CompilerParams(dimension_semantics=("parallel", "parallel")),
        name="nsa_compress",
    )(kvc, pos, w1, w2k, w2vt)


_HEAD_COLS = [slice(r * ATT_Q, (r + 1) * ATT_Q) for r in range(HEADS_PER_GROUP)]


def _score_stage(k_ref, k0, lhs_ref, s_out, t_out, mask=None):
    k_tile = k_ref[pl.ds(k0, ATT_K), :]
    t_max = []
    for cs in _HEAD_COLS:
        s = jnp.dot(k_tile, lhs_ref[:, cs], preferred_element_type=F32)
        if mask is not None:
            s = jnp.where(mask, s, NEG_INF)
        s_out[:, cs] = s
        t_max.append(jnp.max(s, axis=0, keepdims=True))
    t_out[...] = jnp.concatenate(t_max, axis=1)


def _softmax_stage(s_in, t_in, m_ref, p_out, a_out):
    m_all = m_ref[...]
    t_all = t_in[...]
    m_new, alpha = [], []
    for cs in _HEAD_COLS:
        s = s_in[:, cs]
        m_c = jnp.maximum(m_all[:, cs], t_all[:, cs])
        p_out[:, cs] = jnp.exp2((s - m_c).astype(BF16))
        alpha.append(jnp.exp2(m_all[:, cs] - m_c))
        m_new.append(m_c)
    m_ref[...] = jnp.concatenate(m_new, axis=1)
    a_out[...] = jnp.concatenate(alpha, axis=1)


def _value_stage(vt_ref, k0, p_in, a_in, acc_ref):
    vt = vt_ref[:, pl.ds(k0, ATT_K)]
    acc_all, a_all = acc_ref[...], a_in[...]
    acc_ref[...] = jnp.concatenate(
        [a_all[:, cs] * acc_all[:, cs] + jnp.dot(vt, p_in[:, cs], preferred_element_type=F32)
         for cs in _HEAD_COLS], axis=1)


def _tree_sum(terms):
    while len(terms) > 1:
        terms = [a + b for a, b in zip(terms[::2], terms[1::2])] + terms[len(terms) & ~1:]
    return terms[0]


def _select_body(qn_ref, kc_ref, vct_ref, ov_ref, bias_ref, ocmp_ref):
    q0 = pl.program_id(1) * ATT_Q
    heads = [(g, r) for g in range(N_KV_GROUPS) for r in range(HEADS_PER_GROUP)]
    rows = lambda g, r: slice((g * HEADS_PER_GROUP + r) * HEAD_DIM, (g * HEADS_PER_GROUP + r + 1) * HEAD_DIM)

    blk_end = lax.broadcasted_iota(jnp.int32, (N_CMP_PAD, ATT_Q), 0) * CMP_STRIDE + CMP_BLOCK - 1
    qpos = q0 + lax.broadcasted_iota(jnp.int32, (N_CMP_PAD, ATT_Q), 1)
    cmp_mask = blk_end <= qpos
    any_cmp = q0 + lax.broadcasted_iota(jnp.int32, (1, ATT_Q), 1) >= CMP_BLOCK - 1
    scores = [jnp.dot(kc_ref[g], qn_ref[rows(g, r), :], preferred_element_type=F32) for g, r in heads]
    probs = []
    for s in scores:
        s = jnp.where(cmp_mask, s, NEG_INF)
        e = jnp.exp2(s - jnp.max(s, axis=0, keepdims=True))
        probs.append(e * jnp.where(any_cmp, 1.0 / jnp.sum(e, axis=0, keepdims=True), 0.0))
    for (g, r), p in zip(heads, probs):
        ocmp_ref[rows(g, r), :] = jnp.dot(vct_ref[g], p.astype(BF16), preferred_element_type=F32)

    p_sum = jnp.concatenate(
        [_tree_sum(probs[g * HEADS_PER_GROUP:(g + 1) * HEADS_PER_GROUP]) for g in range(N_KV_GROUPS)], axis=1)
    hi = p_sum.astype(BF16)
    rem = p_sum - hi.astype(F32)
    mid = rem.astype(BF16)
    lo = (rem - mid.astype(F32)).astype(BF16)
    ov = ov_ref[...]
    imp = (jnp.dot(ov, hi, preferred_element_type=F32) + jnp.dot(ov, mid, preferred_element_type=F32)
           + jnp.dot(ov, lo, preferred_element_type=F32))
    cols = N_KV_GROUPS * ATT_Q
    j = lax.broadcasted_iota(jnp.int32, (N_SEL_BLOCKS, cols), 0)
    q_in = lax.broadcasted_iota(jnp.int32, (N_SEL_BLOCKS, cols), 1) & (ATT_Q - 1)
    q_blk = (q0 + q_in) // SEL_BLOCK
    forced = (j == 0) | (j == q_blk) | (j == q_blk - 1)
    imp = jnp.where(forced, jnp.inf, jnp.where(j > q_blk, -jnp.inf, imp))
    sub = 8
    row_in = lax.broadcasted_iota(jnp.int32, (sub, cols), 0)
    bias = []
    for b0 in range(0, N_SEL_BLOCKS, sub):
        mine = imp[b0:b0 + sub, :]
        beats = []
        for jp in range(N_SEL_BLOCKS):
            other = imp[jp:jp + 1, :]
            ge = jnp.where(other >= mine, 1.0, 0.0)
            gt = jnp.where(other > mine, 1.0, 0.0)
            if jp < b0:
                beats.append(ge)
            elif jp >= b0 + sub:
                beats.append(gt)
            else:
                beats.append(jnp.where(row_in > jp - b0, ge, gt))
        bias.append(jnp.where(_tree_sum(beats) < N_SELECT, 0.0, NEG_INF))
    bias = jnp.concatenate(bias, axis=0).astype(BF16)
    for g in range(N_KV_GROUPS):
        bias_ref[g] = bias[:, g * ATT_Q:(g + 1) * ATT_Q]


def _select(qn, kc, vct):
    bsz, _, seq = qn.shape
    assert seq // SEL_BLOCK == N_SEL_BLOCKS
    grp = lambda *blk: pl.BlockSpec((None, N_KV_GROUPS) + blk, lambda bi, qi: (bi, 0, 0, 0))
    q_spec = pl.BlockSpec((None, Q_WIDTH, ATT_Q), lambda bi, qi: (bi, 0, qi))
    return pl.pallas_call(
        _select_body,
        grid=(bsz, seq // ATT_Q),
        in_specs=[q_spec, grp(N_CMP_PAD, HEAD_DIM), grp(HEAD_DIM, N_CMP_PAD),
                  _const_spec((N_SEL_BLOCKS, N_CMP_PAD))],
        out_specs=[pl.BlockSpec((None, N_KV_GROUPS, N_SEL_BLOCKS, ATT_Q), lambda bi, qi: (bi, 0, 0, qi)),
                   q_spec],
        out_shape=[jax.ShapeDtypeStruct((bsz, N_KV_GROUPS, N_SEL_BLOCKS, seq), BF16),
                   jax.ShapeDtypeStruct((bsz, Q_WIDTH, seq), F32)],
        compiler_params=pltpu.CompilerParams(dimension_semantics=("parallel", "parallel")),
        name="nsa_select",
    )(qn, kc, vct, _overlap(seq))


def _window_branch(q0, prev0, vwt_ref, sw_ref, tw_ref, pw_ref, accw_ref):
    m_all = jnp.maximum(tw_ref[0], tw_ref[1])
    for cs in _HEAD_COLS:
        m_w = m_all[:, cs]
        pw_ref[0, :, cs] = jnp.exp2((sw_ref[0, :, cs] - m_w).astype(BF16))
        pw_ref[1, :, cs] = jnp.exp2((sw_ref[1, :, cs] - m_w).astype(BF16))
    vw_diag = vwt_ref[:, pl.ds(q0, ATT_K)]
    vw_prev = vwt_ref[:, pl.ds(prev0, ATT_K)]
    for cs in _HEAD_COLS:
        accw_ref[:, cs] = (jnp.dot(vw_diag, pw_ref[0, :, cs], preferred_element_type=F32)
                           + jnp.dot(vw_prev, pw_ref[1, :, cs], preferred_element_type=F32))


def _attn_body(qn_ref, qr_ref, bias_ref, ocmp_ref, ks_ref, kw_ref, vst_ref, vwt_ref, gate_ref,
               o_ref, lhs_ref, s_ref, sw_ref, p_ref, pw_ref, t_ref, tw_ref, a_ref, m_ref, acc_ref, accw_ref):
    qi = pl.program_id(2)
    q0 = pl.multiple_of(qi * ATT_Q, ATT_Q)
    prev0 = pl.multiple_of(jnp.maximum(qi - 1, 0) * ATT_K, ATT_K)
    groups = range(GROUPS_PER_STEP)
    gh = HEADS_PER_GROUP * HEAD_DIM

    pad = jnp.zeros((LANES - HEAD_DIM - N_SEL_BLOCKS, ATT_Q), BF16)
    for g in groups:
        for r in range(HEADS_PER_GROUP):
            h = g * HEADS_PER_GROUP + r
            lhs_ref[g, :, r * ATT_Q:(r + 1) * ATT_Q] = jnp.concatenate(
                [qr_ref[h * ROPE_DIM:(h + 1) * ROPE_DIM, :],
                 qn_ref[h * HEAD_DIM + ROPE_DIM:(h + 1) * HEAD_DIM, :], bias_ref[g], pad], axis=0)


    key_in = lax.broadcasted_iota(jnp.int32, (ATT_K, ATT_Q), 0)
    col_in = lax.broadcasted_iota(jnp.int32, (ATT_K, ATT_Q), 1)
    causal = key_in <= col_in
    prev = (key_in > col_in) & (qi > 0)

    def window_scores(g):
        _score_stage(kw_ref.at[g], q0, lhs_ref.at[g], sw_ref.at[g, 0], tw_ref.at[g, 0], mask=causal)
        _score_stage(kw_ref.at[g], prev0, lhs_ref.at[g], sw_ref.at[g, 1], tw_ref.at[g, 1], mask=prev)

    def window_rest(g):
        _window_branch(q0, prev0, vwt_ref.at[g], sw_ref.at[g], tw_ref.at[g], pw_ref.at[g], accw_ref.at[g])

    for g in groups:
        m_ref[g] = jnp.full(m_ref.shape[1:], NEG_INF, F32)
        acc_ref[g] = jnp.zeros(acc_ref.shape[1:], F32)
    first_mask = causal | (qi > 0)
    for g in groups:
        window_scores(g)
        if g > 0:
            window_rest(g - 1)
        _score_stage(ks_ref.at[g], 0, lhs_ref.at[g], s_ref.at[g, 0], t_ref.at[g, 0], mask=first_mask)
    window_rest(groups[-1])

    def trip(j, cur):
        nxt = 1 - cur
        mask = causal | (j + 1 < qi)
        for g in groups:
            _score_stage(ks_ref.at[g], pl.multiple_of((j + 1) * ATT_K, ATT_K), lhs_ref.at[g],
                         s_ref.at[g, nxt], t_ref.at[g, nxt], mask=mask)
            _softmax_stage(s_ref.at[g, cur], t_ref.at[g, cur], m_ref.at[g], p_ref.at[g, cur], a_ref.at[g, cur])
            _value_stage(vst_ref.at[g], pl.multiple_of(j * ATT_K, ATT_K),
                         p_ref.at[g, cur], a_ref.at[g, cur], acc_ref.at[g])

    def trip_any(j, carry):
        pl.when(j % 2 == 0)(lambda: trip(j, 0))
        pl.when(j % 2 == 1)(lambda: trip(j, 1))
        return carry

    lax.fori_loop(0, qi, trip_any, 0)

    def drain(last):
        for g in groups:
            _softmax_stage(s_ref.at[g, last], t_ref.at[g, last], m_ref.at[g], p_ref.at[g, last], a_ref.at[g, last])
            _value_stage(vst_ref.at[g], q0, p_ref.at[g, last], a_ref.at[g, last], acc_ref.at[g])

    pl.when(qi % 2 == 0)(lambda: drain(0))
    pl.when(qi % 2 == 1)(lambda: drain(1))

    for g in groups:
        gate = gate_ref[g]
        heads = []
        for r, cs in enumerate(_HEAD_COLS):
            g_cmp = gate[r:r + 1, :]
            g_sel = gate[HEADS_PER_GROUP + r:HEADS_PER_GROUP + r + 1, :]
            g_win = gate[2 * HEADS_PER_GROUP + r:2 * HEADS_PER_GROUP + r + 1, :]
            o_sel = acc_ref[g, :HEAD_DIM, cs] / acc_ref[g, HEAD_DIM:HEAD_DIM + 1, cs]
            o_win = accw_ref[g, :HEAD_DIM, cs] / accw_ref[g, HEAD_DIM:HEAD_DIM + 1, cs]
            h = g * HEADS_PER_GROUP + r
            o_cmp = ocmp_ref[h * HEAD_DIM:(h + 1) * HEAD_DIM, :]
            heads.append(g_cmp * o_cmp + g_sel * o_sel + g_win * o_win)
        o_ref[:, g * gh:(g + 1) * gh] = jnp.concatenate(heads, axis=0).T.astype(BF16)


def _overlap(seq):
    ci = jnp.arange(N_CMP_PAD) * CMP_STRIDE
    sj = jnp.arange(seq // SEL_BLOCK) * SEL_BLOCK
    n_cmp = (seq - CMP_BLOCK) // CMP_STRIDE + 1
    ov = ((ci[None, :] < sj[:, None] + SEL_BLOCK) & (ci[None, :] + CMP_BLOCK > sj[:, None])
          & (jnp.arange(N_CMP_PAD)[None, :] < n_cmp))
    return ov.astype(BF16)


def _attention(qn, qr, bias, ocmp, ks, kw, vst, vwt, gate):
    bsz, _, seq = qn.shape
    assert ATT_Q == WINDOW == ATT_K
    gps = GROUPS_PER_STEP
    gh = gps * HEADS_PER_GROUP * HEAD_DIM
    grp = lambda *blk: pl.BlockSpec((None, gps) + blk, lambda bi, gi, qi: (bi, gi, 0, 0))
    q_spec = pl.BlockSpec((None, gh, ATT_Q), lambda bi, gi, qi: (bi, gi, qi))
    return pl.pallas_call(
        _attn_body,
        grid=(bsz, N_KV_GROUPS // gps, seq // ATT_Q),
        in_specs=[q_spec,
                  pl.BlockSpec((None, gps * HEADS_PER_GROUP * ROPE_DIM, ATT_Q), lambda bi, gi, qi: (bi, gi, qi)),
                  pl.BlockSpec((None, gps, N_SEL_BLOCKS, ATT_Q), lambda bi, gi, qi: (bi, gi, 0, qi)),
                  q_spec,
                  grp(seq, LANES), grp(seq, LANES), grp(V_ROWS, seq), grp(V_ROWS, seq),
                  pl.BlockSpec((None, gps, GATE_ROWS, ATT_Q), lambda bi, gi, qi: (bi, gi, 0, qi))],
        out_specs=pl.BlockSpec((None, ATT_Q, gh), lambda bi, gi, qi: (bi, qi, gi)),
        out_shape=jax.ShapeDtypeStruct((bsz, seq, Q_WIDTH), BF16),
        scratch_shapes=[pltpu.VMEM((gps, LANES, ATT_COLS), BF16),
                        pltpu.VMEM((gps, 2, ATT_K, ATT_COLS), F32),
                        pltpu.VMEM((gps, 2, ATT_K, ATT_COLS), F32),
                        pltpu.VMEM((gps, 2, ATT_K, ATT_COLS), BF16),
                        pltpu.VMEM((gps, 2, ATT_K, ATT_COLS), BF16),
                        pltpu.VMEM((gps, 2, 1, ATT_COLS), F32),
                        pltpu.VMEM((gps, 2, 1, ATT_COLS), F32),
                        pltpu.VMEM((gps, 2, 1, ATT_COLS), F32),
                        pltpu.VMEM((gps, 1, ATT_COLS), F32),
                        pltpu.VMEM((gps, V_ROWS, ATT_COLS), F32),
                        pltpu.VMEM((gps, V_ROWS, ATT_COLS), F32)],
        compiler_params=pltpu.CompilerParams(
            dimension_semantics=("parallel", "parallel", "arbitrary"), vmem_limit_bytes=VMEM_LIMIT),
        name="nsa_attention",
    )(qn, qr, bias, ocmp, ks, kw, vst, vwt, gate)


def _oproj_body(o_ref_in, x_ref, w_ref, g_ref, out_ref):
    m = jnp.dot(o_ref_in[...], w_ref[...], preferred_element_type=F32)
    out_ref[...] = x_ref[...] + _rms(m, g_ref[...])


def _out_proj(o2d, x2d, w_o, g3):
    t = x2d.shape[0]
    row = pl.BlockSpec((FFN_TOKENS, D_MODEL), lambda i: (i, 0))
    return pl.pallas_call(
        _oproj_body,
        grid=(t // FFN_TOKENS,),
        in_specs=[row, row, _const_spec((Q_WIDTH, D_MODEL)), _const_spec((1, D_MODEL))],
        out_specs=row,
        out_shape=jax.ShapeDtypeStruct((t, D_MODEL), F32),
        compiler_params=pltpu.CompilerParams(dimension_semantics=("parallel",)),
        name="nsa_out_proj",
    )(o2d, x2d, w_o.astype(BF16), g3.reshape(1, -1))


def _nsa_block(x, g2, g3, w_in, pos_k, pos_v, wk1, wk2, wv1, wv2, w_o):
    bsz, seq, d = x.shape
    w_t, w_n = _nsa_proj_weights(w_in)
    qn, qr, vst, vwt, gate, ks, kw, kvc = _nsa_proj(x, g2, w_t, w_n, _rope_tables(seq))
    kc, vct = _compress(kvc, pos_k, pos_v, wk1, wk2, wv1, wv2)
    bias, ocmp = _select(qn, kc, vct)
    o = _attention(qn, qr, bias, ocmp, ks, kw, vst, vwt, gate)
    return _out_proj(o.reshape(bsz * seq, Q_WIDTH), x.reshape(bsz * seq, d), w_o, g3).reshape(x.shape)


def kernel(x, norm_gains, ffn_w_gate, ffn_w_up, ffn_w_down, pool_w, pool_b, pool_scale, nsa_w_in, nsa_cmp_pos_k, nsa_cmp_pos_v, nsa_cmp_wk1, nsa_cmp_wk2, nsa_cmp_wv1, nsa_cmp_wv2, nsa_w_o):
    bsz, seq, d = x.shape
    depth = norm_gains.shape[0]

    def ffn(x, i, half):
        g = norm_gains[i]
        y = _ffn_block(x.reshape(bsz * seq, d), g[4 * half], g[4 * half + 1],
                       ffn_w_gate[i, half], ffn_w_up[i, half], ffn_w_down[i, half])
        return y.reshape(bsz, seq, d)

    for i in range(depth):
        g = norm_gains[i]
        x = ffn(x, i, 0)
        j = i // 2
        if i % 2 == 0:
            x = _pool_block(x, g[2], g[3], pool_w[j], pool_b[j], pool_scale[j])
        else:
            x = _nsa_block(x, g[2], g[3], nsa_w_in[j], nsa_cmp_pos_k[j], nsa_cmp_pos_v[j],
                           nsa_cmp_wk1[j], nsa_cmp_wk2[j], nsa_cmp_wv1[j], nsa_cmp_wv2[j], nsa_w_o[j])
        x = ffn(x, i, 1)
    return x
```

```python
import functools

import jax
import jax.numpy as jnp
from jax import lax
from jax.experimental import pallas as pl
from jax.experimental.pallas import tpu as pltpu

F32 = jnp.float32
BF16 = jnp.bfloat16

D_MODEL = 1024
EPS = 1e-6
D_FF = 2816
POOL_WINDOWS = (2, 4, 8, 16)
POOL_GROUP = D_MODEL // len(POOL_WINDOWS)
MAX_POOL_WINDOW = max(POOL_WINDOWS)
N_HEADS = 16
HEAD_DIM = 64
N_KV_GROUPS = 4
HEADS_PER_GROUP = N_HEADS // N_KV_GROUPS
ROPE_DIM = HEAD_DIM // 4
ROPE_THETA = 500000.0
CMP_BLOCK = 32
CMP_STRIDE = 16
CMP_HIDDEN = 256
SEL_BLOCK = 64
N_SELECT = 8
WINDOW = 256
Q_WIDTH = N_HEADS * HEAD_DIM
KV_WIDTH = N_KV_GROUPS * HEAD_DIM
GATE_WIDTH = 3 * N_HEADS
NEG_INF = -1e30

LANES = 128
VMEM_LIMIT = 56 * 1024 * 1024

FFN_TOKENS = 512
SEQ_TILE = 512
ATT_Q = 256
ATT_K = 256
N_CMP_PAD = 128
GROUPS_PER_STEP = 4

N_SEL_BLOCKS = 32
GATE_ROWS = 16
ROPE_HALF = ROPE_DIM // 2
V_ROWS = HEAD_DIM + 16
LOG2_E = 1.4426950408889634
ATT_COLS = HEADS_PER_GROUP * ATT_Q
N_CHUNKS = ATT_COLS // LANES

_ROW_VS = Q_WIDTH
_ROW_VW = _ROW_VS + KV_WIDTH
_ROW_GATE = _ROW_VW + KV_WIDTH
_PROJ_ROWS = _ROW_GATE + N_KV_GROUPS * GATE_ROWS
_G_COLS = N_KV_GROUPS * LANES
_OFF_KW = _G_COLS
_OFF_KVC = 2 * _G_COLS
_PROJ_COLS = 3 * _G_COLS

_NT = (((1,), (1,)), ((), ()))


def _rms(x, g):
    return x * lax.rsqrt(jnp.mean(x * x, axis=-1, keepdims=True) + EPS) * g


def _const_spec(shape):
    nd = len(shape)
    return pl.BlockSpec(shape, lambda *_: (0,) * nd, pipeline_mode=pl.Buffered(1))


def _ffn_body(x_ref, gpre_ref, gpost_ref, wg_ref, wu_ref, wd_ref, o_ref):
    x = x_ref[...]
    xb = _rms(x, gpre_ref[...]).astype(BF16)
    hg = jnp.dot(xb, wg_ref[...], preferred_element_type=F32)
    hu = jnp.dot(xb, wu_ref[...], preferred_element_type=F32)
    act = (hg * jax.nn.sigmoid(hg) * hu).astype(BF16)
    f = jnp.dot(act, wd_ref[...], preferred_element_type=F32)
    o_ref[...] = x + 0.5 * _rms(f, gpost_ref[...])


def _ffn_block(x2d, g_pre, g_post, wg, wu, wd):
    t = x2d.shape[0]
    row = pl.BlockSpec((FFN_TOKENS, D_MODEL), lambda i: (i, 0))
    return pl.pallas_call(
        _ffn_body,
        grid=(t // FFN_TOKENS,),
        in_specs=[row, _const_spec((1, D_MODEL)), _const_spec((1, D_MODEL)),
                  _const_spec((D_MODEL, D_FF)), _const_spec((D_MODEL, D_FF)),
                  _const_spec((D_FF, D_MODEL))],
        out_specs=row,
        out_shape=jax.ShapeDtypeStruct((t, D_MODEL), F32),
        compiler_params=pltpu.CompilerParams(
            dimension_semantics=("parallel",), vmem_limit_bytes=VMEM_LIMIT),
        name="ffn_block",
    )(x2d, g_pre.reshape(1, -1), g_post.reshape(1, -1),
      wg.astype(BF16), wu.astype(BF16), wd.astype(BF16))


def _pool_body(x_ref, g2_ref, g3_ref, w_ref, b_ref, sc_ref, o_ref, carry_ref):
    si = pl.program_id(1)

    @pl.when(si == 0)
    def _():
        carry_ref[...] = jnp.zeros_like(carry_ref)

    x = x_ref[...]
    h = _rms(x, g2_ref[...])
    hp = jnp.concatenate([carry_ref[...], h], axis=0)
    carry_ref[...] = h[SEQ_TILE - MAX_POOL_WINDOW:, :]
    pos = si * SEQ_TILE + lax.broadcasted_iota(jnp.int32, (SEQ_TILE, 1), 0)
    ys = []
    for g, w in enumerate(POOL_WINDOWS):
        cols = slice(g * POOL_GROUP, (g + 1) * POOL_GROUP)
        acc = hp[:, cols]
        k = 1
        while k < w:
            acc = acc + pltpu.roll(acc, k, axis=0)
            k *= 2
        cnt = jnp.minimum(pos + 1, w).astype(F32)
        d = acc[MAX_POOL_WINDOW:, :] / cnt - h[:, cols]
        y = jnp.dot(d.astype(BF16), w_ref[g], preferred_element_type=F32) + b_ref[g:g + 1, :]
        ys.append(y)
    m = jnp.concatenate(ys, axis=1) * sc_ref[...]
    o_ref[...] = x + _rms(m, g3_ref[...])


def _pool_block(x, g2, g3, w, b, scale):
    bsz, seq, _ = x.shape
    row = pl.BlockSpec((None, SEQ_TILE, D_MODEL), lambda bi, si: (bi, si, 0))
    return pl.pallas_call(
        _pool_body,
        grid=(bsz, seq // SEQ_TILE),
        in_specs=[row, _const_spec((1, D_MODEL)), _const_spec((1, D_MODEL)),
                  _const_spec((len(POOL_WINDOWS), POOL_GROUP, POOL_GROUP)),
                  _const_spec((len(POOL_WINDOWS), POOL_GROUP)), _const_spec((1, D_MODEL))],
        out_specs=row,
        out_shape=jax.ShapeDtypeStruct(x.shape, F32),
        scratch_shapes=[pltpu.VMEM((MAX_POOL_WINDOW, D_MODEL), F32)],
        compiler_params=pltpu.CompilerParams(
            dimension_semantics=("parallel", "arbitrary"), vmem_limit_bytes=VMEM_LIMIT),
        name="pool_block",
    )(x, g2.reshape(1, -1), g3.reshape(1, -1), w.astype(BF16), b, scale.reshape(1, -1))


def _proj_body(x_ref, g2_ref, wt_ref, wn_ref, c_ref, sa_ref, sb_ref, ct_ref, st_ref,
               qn_ref, qr_ref, vs_ref, vw_ref, gate_ref, ks_ref, kw_ref, kvc_ref):
    si = pl.program_id(1)
    hb = _rms(x_ref[...], g2_ref[...]).astype(BF16)

    pt = lax.dot_general(wt_ref[...], hb, _NT, preferred_element_type=F32)
    qn_ref[...] = pt[:Q_WIDTH].astype(BF16)
    cos_t, sin_t = ct_ref[...], st_ref[...]
    for h in range(N_HEADS):
        x1 = pt[h * HEAD_DIM:h * HEAD_DIM + ROPE_HALF]
        x2 = pt[h * HEAD_DIM + ROPE_HALF:h * HEAD_DIM + ROPE_DIM]
        rot = jnp.concatenate([x1 * cos_t - x2 * sin_t, x1 * sin_t + x2 * cos_t], axis=0)
        qr_ref[h * ROPE_DIM:(h + 1) * ROPE_DIM, :] = rot.astype(BF16)
    ones_rows = jnp.ones((V_ROWS - HEAD_DIM, SEQ_TILE), F32)
    for g in range(N_KV_GROUPS):
        vs_ref[g] = jnp.concatenate(
            [pt[_ROW_VS + g * HEAD_DIM:_ROW_VS + (g + 1) * HEAD_DIM], ones_rows], axis=0).astype(BF16)
        vw_ref[g] = jnp.concatenate(
            [pt[_ROW_VW + g * HEAD_DIM:_ROW_VW + (g + 1) * HEAD_DIM], ones_rows], axis=0).astype(BF16)
        gate_ref[g] = jax.nn.sigmoid(pt[_ROW_GATE + g * GATE_ROWS:_ROW_GATE + (g + 1) * GATE_ROWS])

    pn = jnp.dot(hb, wn_ref[...], preferred_element_type=F32)
    cos, sin_a, sin_b = c_ref[...], sa_ref[...], sb_ref[...]

    def slab(off, i):
        return pn[:, off + i * LANES: off + (i + 1) * LANES]

    def rope(t):
        return (t * cos + pltpu.roll(t, LANES - ROPE_HALF, axis=1) * sin_a
                + pltpu.roll(t, ROPE_HALF, axis=1) * sin_b)

    pos = si * SEQ_TILE + lax.broadcasted_iota(jnp.int32, (SEQ_TILE, LANES), 0)
    lane = lax.broadcasted_iota(jnp.int32, (SEQ_TILE, LANES), 1)
    blk_onehot = jnp.where(lane - HEAD_DIM == pos // SEL_BLOCK, 1.0, 0.0)
    for g in range(N_KV_GROUPS):
        ks_ref[g] = (rope(slab(0, g)) + blk_onehot).astype(BF16)
        kw_ref[g] = rope(slab(_OFF_KW, g)).astype(BF16)
        kvc_ref[g] = slab(_OFF_KVC, g)


def _nsa_proj_weights(w_in):
    d = w_in.shape[0]
    wq = w_in[:, :Q_WIDTH] * (HEAD_DIM ** -0.5)
    kv = w_in[:, Q_WIDTH:Q_WIDTH + 6 * KV_WIDTH].reshape(d, 6, N_KV_GROUPS, HEAD_DIM)
    wkc, wvc, wks, wvs, wkw, wvw = [kv[:, i] for i in range(6)]
    wgate = w_in[:, Q_WIDTH + 6 * KV_WIDTH:].reshape(d, 3, N_KV_GROUPS, HEADS_PER_GROUP)
    wgate = wgate.transpose(0, 2, 1, 3).reshape(d, N_KV_GROUPS, 3 * HEADS_PER_GROUP)
    wgate = jnp.pad(wgate, ((0, 0), (0, 0), (0, GATE_ROWS - 3 * HEADS_PER_GROUP)))
    w_t = jnp.concatenate([wq, wvs.reshape(d, KV_WIDTH), wvw.reshape(d, KV_WIDTH),
                           wgate.reshape(d, N_KV_GROUPS * GATE_ROWS)], axis=1).T.astype(BF16)
    zero = jnp.zeros_like(wkc)

    def pair(a, b):
        return jnp.concatenate([a, b], axis=-1).reshape(d, _G_COLS)

    w_n = jnp.concatenate([pair(wks * LOG2_E, zero), pair(wkw * LOG2_E, zero), pair(wkc, wvc)],
                          axis=1).astype(BF16)
    return w_t, w_n


def _rope_tables(seq):
    inv = 1.0 / (ROPE_THETA ** (jnp.arange(0, ROPE_DIM, 2, dtype=F32) / ROPE_DIM))
    ang = jnp.arange(seq, dtype=F32)[:, None] * inv[None, :]
    cos, sin = jnp.cos(ang), jnp.sin(ang)
    pad = LANES - ROPE_DIM
    c = jnp.concatenate([cos, cos, jnp.ones((seq, pad), F32)], axis=1)
    sa = jnp.concatenate([-sin, jnp.zeros((seq, ROPE_HALF + pad), F32)], axis=1)
    sb = jnp.concatenate([jnp.zeros((seq, ROPE_HALF), F32), sin, jnp.zeros((seq, pad), F32)], axis=1)
    return c, sa, sb, cos.T, sin.T


def _nsa_proj(x, g2, w_t, w_n, tables):
    bsz, seq, _ = x.shape
    feat = lambda rows, dt: jax.ShapeDtypeStruct((bsz, N_KV_GROUPS, rows, seq), dt)
    feat_spec = lambda rows: pl.BlockSpec((None, N_KV_GROUPS, rows, SEQ_TILE), lambda bi, si: (bi, 0, 0, si))
    tok = lambda dt: jax.ShapeDtypeStruct((bsz, N_KV_GROUPS, seq, LANES), dt)
    tok_spec = pl.BlockSpec((None, N_KV_GROUPS, SEQ_TILE, LANES), lambda bi, si: (bi, 0, si, 0))
    tab_spec = pl.BlockSpec((SEQ_TILE, LANES), lambda bi, si: (si, 0))
    tab_t_spec = pl.BlockSpec((ROPE_HALF, SEQ_TILE), lambda bi, si: (0, si))
    n_rot = N_HEADS * ROPE_DIM
    return pl.pallas_call(
        _proj_body,
        grid=(bsz, seq // SEQ_TILE),
        in_specs=[pl.BlockSpec((None, SEQ_TILE, D_MODEL), lambda bi, si: (bi, si, 0)),
                  _const_spec((1, D_MODEL)), _const_spec((_PROJ_ROWS, D_MODEL)),
                  _const_spec((D_MODEL, _PROJ_COLS)),
                  tab_spec, tab_spec, tab_spec, tab_t_spec, tab_t_spec],
        out_specs=[pl.BlockSpec((None, Q_WIDTH, SEQ_TILE), lambda bi, si: (bi, 0, si)),
                   pl.BlockSpec((None, n_rot, SEQ_TILE), lambda bi, si: (bi, 0, si)),
                   feat_spec(V_ROWS), feat_spec(V_ROWS), feat_spec(GATE_ROWS),
                   tok_spec, tok_spec, tok_spec],
        out_shape=[jax.ShapeDtypeStruct((bsz, Q_WIDTH, seq), BF16),
                   jax.ShapeDtypeStruct((bsz, n_rot, seq), BF16),
                   feat(V_ROWS, BF16), feat(V_ROWS, BF16), feat(GATE_ROWS, F32),
                   tok(BF16), tok(BF16), tok(F32)],
        compiler_params=pltpu.CompilerParams(
            dimension_semantics=("parallel", "parallel"), vmem_limit_bytes=VMEM_LIMIT),
        name="nsa_proj",
    )(x, g2.reshape(1, -1), w_t, w_n, *tables)


def _cmp_body(kvc_ref, pos_ref, w1_ref, w2k_ref, w2vt_ref, kc_ref, vct_ref):
    za, zb = [], []
    for t in range(CMP_STRIDE):
        xt = kvc_ref[pl.ds(t, N_CMP_PAD, stride=CMP_STRIDE), :]
        za.append((xt + pos_ref[t:t + 1, :]).astype(BF16))
        zb.append((xt + pos_ref[CMP_STRIDE + t:CMP_STRIDE + t + 1, :]).astype(BF16))
    first = jnp.dot(jnp.concatenate(za, axis=1), w1_ref[0], preferred_element_type=F32)
    second = jnp.dot(jnp.concatenate(zb, axis=1), w1_ref[1], preferred_element_type=F32)
    row = lax.broadcasted_iota(jnp.int32, second.shape, 0)
    second = jnp.where(row < N_CMP_PAD - 1, pltpu.roll(second, N_CMP_PAD - 1, axis=0), 0.0)
    pre = first + second
    hid = (pre * jax.nn.sigmoid(pre)).astype(BF16)
    kc_ref[...] = jnp.dot(hid[:, :CMP_HIDDEN], w2k_ref[...], preferred_element_type=F32).astype(BF16)
    vct_ref[...] = lax.dot_general(w2vt_ref[...], hid[:, CMP_HIDDEN:], _NT,
                                   preferred_element_type=F32).astype(BF16)


def _compress(kvc, pos_k, pos_v, wk1, wk2, wv1, wv2):
    bsz, _, seq, _ = kvc.shape
    assert seq == N_CMP_PAD * CMP_STRIDE
    pos = jnp.concatenate([pos_k, pos_v], axis=1)
    k1 = wk1.reshape(CMP_BLOCK, HEAD_DIM, CMP_HIDDEN)
    v1 = wv1.reshape(CMP_BLOCK, HEAD_DIM, CMP_HIDDEN)
    z1 = jnp.zeros_like(k1)
    w1 = jnp.concatenate([jnp.concatenate([k1, z1], axis=2),
                          jnp.concatenate([z1, v1], axis=2)], axis=1).astype(BF16)
    w1 = w1.reshape(2, CMP_STRIDE * LANES, 2 * CMP_HIDDEN)
    w2k = (wk2 * LOG2_E).astype(BF16)
    w2vt = wv2.T.astype(BF16)
    return pl.pallas_call(
        _cmp_body,
        grid=(bsz, N_KV_GROUPS),
        in_specs=[pl.BlockSpec((None, None, seq, LANES), lambda bi, gi: (bi, gi, 0, 0)),
                  _const_spec(pos.shape), _const_spec(w1.shape),
                  _const_spec(w2k.shape), _const_spec(w2vt.shape)],
        out_specs=[pl.BlockSpec((None, None, N_CMP_PAD, HEAD_DIM), lambda bi, gi: (bi, gi, 0, 0)),
                   pl.BlockSpec((None, None, HEAD_DIM, N_CMP_PAD), lambda bi, gi: (bi, gi, 0, 0))],
        out_shape=[jax.ShapeDtypeStruct((bsz, N_KV_GROUPS, N_CMP_PAD, HEAD_DIM), BF16),
                   jax.ShapeDtypeStruct((bsz, N_KV_GROUPS, HEAD_DIM, N_CMP_PAD), BF16)],
        compiler_params=pltpu.CompilerParams(dimension_semantics=("parallel", "parallel")),
        name="nsa_compress",
    )(kvc, pos, w1, w2k, w2vt)


_HEAD_COLS = [slice(r * ATT_Q, (r + 1) * ATT_Q) for r in range(HEADS_PER_GROUP)]


def _score_stage(k_ref, k0, lhs_ref, s_out, t_out, mask=None):
    k_tile = k_ref[pl.ds(k0, ATT_K), :]
    t_max = []
    for cs in _HEAD_COLS:
        s = jnp.dot(k_tile, lhs_ref[:, cs], preferred_element_type=F32)
        if mask is not None:
            s = jnp.where(mask, s, NEG_INF)
        s_out[:, cs] = s
        t_max.append(jnp.max(s, axis=0, keepdims=True))
    t_out[...] = jnp.concatenate(t_max, axis=1)


def _softmax_stage(s_in, t_in, m_ref, p_out, a_out):
    m_all = m_ref[...]
    t_all = t_in[...]
    m_new, alpha = [], []
    for cs in _HEAD_COLS:
        s = s_in[:, cs]
        m_c = jnp.maximum(m_all[:, cs], t_all[:, cs])
        p_out[:, cs] = jnp.exp2((s - m_c).astype(BF16))
        alpha.append(jnp.exp2(m_all[:, cs] - m_c))
        m_new.append(m_c)
    m_ref[...] = jnp.concatenate(m_new, axis=1)
    a_out[...] = jnp.concatenate(alpha, axis=1)


def _value_stage(vt_ref, k0, p_in, a_in, acc_ref):
    vt = vt_ref[:, pl.ds(k0, ATT_K)]
    acc_all, a_all = acc_ref[...], a_in[...]
    acc_ref[...] = jnp.concatenate(
        [a_all[:, cs] * acc_all[:, cs] + jnp.dot(vt, p_in[:, cs], preferred_element_type=F32)
         for cs in _HEAD_COLS], axis=1)


def _tree_sum(terms):
    while len(terms) > 1:
        terms = [a + b for a, b in zip(terms[::2], terms[1::2])] + terms[len(terms) & ~1:]
    return terms[0]


def _select_body(qn_ref, kc_ref, vct_ref, ov_ref, bias_ref, ocmp_ref):
    q0 = pl.program_id(1) * ATT_Q
    heads = [(g, r) for g in range(N_KV_GROUPS) for r in range(HEADS_PER_GROUP)]
    rows = lambda g, r: slice((g * HEADS_PER_GROUP + r) * HEAD_DIM, (g * HEADS_PER_GROUP + r + 1) * HEAD_DIM)

    blk_end = lax.broadcasted_iota(jnp.int32, (N_CMP_PAD, ATT_Q), 0) * CMP_STRIDE + CMP_BLOCK - 1
    qpos = q0 + lax.broadcasted_iota(jnp.int32, (N_CMP_PAD, ATT_Q), 1)
    cmp_mask = blk_end <= qpos
    any_cmp = q0 + lax.broadcasted_iota(jnp.int32, (1, ATT_Q), 1) >= CMP_BLOCK - 1
    scores = [jnp.dot(kc_ref[g], qn_ref[rows(g, r), :], preferred_element_type=F32) for g, r in heads]
    probs = []
    for s in scores:
        s = jnp.where(cmp_mask, s, NEG_INF)
        e = jnp.exp2(s - jnp.max(s, axis=0, keepdims=True))
        probs.append(e * jnp.where(any_cmp, 1.0 / jnp.sum(e, axis=0, keepdims=True), 0.0))
    for (g, r), p in zip(heads, probs):
        ocmp_ref[rows(g, r), :] = jnp.dot(vct_ref[g], p.astype(BF16), preferred_element_type=F32)

    p_sum = jnp.concatenate(
        [_tree_sum(probs[g * HEADS_PER_GROUP:(g + 1) * HEADS_PER_GROUP]) for g in range(N_KV_GROUPS)], axis=1)
    hi = p_sum.astype(BF16)
    rem = p_sum - hi.astype(F32)
    mid = rem.astype(BF16)
    lo = (rem - mid.astype(F32)).astype(BF16)
    ov = ov_ref[...]
    imp = (jnp.dot(ov, hi, preferred_element_type=F32) + jnp.dot(ov, mid, preferred_element_type=F32)
           + jnp.dot(ov, lo, preferred_element_type=F32))
    cols = N_KV_GROUPS * ATT_Q
    j = lax.broadcasted_iota(jnp.int32, (N_SEL_BLOCKS, cols), 0)
    q_in = lax.broadcasted_iota(jnp.int32, (N_SEL_BLOCKS, cols), 1) & (ATT_Q - 1)
    q_blk = (q0 + q_in) // SEL_BLOCK
    forced = (j == 0) | (j == q_blk) | (j == q_blk - 1)
    imp = jnp.where(forced, jnp.inf, jnp.where(j > q_blk, -jnp.inf, imp))
    sub = 8
    row_in = lax.broadcasted_iota(jnp.int32, (sub, cols), 0)
    bias = []
    for b0 in range(0, N_SEL_BLOCKS, sub):
        mine = imp[b0:b0 + sub, :]
        beats = []
        for jp in range(N_SEL_BLOCKS):
            other = imp[jp:jp + 1, :]
            ge = jnp.where(other >= mine, 1.0, 0.0)
            gt = jnp.where(other > mine, 1.0, 0.0)
            if jp < b0:
                beats.append(ge)
            elif jp >= b0 + sub:
                beats.append(gt)
            else:
                beats.append(jnp.where(row_in > jp - b0, ge, gt))
        bias.append(jnp.where(_tree_sum(beats) < N_SELECT, 0.0, NEG_INF))
    bias = jnp.concatenate(bias, axis=0).astype(BF16)
    for g in range(N_KV_GROUPS):
        bias_ref[g] = bias[:, g * ATT_Q:(g + 1) * ATT_Q]


def _select(qn, kc, vct):
    bsz, _, seq = qn.shape
    assert seq // SEL_BLOCK == N_SEL_BLOCKS
    grp = lambda *blk: pl.BlockSpec((None, N_KV_GROUPS) + blk, lambda bi, qi: (bi, 0, 0, 0))
    q_spec = pl.BlockSpec((None, Q_WIDTH, ATT_Q), lambda bi, qi: (bi, 0, qi))
    return pl.pallas_call(
        _select_body,
        grid=(bsz, seq // ATT_Q),
        in_specs=[q_spec, grp(N_CMP_PAD, HEAD_DIM), grp(HEAD_DIM, N_CMP_PAD),
                  _const_spec((N_SEL_BLOCKS, N_CMP_PAD))],
        out_specs=[pl.BlockSpec((None, N_KV_GROUPS, N_SEL_BLOCKS, ATT_Q), lambda bi, qi: (bi, 0, 0, qi)),
                   q_spec],
        out_shape=[jax.ShapeDtypeStruct((bsz, N_KV_GROUPS, N_SEL_BLOCKS, seq), BF16),
                   jax.ShapeDtypeStruct((bsz, Q_WIDTH, seq), F32)],
        compiler_params=pltpu.CompilerParams(dimension_semantics=("parallel", "parallel")),
        name="nsa_select",
    )(qn, kc, vct, _overlap(seq))


def _window_branch(q0, prev0, vwt_ref, sw_ref, tw_ref, pw_ref, accw_ref):
    m_all = jnp.maximum(tw_ref[0], tw_ref[1])
    for cs in _HEAD_COLS:
        m_w = m_all[:, cs]
        pw_ref[0, :, cs] = jnp.exp2((sw_ref[0, :, cs] - m_w).astype(BF16))
        pw_ref[1, :, cs] = jnp.exp2((sw_ref[1, :, cs] - m_w).astype(BF16))
    vw_diag = vwt_ref[:, pl.ds(q0, ATT_K)]
    vw_prev = vwt_ref[:, pl.ds(prev0, ATT_K)]
    for cs in _HEAD_COLS:
        accw_ref[:, cs] = (jnp.dot(vw_diag, pw_ref[0, :, cs], preferred_element_type=F32)
                           + jnp.dot(vw_prev, pw_ref[1, :, cs], preferred_element_type=F32))


def _attn_body(qn_ref, qr_ref, bias_ref, ocmp_ref, ks_ref, kw_ref, vst_ref, vwt_ref, gate_ref,
               o_ref, lhs_ref, s_ref, sw_ref, p_ref, pw_ref, t_ref, tw_ref, a_ref, m_ref, acc_ref, accw_ref):
    qi = pl.program_id(2)
    q0 = pl.multiple_of(qi * ATT_Q, ATT_Q)
    prev0 = pl.multiple_of(jnp.maximum(qi - 1, 0) * ATT_K, ATT_K)
    groups = range(GROUPS_PER_STEP)
    gh = HEADS_PER_GROUP * HEAD_DIM

    pad = jnp.zeros((LANES - HEAD_DIM - N_SEL_BLOCKS, ATT_Q), BF16)
    for g in groups:
        for r in range(HEADS_PER_GROUP):
            h = g * HEADS_PER_GROUP + r
            lhs_ref[g, :, r * ATT_Q:(r + 1) * ATT_Q] = jnp.concatenate(
                [qr_ref[h * ROPE_DIM:(h + 1) * ROPE_DIM, :],
                 qn_ref[h * HEAD_DIM + ROPE_DIM:(h + 1) * HEAD_DIM, :], bias_ref[g], pad], axis=0)


    key_in = lax.broadcasted_iota(jnp.int32, (ATT_K, ATT_Q), 0)
    col_in = lax.broadcasted_iota(jnp.int32, (ATT_K, ATT_Q), 1)
    causal = key_in <= col_in
    prev = (key_in > col_in) & (qi > 0)

    def window_scores(g):
        _score_stage(kw_ref.at[g], q0, lhs_ref.at[g], sw_ref.at[g, 0], tw_ref.at[g, 0], mask=causal)
        _score_stage(kw_ref.at[g], prev0, lhs_ref.at[g], sw_ref.at[g, 1], tw_ref.at[g, 1], mask=prev)

    def window_rest(g):
        _window_branch(q0, prev0, vwt_ref.at[g], sw_ref.at[g], tw_ref.at[g], pw_ref.at[g], accw_ref.at[g])

    for g in groups:
        m_ref[g] = jnp.full(m_ref.shape[1:], NEG_INF, F32)
        acc_ref[g] = jnp.zeros(acc_ref.shape[1:], F32)
    first_mask = causal | (qi > 0)
    for g in groups:
        window_scores(g)
        if g > 0:
            window_rest(g - 1)
        _score_stage(ks_ref.at[g], 0, lhs_ref.at[g], s_ref.at[g, 0], t_ref.at[g, 0], mask=first_mask)
    window_rest(groups[-1])

    def finish(g, k0, slot):
        _softmax_stage(s_ref.at[g, slot], t_ref.at[g, slot], m_ref.at[g], p_ref.at[g, slot], a_ref.at[g, slot])
        _value_stage(vst_ref.at[g], k0, p_ref.at[g, slot], a_ref.at[g, slot], acc_ref.at[g])

    def trip(j, cur, last):
        for g in groups:
            _score_stage(ks_ref.at[g], pl.multiple_of((j + 1) * ATT_K, ATT_K), lhs_ref.at[g],
                         s_ref.at[g, 1 - cur], t_ref.at[g, 1 - cur], mask=causal if last else None)
            finish(g, pl.multiple_of(j * ATT_K, ATT_K), cur)
        if last:
            for g in groups:
                finish(g, q0, 1 - cur)

    def trip_any(j, carry):
        pl.when(j % 2 == 0)(lambda: trip(j, 0, False))
        pl.when(j % 2 == 1)(lambda: trip(j, 1, False))
        return carry

    lax.fori_loop(0, qi - 1, trip_any, 0)
    pl.when((qi > 0) & (qi % 2 == 1))(lambda: trip(qi - 1, 0, True))
    pl.when((qi > 0) & (qi % 2 == 0))(lambda: trip(qi - 1, 1, True))

    @pl.when(qi == 0)
    def _():
        for g in groups:
            finish(g, q0, 0)

    for g in groups:
        gate = gate_ref[g]
        heads = []
        for r, cs in enumerate(_HEAD_COLS):
            g_cmp = gate[r:r + 1, :]
            g_sel = gate[HEADS_PER_GROUP + r:HEADS_PER_GROUP + r + 1, :]
            g_win = gate[2 * HEADS_PER_GROUP + r:2 * HEADS_PER_GROUP + r + 1, :]
            c_sel = g_sel / acc_ref[g, HEAD_DIM:HEAD_DIM + 1, cs]
            c_win = g_win / accw_ref[g, HEAD_DIM:HEAD_DIM + 1, cs]
            h = g * HEADS_PER_GROUP + r
            o_cmp = ocmp_ref[h * HEAD_DIM:(h + 1) * HEAD_DIM, :]
            heads.append(g_cmp * o_cmp + c_sel * acc_ref[g, :HEAD_DIM, cs] + c_win * accw_ref[g, :HEAD_DIM, cs])
        o_ref[:, g * gh:(g + 1) * gh] = jnp.concatenate(heads, axis=0).T.astype(BF16)


def _overlap(seq):
    ci = jnp.arange(N_CMP_PAD) * CMP_STRIDE
    sj = jnp.arange(seq // SEL_BLOCK) * SEL_BLOCK
    n_cmp = (seq - CMP_BLOCK) // CMP_STRIDE + 1
    ov = ((ci[None, :] < sj[:, None] + SEL_BLOCK) & (ci[None, :] + CMP_BLOCK > sj[:, None])
          & (jnp.arange(N_CMP_PAD)[None, :] < n_cmp))
    return ov.astype(BF16)


def _attention(qn, qr, bias, ocmp, ks, kw, vst, vwt, gate):
    bsz, _, seq = qn.shape
    assert ATT_Q == WINDOW == ATT_K
    gps = GROUPS_PER_STEP
    gh = gps * HEADS_PER_GROUP * HEAD_DIM
    grp = lambda *blk: pl.BlockSpec((None, gps) + blk, lambda bi, gi, qi: (bi, gi, 0, 0))
    q_spec = pl.BlockSpec((None, gh, ATT_Q), lambda bi, gi, qi: (bi, gi, qi))
    return pl.pallas_call(
        _attn_body,
        grid=(bsz, N_KV_GROUPS // gps, seq // ATT_Q),
        in_specs=[q_spec,
                  pl.BlockSpec((None, gps * HEADS_PER_GROUP * ROPE_DIM, ATT_Q), lambda bi, gi, qi: (bi, gi, qi)),
                  pl.BlockSpec((None, gps, N_SEL_BLOCKS, ATT_Q), lambda bi, gi, qi: (bi, gi, 0, qi)),
                  q_spec,
                  grp(seq, LANES), grp(seq, LANES), grp(V_ROWS, seq), grp(V_ROWS, seq),
                  pl.BlockSpec((None, gps, GATE_ROWS, ATT_Q), lambda bi, gi, qi: (bi, gi, 0, qi))],
        out_specs=pl.BlockSpec((None, ATT_Q, gh), lambda bi, gi, qi: (bi, qi, gi)),
        out_shape=jax.ShapeDtypeStruct((bsz, seq, Q_WIDTH), BF16),
        scratch_shapes=[pltpu.VMEM((gps, LANES, ATT_COLS), BF16),
                        pltpu.VMEM((gps, 2, ATT_K, ATT_COLS), F32),
                        pltpu.VMEM((gps, 2, ATT_K, ATT_COLS), F32),
                        pltpu.VMEM((gps, 2, ATT_K, ATT_COLS), BF16),
                        pltpu.VMEM((gps, 2, ATT_K, ATT_COLS), BF16),
                        pltpu.VMEM((gps, 2, 1, ATT_COLS), F32),
                        pltpu.VMEM((gps, 2, 1, ATT_COLS), F32),
                        pltpu.VMEM((gps, 2, 1, ATT_COLS), F32),
                        pltpu.VMEM((gps, 1, ATT_COLS), F32),
                        pltpu.VMEM((gps, V_ROWS, ATT_COLS), F32),
                        pltpu.VMEM((gps, V_ROWS, ATT_COLS), F32)],
        compiler_params=pltpu.CompilerParams(
            dimension_semantics=("parallel", "parallel", "arbitrary"), vmem_limit_bytes=VMEM_LIMIT),
        name="nsa_attention",
    )(qn, qr, bias, ocmp, ks, kw, vst, vwt, gate)


def _oproj_body(o_ref_in, x_ref, w_ref, g_ref, out_ref):
    m = jnp.dot(o_ref_in[...], w_ref[...], preferred_element_type=F32)
    out_ref[...] = x_ref[...] + _rms(m, g_ref[...])


def _out_proj(o2d, x2d, w_o, g3):
    t = x2d.shape[0]
    row = pl.BlockSpec((FFN_TOKENS, D_MODEL), lambda i: (i, 0))
    return pl.pallas_call(
        _oproj_body,
        grid=(t // FFN_TOKENS,),
        in_specs=[row, row, _const_spec((Q_WIDTH, D_MODEL)), _const_spec((1, D_MODEL))],
        out_specs=row,
        out_shape=jax.ShapeDtypeStruct((t, D_MODEL), F32),
        compiler_params=pltpu.CompilerParams(dimension_semantics=("parallel",)),
        name="nsa_out_proj",
    )(o2d, x2d, w_o.astype(BF16), g3.reshape(1, -1))


def _nsa_block(x, g2, g3, w_in, pos_k, pos_v, wk1, wk2, wv1, wv2, w_o):
    bsz, seq, d = x.shape
    w_t, w_n = _nsa_proj_weights(w_in)
    qn, qr, vst, vwt, gate, ks, kw, kvc = _nsa_proj(x, g2, w_t, w_n, _rope_tables(seq))
    kc, vct = _compress(kvc, pos_k, pos_v, wk1, wk2, wv1, wv2)
    bias, ocmp = _select(qn, kc, vct)
    o = _attention(qn, qr, bias, ocmp, ks, kw, vst, vwt, gate)
    return _out_proj(o.reshape(bsz * seq, Q_WIDTH), x.reshape(bsz * seq, d), w_o, g3).reshape(x.shape)


def kernel(x, norm_gains, ffn_w_gate, ffn_w_up, ffn_w_down, pool_w, pool_b, pool_scale, nsa_w_in, nsa_cmp_pos_k, nsa_cmp_pos_v, nsa_cmp_wk1, nsa_cmp_wk2, nsa_cmp_wv1, nsa_cmp_wv2, nsa_w_o):
    bsz, seq, d = x.shape
    depth = norm_gains.shape[0]

    def ffn(x, i, half):
        g = norm_gains[i]
        y = _ffn_block(x.reshape(bsz * seq, d), g[4 * half], g[4 * half + 1],
                       ffn_w_gate[i, half], ffn_w_up[i, half], ffn_w_down[i, half])
        return y.reshape(bsz, seq, d)

    for i in range(depth):
        g = norm_gains[i]
        x = ffn(x, i, 0)
        j = i // 2
        if i % 2 == 0:
            x = _pool_block(x, g[2], g[3], pool_w[j], pool_b[j], pool_scale[j])
        else:
            x = _nsa_block(x, g[2], g[3], nsa_w_in[j], nsa_cmp_pos_k[j], nsa_cmp_pos_v[j],
                           nsa_cmp_wk1[j], nsa_cmp_wk2[j], nsa_cmp_wv1[j], nsa_cmp_wv2[j], nsa_w_o[j])
        x = ffn(x, i, 1)
    return x
```

```python
import jax
import jax.numpy as jnp
from jax import lax
from jax.experimental import pallas as pl
from jax.experimental.pallas import tpu as pltpu

F32 = jnp.float32
BF16 = jnp.bfloat16

D_MODEL = 1024
EPS = 1e-6
D_FF = 2816
POOL_WINDOWS = (2, 4, 8, 16)
POOL_GROUP = D_MODEL // len(POOL_WINDOWS)
MAX_POOL_WINDOW = max(POOL_WINDOWS)
N_HEADS = 16
HEAD_DIM = 64
N_KV_GROUPS = 4
HEADS_PER_GROUP = N_HEADS // N_KV_GROUPS
ROPE_DIM = HEAD_DIM // 4
ROPE_THETA = 500000.0
CMP_BLOCK = 32
CMP_STRIDE = 16
CMP_HIDDEN = 256
SEL_BLOCK = 64
N_SELECT = 8
WINDOW = 256
Q_WIDTH = N_HEADS * HEAD_DIM
KV_WIDTH = N_KV_GROUPS * HEAD_DIM
NEG_INF = -1e30

LANES = 128
F32_SUBLANES = 8
BF16_SUBLANES = 16
VMEM_LIMIT = 56 * 1024 * 1024

FFN_TOKENS = 512
SEQ_TILE = 512
ATT_Q = 256
ATT_K = 256
N_CMP_PAD = 128
GROUPS_PER_STEP = 4

N_SEL_BLOCKS = 32
GATE_ROWS = 2 * F32_SUBLANES
ROPE_HALF = ROPE_DIM // 2
V_ROWS = HEAD_DIM + BF16_SUBLANES
LOG2_E = 1.4426950408889634
ATT_COLS = HEADS_PER_GROUP * ATT_Q

_ROW_VS = Q_WIDTH
_ROW_VW = _ROW_VS + KV_WIDTH
_ROW_GATE = _ROW_VW + KV_WIDTH
_PROJ_ROWS = _ROW_GATE + N_KV_GROUPS * GATE_ROWS
_G_COLS = N_KV_GROUPS * LANES
_OFF_KW = _G_COLS
_OFF_KVC = 2 * _G_COLS
_PROJ_COLS = 3 * _G_COLS

_NT = (((1,), (1,)), ((), ()))


def _rms(x, g):
    return x * lax.rsqrt(jnp.mean(x * x, axis=-1, keepdims=True) + EPS) * g


def _const_spec(shape):
    nd = len(shape)
    return pl.BlockSpec(shape, lambda *_: (0,) * nd, pipeline_mode=pl.Buffered(1))


def _ffn_body(x_ref, gpre_ref, gpost_ref, wg_ref, wu_ref, wd_ref, o_ref):
    x = x_ref[...]
    xb = _rms(x, gpre_ref[...]).astype(BF16)
    hg = jnp.dot(xb, wg_ref[...], preferred_element_type=F32)
    hu = jnp.dot(xb, wu_ref[...], preferred_element_type=F32)
    act = (hg * jax.nn.sigmoid(hg) * hu).astype(BF16)
    f = jnp.dot(act, wd_ref[...], preferred_element_type=F32)
    o_ref[...] = x + _rms(f, gpost_ref[...])


def _ffn_block(x2d, g_pre, g_post, wg, wu, wd):
    t = x2d.shape[0]
    row = pl.BlockSpec((FFN_TOKENS, D_MODEL), lambda i: (i, 0))
    return pl.pallas_call(
        _ffn_body,
        grid=(t // FFN_TOKENS,),
        in_specs=[row, _const_spec((1, D_MODEL)), _const_spec((1, D_MODEL)),
                  _const_spec((D_MODEL, D_FF)), _const_spec((D_MODEL, D_FF)),
                  _const_spec((D_FF, D_MODEL))],
        out_specs=row,
        out_shape=jax.ShapeDtypeStruct((t, D_MODEL), F32),
        compiler_params=pltpu.CompilerParams(
            dimension_semantics=("parallel",), vmem_limit_bytes=VMEM_LIMIT),
        name="ffn_block",
    )(x2d, g_pre.reshape(1, -1), (0.5 * g_post).reshape(1, -1),
      wg.astype(BF16), wu.astype(BF16), wd.astype(BF16))


def _pool_body(x_ref, g2_ref, g3_ref, w_ref, b_ref, sc_ref, o_ref, carry_ref):
    si = pl.program_id(1)

    @pl.when(si == 0)
    def _():
        carry_ref[...] = jnp.zeros_like(carry_ref)

    x = x_ref[...]
    h = _rms(x, g2_ref[...])
    hp = jnp.concatenate([carry_ref[...], h], axis=0)
    carry_ref[...] = h[SEQ_TILE - MAX_POOL_WINDOW:, :]
    pos = si * SEQ_TILE + lax.broadcasted_iota(jnp.int32, (SEQ_TILE, 1), 0)
    ys = []
    for g, w in enumerate(POOL_WINDOWS):
        cols = slice(g * POOL_GROUP, (g + 1) * POOL_GROUP)
        acc = hp[:, cols]
        k = 1
        while k < w:
            acc = acc + pltpu.roll(acc, k, axis=0)
            k *= 2
        cnt = jnp.minimum(pos + 1, w).astype(F32)
        d = acc[MAX_POOL_WINDOW:, :] / cnt - h[:, cols]
        y = jnp.dot(d.astype(BF16), w_ref[g], preferred_element_type=F32) + b_ref[g:g + 1, :]
        ys.append(y)
    m = jnp.concatenate(ys, axis=1) * sc_ref[...]
    o_ref[...] = x + _rms(m, g3_ref[...])


def _pool_block(x, g2, g3, w, b, scale):
    bsz, seq, _ = x.shape
    row = pl.BlockSpec((None, SEQ_TILE, D_MODEL), lambda bi, si: (bi, si, 0))
    return pl.pallas_call(
        _pool_body,
        grid=(bsz, seq // SEQ_TILE),
        in_specs=[row, _const_spec((1, D_MODEL)), _const_spec((1, D_MODEL)),
                  _const_spec((len(POOL_WINDOWS), POOL_GROUP, POOL_GROUP)),
                  _const_spec((len(POOL_WINDOWS), POOL_GROUP)), _const_spec((1, D_MODEL))],
        out_specs=row,
        out_shape=jax.ShapeDtypeStruct(x.shape, F32),
        scratch_shapes=[pltpu.VMEM((MAX_POOL_WINDOW, D_MODEL), F32)],
        compiler_params=pltpu.CompilerParams(
            dimension_semantics=("parallel", "arbitrary"), vmem_limit_bytes=VMEM_LIMIT),
        name="pool_block",
    )(x, g2.reshape(1, -1), g3.reshape(1, -1), w.astype(BF16), b, scale.reshape(1, -1))


def _proj_body(x_ref, g2_ref, wt_ref, wn_ref, c_ref, sa_ref, sb_ref, ct_ref, st_ref,
               qn_ref, qr_ref, vs_ref, vw_ref, gate_ref, ks_ref, kw_ref, kvc_ref):
    si = pl.program_id(1)
    hb = _rms(x_ref[...], g2_ref[...]).astype(BF16)

    pt = lax.dot_general(wt_ref[...], hb, _NT, preferred_element_type=F32)
    qn_ref[...] = pt[:Q_WIDTH].astype(BF16)
    cos_t, sin_t = ct_ref[...], st_ref[...]
    for h in range(N_HEADS):
        x1 = pt[h * HEAD_DIM:h * HEAD_DIM + ROPE_HALF]
        x2 = pt[h * HEAD_DIM + ROPE_HALF:h * HEAD_DIM + ROPE_DIM]
        rot = jnp.concatenate([x1 * cos_t - x2 * sin_t, x1 * sin_t + x2 * cos_t], axis=0)
        qr_ref[h * ROPE_DIM:(h + 1) * ROPE_DIM, :] = rot.astype(BF16)
    ones_rows = jnp.ones((V_ROWS - HEAD_DIM, SEQ_TILE), F32)
    for g in range(N_KV_GROUPS):
        vs_ref[g] = jnp.concatenate(
            [pt[_ROW_VS + g * HEAD_DIM:_ROW_VS + (g + 1) * HEAD_DIM], ones_rows], axis=0).astype(BF16)
        vw_ref[g] = jnp.concatenate(
            [pt[_ROW_VW + g * HEAD_DIM:_ROW_VW + (g + 1) * HEAD_DIM], ones_rows], axis=0).astype(BF16)
        gate_ref[g] = jax.nn.sigmoid(pt[_ROW_GATE + g * GATE_ROWS:_ROW_GATE + (g + 1) * GATE_ROWS])

    pn = jnp.dot(hb, wn_ref[...], preferred_element_type=F32)
    cos, sin_a, sin_b = c_ref[...], sa_ref[...], sb_ref[...]

    def slab(off, i):
        return pn[:, off + i * LANES: off + (i + 1) * LANES]

    def rope(t):
        return (t * cos + pltpu.roll(t, LANES - ROPE_HALF, axis=1) * sin_a
                + pltpu.roll(t, ROPE_HALF, axis=1) * sin_b)

    pos = si * SEQ_TILE + lax.broadcasted_iota(jnp.int32, (SEQ_TILE, LANES), 0)
    lane = lax.broadcasted_iota(jnp.int32, (SEQ_TILE, LANES), 1)
    blk_onehot = jnp.where(lane - HEAD_DIM == pos // SEL_BLOCK, 1.0, 0.0)
    for g in range(N_KV_GROUPS):
        ks_ref[g] = (rope(slab(0, g)) + blk_onehot).astype(BF16)
        kw_ref[g] = rope(slab(_OFF_KW, g)).astype(BF16)
        kvc_ref[g] = slab(_OFF_KVC, g)


def _nsa_proj_weights(w_in):
    d = w_in.shape[0]
    wq = w_in[:, :Q_WIDTH] * (HEAD_DIM ** -0.5)
    kv = w_in[:, Q_WIDTH:Q_WIDTH + 6 * KV_WIDTH].reshape(d, 6, N_KV_GROUPS, HEAD_DIM)
    wkc, wvc, wks, wvs, wkw, wvw = [kv[:, i] for i in range(6)]
    wgate = w_in[:, Q_WIDTH + 6 * KV_WIDTH:].reshape(d, 3, N_KV_GROUPS, HEADS_PER_GROUP)
    wgate = wgate.transpose(0, 2, 1, 3).reshape(d, N_KV_GROUPS, 3 * HEADS_PER_GROUP)
    wgate = jnp.pad(wgate, ((0, 0), (0, 0), (0, GATE_ROWS - 3 * HEADS_PER_GROUP)))
    w_t = jnp.concatenate([wq, wvs.reshape(d, KV_WIDTH), wvw.reshape(d, KV_WIDTH),
                           wgate.reshape(d, N_KV_GROUPS * GATE_ROWS)], axis=1).T.astype(BF16)
    zero = jnp.zeros_like(wkc)

    def pair(a, b):
        return jnp.concatenate([a, b], axis=-1).reshape(d, _G_COLS)

    w_n = jnp.concatenate([pair(wks * LOG2_E, zero), pair(wkw * LOG2_E, zero), pair(wkc, wvc)],
                          axis=1).astype(BF16)
    return w_t, w_n


def _rope_tables(seq):
    inv = 1.0 / (ROPE_THETA ** (jnp.arange(0, ROPE_DIM, 2, dtype=F32) / ROPE_DIM))
    ang = jnp.arange(seq, dtype=F32)[:, None] * inv[None, :]
    cos, sin = jnp.cos(ang), jnp.sin(ang)
    pad = LANES - ROPE_DIM
    c = jnp.concatenate([cos, cos, jnp.ones((seq, pad), F32)], axis=1)
    sa = jnp.concatenate([-sin, jnp.zeros((seq, ROPE_HALF + pad), F32)], axis=1)
    sb = jnp.concatenate([jnp.zeros((seq, ROPE_HALF), F32), sin, jnp.zeros((seq, pad), F32)], axis=1)
    return c, sa, sb, cos.T, sin.T


def _nsa_proj(x, g2, w_t, w_n, tables):
    bsz, seq, _ = x.shape
    feat = lambda rows, dt: jax.ShapeDtypeStruct((bsz, N_KV_GROUPS, rows, seq), dt)
    feat_spec = lambda rows: pl.BlockSpec((None, N_KV_GROUPS, rows, SEQ_TILE), lambda bi, si: (bi, 0, 0, si))
    tok = lambda dt: jax.ShapeDtypeStruct((bsz, N_KV_GROUPS, seq, LANES), dt)
    tok_spec = pl.BlockSpec((None, N_KV_GROUPS, SEQ_TILE, LANES), lambda bi, si: (bi, 0, si, 0))
    tab_spec = pl.BlockSpec((SEQ_TILE, LANES), lambda bi, si: (si, 0))
    tab_t_spec = pl.BlockSpec((ROPE_HALF, SEQ_TILE), lambda bi, si: (0, si))
    n_rot = N_HEADS * ROPE_DIM
    return pl.pallas_call(
        _proj_body,
        grid=(bsz, seq // SEQ_TILE),
        in_specs=[pl.BlockSpec((None, SEQ_TILE, D_MODEL), lambda bi, si: (bi, si, 0)),
                  _const_spec((1, D_MODEL)), _const_spec((_PROJ_ROWS, D_MODEL)),
                  _const_spec((D_MODEL, _PROJ_COLS)),
                  tab_spec, tab_spec, tab_spec, tab_t_spec, tab_t_spec],
        out_specs=[pl.BlockSpec((None, Q_WIDTH, SEQ_TILE), lambda bi, si: (bi, 0, si)),
                   pl.BlockSpec((None, n_rot, SEQ_TILE), lambda bi, si: (bi, 0, si)),
                   feat_spec(V_ROWS), feat_spec(V_ROWS), feat_spec(GATE_ROWS),
                   tok_spec, tok_spec, tok_spec],
        out_shape=[jax.ShapeDtypeStruct((bsz, Q_WIDTH, seq), BF16),
                   jax.ShapeDtypeStruct((bsz, n_rot, seq), BF16),
                   feat(V_ROWS, BF16), feat(V_ROWS, BF16), feat(GATE_ROWS, F32),
                   tok(BF16), tok(BF16), tok(F32)],
        compiler_params=pltpu.CompilerParams(
            dimension_semantics=("parallel", "parallel"), vmem_limit_bytes=VMEM_LIMIT),
        name="nsa_proj",
    )(x, g2.reshape(1, -1), w_t, w_n, *tables)


def _cmp_body(kvc_ref, pos_ref, w1_ref, w2k_ref, w2vt_ref, kc_ref, vct_ref):
    za, zb = [], []
    for t in range(CMP_STRIDE):
        xt = kvc_ref[pl.ds(t, N_CMP_PAD, stride=CMP_STRIDE), :]
        za.append((xt + pos_ref[t:t + 1, :]).astype(BF16))
        zb.append((xt + pos_ref[CMP_STRIDE + t:CMP_STRIDE + t + 1, :]).astype(BF16))
    first = jnp.dot(jnp.concatenate(za, axis=1), w1_ref[0], preferred_element_type=F32)
    second = jnp.dot(jnp.concatenate(zb, axis=1), w1_ref[1], preferred_element_type=F32)
    row = lax.broadcasted_iota(jnp.int32, second.shape, 0)
    second = jnp.where(row < N_CMP_PAD - 1, pltpu.roll(second, N_CMP_PAD - 1, axis=0), 0.0)
    pre = first + second
    hid = (pre * jax.nn.sigmoid(pre)).astype(BF16)
    kc_ref[...] = jnp.dot(hid[:, :CMP_HIDDEN], w2k_ref[...], preferred_element_type=F32).astype(BF16)
    vct_ref[...] = lax.dot_general(w2vt_ref[...], hid[:, CMP_HIDDEN:], _NT,
                                   preferred_element_type=F32).astype(BF16)


def _compress(kvc, pos_k, pos_v, wk1, wk2, wv1, wv2):
    bsz, _, seq, _ = kvc.shape
    assert seq == N_CMP_PAD * CMP_STRIDE
    pos = jnp.concatenate([pos_k, pos_v], axis=1)
    k1 = wk1.reshape(CMP_BLOCK, HEAD_DIM, CMP_HIDDEN)
    v1 = wv1.reshape(CMP_BLOCK, HEAD_DIM, CMP_HIDDEN)
    z1 = jnp.zeros_like(k1)
    w1 = jnp.concatenate([jnp.concatenate([k1, z1], axis=2),
                          jnp.concatenate([z1, v1], axis=2)], axis=1).astype(BF16)
    w1 = w1.reshape(2, CMP_STRIDE * LANES, 2 * CMP_HIDDEN)
    w2k = (wk2 * LOG2_E).astype(BF16)
    w2vt = wv2.T.astype(BF16)
    return pl.pallas_call(
        _cmp_body,
        grid=(bsz, N_KV_GROUPS),
        in_specs=[pl.BlockSpec((None, None, seq, LANES), lambda bi, gi: (bi, gi, 0, 0)),
                  _const_spec(pos.shape), _const_spec(w1.shape),
                  _const_spec(w2k.shape), _const_spec(w2vt.shape)],
        out_specs=[pl.BlockSpec((None, None, N_CMP_PAD, HEAD_DIM), lambda bi, gi: (bi, gi, 0, 0)),
                   pl.BlockSpec((None, None, HEAD_DIM, N_CMP_PAD), lambda bi, gi: (bi, gi, 0, 0))],
        out_shape=[jax.ShapeDtypeStruct((bsz, N_KV_GROUPS, N_CMP_PAD, HEAD_DIM), BF16),
                   jax.ShapeDtypeStruct((bsz, N_KV_GROUPS, HEAD_DIM, N_CMP_PAD), BF16)],
        compiler_params=pltpu.CompilerParams(dimension_semantics=("parallel", "parallel")),
        name="nsa_compress",
    )(kvc, pos, w1, w2k, w2vt)


_HEAD_COLS = [slice(r * ATT_Q, (r + 1) * ATT_Q) for r in range(HEADS_PER_GROUP)]


def _score_stage(k_ref, k0, lhs_ref, s_out, t_out, mask=None):
    k_tile = k_ref[pl.ds(k0, ATT_K), :]
    t_max = []
    for cs in _HEAD_COLS:
        s = jnp.dot(k_tile, lhs_ref[:, cs], preferred_element_type=F32)
        if mask is not None:
            s = jnp.where(mask, s, NEG_INF)
        s_out[:, cs] = s
        t_max.append(jnp.max(s, axis=0, keepdims=True))
    t_out[...] = jnp.concatenate(t_max, axis=1)


def _softmax_stage(s_in, t_in, m_ref, p_out, a_out):
    m_all = m_ref[...]
    t_all = t_in[...]
    m_new, alpha = [], []
    for cs in _HEAD_COLS:
        s = s_in[:, cs]
        m_c = jnp.maximum(m_all[:, cs], t_all[:, cs])
        p_out[:, cs] = jnp.exp2((s - m_c).astype(BF16))
        alpha.append(jnp.exp2(m_all[:, cs] - m_c))
        m_new.append(m_c)
    m_ref[...] = jnp.concatenate(m_new, axis=1)
    a_out[...] = jnp.concatenate(alpha, axis=1)


def _value_stage(vt_ref, k0, p_in, a_in, acc_ref):
    vt = vt_ref[:, pl.ds(k0, ATT_K)]
    acc_all, a_all = acc_ref[...], a_in[...]
    acc_ref[...] = jnp.concatenate(
        [a_all[:, cs] * acc_all[:, cs] + jnp.dot(vt, p_in[:, cs], preferred_element_type=F32)
         for cs in _HEAD_COLS], axis=1)


def _tree_sum(terms):
    while len(terms) > 1:
        terms = [a + b for a, b in zip(terms[::2], terms[1::2])] + terms[len(terms) & ~1:]
    return terms[0]


def _select_body(qn_ref, kc_ref, vct_ref, ov_ref, bias_ref, ocmp_ref):
    q0 = pl.program_id(1) * ATT_Q
    heads = [(g, r) for g in range(N_KV_GROUPS) for r in range(HEADS_PER_GROUP)]
    rows = lambda g, r: slice((g * HEADS_PER_GROUP + r) * HEAD_DIM, (g * HEADS_PER_GROUP + r + 1) * HEAD_DIM)

    blk_end = lax.broadcasted_iota(jnp.int32, (N_CMP_PAD, ATT_Q), 0) * CMP_STRIDE + CMP_BLOCK - 1
    qpos = q0 + lax.broadcasted_iota(jnp.int32, (N_CMP_PAD, ATT_Q), 1)
    cmp_mask = blk_end <= qpos
    any_cmp = q0 + lax.broadcasted_iota(jnp.int32, (1, ATT_Q), 1) >= CMP_BLOCK - 1
    scores = [jnp.dot(kc_ref[g], qn_ref[rows(g, r), :], preferred_element_type=F32) for g, r in heads]
    probs = []
    for s in scores:
        s = jnp.where(cmp_mask, s, NEG_INF)
        e = jnp.exp2(s - jnp.max(s, axis=0, keepdims=True))
        probs.append(e * jnp.where(any_cmp, 1.0 / jnp.sum(e, axis=0, keepdims=True), 0.0))
    for (g, r), p in zip(heads, probs):
        ocmp_ref[rows(g, r), :] = jnp.dot(vct_ref[g], p.astype(BF16), preferred_element_type=F32)

    p_sum = jnp.concatenate(
        [_tree_sum(probs[g * HEADS_PER_GROUP:(g + 1) * HEADS_PER_GROUP]) for g in range(N_KV_GROUPS)], axis=1)
    hi = p_sum.astype(BF16)
    rem = p_sum - hi.astype(F32)
    mid = rem.astype(BF16)
    lo = (rem - mid.astype(F32)).astype(BF16)
    ov = ov_ref[...]
    imp = (jnp.dot(ov, hi, preferred_element_type=F32) + jnp.dot(ov, mid, preferred_element_type=F32)
           + jnp.dot(ov, lo, preferred_element_type=F32))
    cols = N_KV_GROUPS * ATT_Q
    j = lax.broadcasted_iota(jnp.int32, (N_SEL_BLOCKS, cols), 0)
    q_in = lax.broadcasted_iota(jnp.int32, (N_SEL_BLOCKS, cols), 1) & (ATT_Q - 1)
    q_blk = (q0 + q_in) // SEL_BLOCK
    forced = (j == 0) | (j == q_blk) | (j == q_blk - 1)
    imp = jnp.where(forced, jnp.inf, jnp.where(j > q_blk, -jnp.inf, imp))
    sub = F32_SUBLANES
    row_in = lax.broadcasted_iota(jnp.int32, (sub, cols), 0)
    bias = []
    for b0 in range(0, N_SEL_BLOCKS, sub):
        mine = imp[b0:b0 + sub, :]
        beats = []
        for jp in range(N_SEL_BLOCKS):
            other = imp[jp:jp + 1, :]
            ge = jnp.where(other >= mine, 1.0, 0.0)
            gt = jnp.where(other > mine, 1.0, 0.0)
            if jp < b0:
                beats.append(ge)
            elif jp >= b0 + sub:
                beats.append(gt)
            else:
                beats.append(jnp.where(row_in > jp - b0, ge, gt))
        bias.append(jnp.where(_tree_sum(beats) < N_SELECT, 0.0, NEG_INF))
    bias = jnp.concatenate(bias, axis=0).astype(BF16)
    for g in range(N_KV_GROUPS):
        bias_ref[g] = bias[:, g * ATT_Q:(g + 1) * ATT_Q]


def _select(qn, kc, vct):
    bsz, _, seq = qn.shape
    assert seq // SEL_BLOCK == N_SEL_BLOCKS
    grp = lambda *blk: pl.BlockSpec((None, N_KV_GROUPS) + blk, lambda bi, qi: (bi, 0, 0, 0))
    q_spec = pl.BlockSpec((None, Q_WIDTH, ATT_Q), lambda bi, qi: (bi, 0, qi))
    return pl.pallas_call(
        _select_body,
        grid=(bsz, seq // ATT_Q),
        in_specs=[q_spec, grp(N_CMP_PAD, HEAD_DIM), grp(HEAD_DIM, N_CMP_PAD),
                  _const_spec((N_SEL_BLOCKS, N_CMP_PAD))],
        out_specs=[pl.BlockSpec((None, N_KV_GROUPS, N_SEL_BLOCKS, ATT_Q), lambda bi, qi: (bi, 0, 0, qi)),
                   q_spec],
        out_shape=[jax.ShapeDtypeStruct((bsz, N_KV_GROUPS, N_SEL_BLOCKS, seq), BF16),
                   jax.ShapeDtypeStruct((bsz, Q_WIDTH, seq), F32)],
        compiler_params=pltpu.CompilerParams(dimension_semantics=("parallel", "parallel")),
        name="nsa_select",
    )(qn, kc, vct, _overlap(seq))


def _window_branch(q0, prev0, vwt_ref, sw_ref, tw_ref, pw_ref, accw_ref):
    m_all = jnp.maximum(tw_ref[0], tw_ref[1])
    for cs in _HEAD_COLS:
        m_w = m_all[:, cs]
        pw_ref[0, :, cs] = jnp.exp2((sw_ref[0, :, cs] - m_w).astype(BF16))
        pw_ref[1, :, cs] = jnp.exp2((sw_ref[1, :, cs] - m_w).astype(BF16))
    vw_diag = vwt_ref[:, pl.ds(q0, ATT_K)]
    vw_prev = vwt_ref[:, pl.ds(prev0, ATT_K)]
    for cs in _HEAD_COLS:
        accw_ref[:, cs] = (jnp.dot(vw_diag, pw_ref[0, :, cs], preferred_element_type=F32)
                           + jnp.dot(vw_prev, pw_ref[1, :, cs], preferred_element_type=F32))


def _attn_body(qn_ref, qr_ref, bias_ref, ocmp_ref, ks_ref, kw_ref, vst_ref, vwt_ref, gate_ref,
               o_ref, lhs_ref, s_ref, sw_ref, p_ref, pw_ref, t_ref, tw_ref, a_ref, m_ref, acc_ref, accw_ref):
    qi = pl.program_id(2)
    q0 = pl.multiple_of(qi * ATT_Q, ATT_Q)
    prev0 = pl.multiple_of(jnp.maximum(qi - 1, 0) * ATT_K, ATT_K)
    groups = range(GROUPS_PER_STEP)
    gh = HEADS_PER_GROUP * HEAD_DIM

    pad = jnp.zeros((LANES - HEAD_DIM - N_SEL_BLOCKS, ATT_Q), BF16)
    for g in groups:
        for r in range(HEADS_PER_GROUP):
            h = g * HEADS_PER_GROUP + r
            lhs_ref[g, :, r * ATT_Q:(r + 1) * ATT_Q] = jnp.concatenate(
                [qr_ref[h * ROPE_DIM:(h + 1) * ROPE_DIM, :],
                 qn_ref[h * HEAD_DIM + ROPE_DIM:(h + 1) * HEAD_DIM, :], bias_ref[g], pad], axis=0)


    key_in = lax.broadcasted_iota(jnp.int32, (ATT_K, ATT_Q), 0)
    col_in = lax.broadcasted_iota(jnp.int32, (ATT_K, ATT_Q), 1)
    causal = key_in <= col_in
    prev = (key_in > col_in) & (qi > 0)

    def window_scores(g):
        _score_stage(kw_ref.at[g], q0, lhs_ref.at[g], sw_ref.at[g, 0], tw_ref.at[g, 0], mask=causal)
        _score_stage(kw_ref.at[g], prev0, lhs_ref.at[g], sw_ref.at[g, 1], tw_ref.at[g, 1], mask=prev)

    def window_rest(g):
        _window_branch(q0, prev0, vwt_ref.at[g], sw_ref.at[g], tw_ref.at[g], pw_ref.at[g], accw_ref.at[g])

    for g in groups:
        m_ref[g] = jnp.full(m_ref.shape[1:], NEG_INF, F32)
        acc_ref[g] = jnp.zeros(acc_ref.shape[1:], F32)
    first_mask = causal | (qi > 0)
    for g in groups:
        window_scores(g)
        if g > 0:
            window_rest(g - 1)
        _score_stage(ks_ref.at[g], 0, lhs_ref.at[g], s_ref.at[g, 0], t_ref.at[g, 0], mask=first_mask)
    window_rest(groups[-1])

    def finish(g, k0, slot):
        _softmax_stage(s_ref.at[g, slot], t_ref.at[g, slot], m_ref.at[g], p_ref.at[g, slot], a_ref.at[g, slot])
        _value_stage(vst_ref.at[g], k0, p_ref.at[g, slot], a_ref.at[g, slot], acc_ref.at[g])

    def trip(j, cur, last):
        for g in groups:
            _score_stage(ks_ref.at[g], pl.multiple_of((j + 1) * ATT_K, ATT_K), lhs_ref.at[g],
                         s_ref.at[g, 1 - cur], t_ref.at[g, 1 - cur], mask=causal if last else None)
            finish(g, pl.multiple_of(j * ATT_K, ATT_K), cur)
        if last:
            for g in groups:
                finish(g, q0, 1 - cur)

    def trip_any(j, carry):
        pl.when(j % 2 == 0)(lambda: trip(j, 0, False))
        pl.when(j % 2 == 1)(lambda: trip(j, 1, False))
        return carry

    lax.fori_loop(0, qi - 1, trip_any, 0)
    pl.when((qi > 0) & (qi % 2 == 1))(lambda: trip(qi - 1, 0, True))
    pl.when((qi > 0) & (qi % 2 == 0))(lambda: trip(qi - 1, 1, True))

    @pl.when(qi == 0)
    def _():
        for g in groups:
            finish(g, q0, 0)

    for g in groups:
        gate = gate_ref[g]
        heads = []
        for r, cs in enumerate(_HEAD_COLS):
            g_cmp = gate[r:r + 1, :]
            g_sel = gate[HEADS_PER_GROUP + r:HEADS_PER_GROUP + r + 1, :]
            g_win = gate[2 * HEADS_PER_GROUP + r:2 * HEADS_PER_GROUP + r + 1, :]
            c_sel = g_sel / acc_ref[g, HEAD_DIM:HEAD_DIM + 1, cs]
            c_win = g_win / accw_ref[g, HEAD_DIM:HEAD_DIM + 1, cs]
            h = g * HEADS_PER_GROUP + r
            o_cmp = ocmp_ref[h * HEAD_DIM:(h + 1) * HEAD_DIM, :]
            heads.append(g_cmp * o_cmp + c_sel * acc_ref[g, :HEAD_DIM, cs] + c_win * accw_ref[g, :HEAD_DIM, cs])
        o_ref[:, g * gh:(g + 1) * gh] = jnp.concatenate(heads, axis=0).T.astype(BF16)


def _overlap(seq):
    ci = jnp.arange(N_CMP_PAD) * CMP_STRIDE
    sj = jnp.arange(seq // SEL_BLOCK) * SEL_BLOCK
    n_cmp = (seq - CMP_BLOCK) // CMP_STRIDE + 1
    ov = ((ci[None, :] < sj[:, None] + SEL_BLOCK) & (ci[None, :] + CMP_BLOCK > sj[:, None])
          & (jnp.arange(N_CMP_PAD)[None, :] < n_cmp))
    return ov.astype(BF16)


def _attention(qn, qr, bias, ocmp, ks, kw, vst, vwt, gate):
    bsz, _, seq = qn.shape
    assert ATT_Q == WINDOW == ATT_K
    gps = GROUPS_PER_STEP
    gh = gps * HEADS_PER_GROUP * HEAD_DIM
    grp = lambda *blk: pl.BlockSpec((None, gps) + blk, lambda bi, gi, qi: (bi, gi, 0, 0))
    q_spec = pl.BlockSpec((None, gh, ATT_Q), lambda bi, gi, qi: (bi, gi, qi))
    return pl.pallas_call(
        _attn_body,
        grid=(bsz, N_KV_GROUPS // gps, seq // ATT_Q),
        in_specs=[q_spec,
                  pl.BlockSpec((None, gps * HEADS_PER_GROUP * ROPE_DIM, ATT_Q), lambda bi, gi, qi: (bi, gi, qi)),
                  pl.BlockSpec((None, gps, N_SEL_BLOCKS, ATT_Q), lambda bi, gi, qi: (bi, gi, 0, qi)),
                  q_spec,
                  grp(seq, LANES), grp(seq, LANES), grp(V_ROWS, seq), grp(V_ROWS, seq),
                  pl.BlockSpec((None, gps, GATE_ROWS, ATT_Q), lambda bi, gi, qi: (bi, gi, 0, qi))],
        out_specs=pl.BlockSpec((None, ATT_Q, gh), lambda bi, gi, qi: (bi, qi, gi)),
        out_shape=jax.ShapeDtypeStruct((bsz, seq, Q_WIDTH), BF16),
        scratch_shapes=[pltpu.VMEM((gps, LANES, ATT_COLS), BF16),
                        pltpu.VMEM((gps, 2, ATT_K, ATT_COLS), F32),
                        pltpu.VMEM((gps, 2, ATT_K, ATT_COLS), F32),
                        pltpu.VMEM((gps, 2, ATT_K, ATT_COLS), BF16),
                        pltpu.VMEM((gps, 2, ATT_K, ATT_COLS), BF16),
                        pltpu.VMEM((gps, 2, 1, ATT_COLS), F32),
                        pltpu.VMEM((gps, 2, 1, ATT_COLS), F32),
                        pltpu.VMEM((gps, 2, 1, ATT_COLS), F32),
                        pltpu.VMEM((gps, 1, ATT_COLS), F32),
                        pltpu.VMEM((gps, V_ROWS, ATT_COLS), F32),
                        pltpu.VMEM((gps, V_ROWS, ATT_COLS), F32)],
        compiler_params=pltpu.CompilerParams(
            dimension_semantics=("parallel", "parallel", "arbitrary"), vmem_limit_bytes=VMEM_LIMIT),
        name="nsa_attention",
    )(qn, qr, bias, ocmp, ks, kw, vst, vwt, gate)


def _oproj_body(o_ref_in, x_ref, w_ref, g_ref, out_ref):
    m = jnp.dot(o_ref_in[...], w_ref[...], preferred_element_type=F32)
    out_ref[...] = x_ref[...] + _rms(m, g_ref[...])


def _out_proj(o2d, x2d, w_o, g3):
    t = x2d.shape[0]
    row = pl.BlockSpec((FFN_TOKENS, D_MODEL), lambda i: (i, 0))
    return pl.pallas_call(
        _oproj_body,
        grid=(t // FFN_TOKENS,),
        in_specs=[row, row, _const_spec((Q_WIDTH, D_MODEL)), _const_spec((1, D_MODEL))],
        out_specs=row,
        out_shape=jax.ShapeDtypeStruct((t, D_MODEL), F32),
        compiler_params=pltpu.CompilerParams(dimension_semantics=("parallel",)),
        name="nsa_out_proj",
    )(o2d, x2d, w_o.astype(BF16), g3.reshape(1, -1))


def _nsa_block(x, g2, g3, w_in, pos_k, pos_v, wk1, wk2, wv1, wv2, w_o):
    bsz, seq, d = x.shape
    w_t, w_n = _nsa_proj_weights(w_in)
    qn, qr, vst, vwt, gate, ks, kw, kvc = _nsa_proj(x, g2, w_t, w_n, _rope_tables(seq))
    kc, vct = _compress(kvc, pos_k, pos_v, wk1, wk2, wv1, wv2)
    bias, ocmp = _select(qn, kc, vct)
    o = _attention(qn, qr, bias, ocmp, ks, kw, vst, vwt, gate)
    return _out_proj(o.reshape(bsz * seq, Q_WIDTH), x.reshape(bsz * seq, d), w_o, g3).reshape(x.shape)


def kernel(x, norm_gains, ffn_w_gate, ffn_w_up, ffn_w_down, pool_w, pool_b, pool_scale, nsa_w_in, nsa_cmp_pos_k, nsa_cmp_pos_v, nsa_cmp_wk1, nsa_cmp_wk2, nsa_cmp_wv1, nsa_cmp_wv2, nsa_w_o):
    bsz, seq, d = x.shape
    depth = norm_gains.shape[0]

    def ffn(x, i, half):
        g = norm_gains[i]
        y = _ffn_block(x.reshape(bsz * seq, d), g[4 * half], g[4 * half + 1],
                       ffn_w_gate[i, half], ffn_w_up[i, half], ffn_w_down[i, half])
        return y.reshape(bsz, seq, d)

    for i in range(depth):
        g = norm_gains[i]
        x = ffn(x, i, 0)
        j = i // 2
        if i % 2 == 0:
            x = _pool_block(x, g[2], g[3], pool_w[j], pool_b[j], pool_scale[j])
        else:
            x = _nsa_block(x, g[2], g[3], nsa_w_in[j], nsa_cmp_pos_k[j], nsa_cmp_pos_v[j],
                           nsa_cmp_wk1[j], nsa_cmp_wk2[j], nsa_cmp_wv1[j], nsa_cmp_wv2[j], nsa_w_o[j])
        x = ffn(x, i, 1)
    return x
```

```python
import jax
import jax.numpy as jnp
from jax import lax
from jax.experimental import pallas as pl
from jax.experimental.pallas import tpu as pltpu

F32 = jnp.float32
BF16 = jnp.bfloat16

D_MODEL = 1024
EPS = 1e-6
D_FF = 2816
POOL_WINDOWS = (2, 4, 8, 16)
POOL_GROUP = D_MODEL // len(POOL_WINDOWS)
MAX_POOL_WINDOW = max(POOL_WINDOWS)
N_HEADS = 16
HEAD_DIM = 64
N_KV_GROUPS = 4
HEADS_PER_GROUP = N_HEADS // N_KV_GROUPS
ROPE_DIM = HEAD_DIM // 4
ROPE_THETA = 500000.0
CMP_BLOCK = 32
CMP_STRIDE = 16
CMP_HIDDEN = 256
SEL_BLOCK = 64
N_SELECT = 8
WINDOW = 256
Q_WIDTH = N_HEADS * HEAD_DIM
KV_WIDTH = N_KV_GROUPS * HEAD_DIM
NEG_INF = -1e30

LANES = 128
F32_SUBLANES = 8
BF16_SUBLANES = 16
VMEM_LIMIT = 56 * 1024 * 1024

FFN_TOKENS = 512
SEQ_TILE = 512
ATT_Q = 256
ATT_K = 256
N_CMP_PAD = 128
GROUPS_PER_STEP = 4

N_SEL_BLOCKS = 32
GATE_ROWS = 2 * F32_SUBLANES
ROPE_HALF = ROPE_DIM // 2
V_ROWS = HEAD_DIM + BF16_SUBLANES
LOG2_E = 1.4426950408889634
ATT_COLS = HEADS_PER_GROUP * ATT_Q

_ROW_VS = Q_WIDTH
_ROW_VW = _ROW_VS + KV_WIDTH
_ROW_GATE = _ROW_VW + KV_WIDTH
_PROJ_ROWS = _ROW_GATE + N_KV_GROUPS * GATE_ROWS
_G_COLS = N_KV_GROUPS * LANES
_OFF_KW = _G_COLS
_OFF_KVC = 2 * _G_COLS
_PROJ_COLS = 3 * _G_COLS

_NT = (((1,), (1,)), ((), ()))


def _rms(x, g):
    return x * lax.rsqrt(jnp.mean(x * x, axis=-1, keepdims=True) + EPS) * g


def _const_spec(shape):
    nd = len(shape)
    return pl.BlockSpec(shape, lambda *_: (0,) * nd, pipeline_mode=pl.Buffered(1))


def _ffn_body(x_ref, gpre_ref, gpost_ref, wg_ref, wu_ref, wd_ref, o_ref):
    x = x_ref[...]
    xb = _rms(x, gpre_ref[...]).astype(BF16)
    hg = jnp.dot(xb, wg_ref[...], preferred_element_type=F32)
    hu = jnp.dot(xb, wu_ref[...], preferred_element_type=F32)
    act = (hg * jax.nn.sigmoid(hg) * hu).astype(BF16)
    f = jnp.dot(act, wd_ref[...], preferred_element_type=F32)
    o_ref[...] = x + _rms(f, gpost_ref[...])


def _ffn_block(x2d, g_pre, g_post, wg_all, wu_all, wd_all, layer, half):
    t = x2d.shape[0]
    row = pl.BlockSpec((FFN_TOKENS, D_MODEL), lambda i: (i, 0))
    pick = lambda *blk: pl.BlockSpec((None, None) + blk, lambda i: (layer, half, 0, 0),
                                     pipeline_mode=pl.Buffered(1))
    return pl.pallas_call(
        _ffn_body,
        grid=(t // FFN_TOKENS,),
        in_specs=[row, _const_spec((1, D_MODEL)), _const_spec((1, D_MODEL)),
                  pick(D_MODEL, D_FF), pick(D_MODEL, D_FF), pick(D_FF, D_MODEL)],
        out_specs=row,
        out_shape=jax.ShapeDtypeStruct((t, D_MODEL), F32),
        compiler_params=pltpu.CompilerParams(
            dimension_semantics=("parallel",), vmem_limit_bytes=VMEM_LIMIT),
        name="ffn_block",
    )(x2d, g_pre.reshape(1, -1), (0.5 * g_post).reshape(1, -1),
      wg_all, wu_all, wd_all)


def _pool_body(x_ref, g2_ref, g3_ref, w_ref, b_ref, sc_ref, o_ref, carry_ref):
    si = pl.program_id(1)

    @pl.when(si == 0)
    def _():
        carry_ref[...] = jnp.zeros_like(carry_ref)

    x = x_ref[...]
    h = _rms(x, g2_ref[...])
    hp = jnp.concatenate([carry_ref[...], h], axis=0)
    carry_ref[...] = h[SEQ_TILE - MAX_POOL_WINDOW:, :]
    pos = si * SEQ_TILE + lax.broadcasted_iota(jnp.int32, (SEQ_TILE, 1), 0)
    ys = []
    for g, w in enumerate(POOL_WINDOWS):
        cols = slice(g * POOL_GROUP, (g + 1) * POOL_GROUP)
        acc = hp[:, cols]
        k = 1
        while k < w:
            acc = acc + pltpu.roll(acc, k, axis=0)
            k *= 2
        cnt = jnp.minimum(pos + 1, w).astype(F32)
        d = acc[MAX_POOL_WINDOW:, :] / cnt - h[:, cols]
        y = jnp.dot(d.astype(BF16), w_ref[g], preferred_element_type=F32) + b_ref[g:g + 1, :]
        ys.append(y)
    m = jnp.concatenate(ys, axis=1) * sc_ref[...]
    o_ref[...] = x + _rms(m, g3_ref[...])


def _pool_block(x, g2, g3, w, b, scale):
    bsz, seq, _ = x.shape
    row = pl.BlockSpec((None, SEQ_TILE, D_MODEL), lambda bi, si: (bi, si, 0))
    return pl.pallas_call(
        _pool_body,
        grid=(bsz, seq // SEQ_TILE),
        in_specs=[row, _const_spec((1, D_MODEL)), _const_spec((1, D_MODEL)),
                  _const_spec((len(POOL_WINDOWS), POOL_GROUP, POOL_GROUP)),
                  _const_spec((len(POOL_WINDOWS), POOL_GROUP)), _const_spec((1, D_MODEL))],
        out_specs=row,
        out_shape=jax.ShapeDtypeStruct(x.shape, F32),
        scratch_shapes=[pltpu.VMEM((MAX_POOL_WINDOW, D_MODEL), F32)],
        compiler_params=pltpu.CompilerParams(
            dimension_semantics=("parallel", "arbitrary"), vmem_limit_bytes=VMEM_LIMIT),
        name="pool_block",
    )(x, g2.reshape(1, -1), g3.reshape(1, -1), w.astype(BF16), b, scale.reshape(1, -1))


def _proj_body(x_ref, g2_ref, wt_ref, wn_ref, c_ref, sa_ref, sb_ref, ct_ref, st_ref,
               qn_ref, qr_ref, vs_ref, vw_ref, gate_ref, ks_ref, kw_ref, kvc_ref):
    si = pl.program_id(1)
    hb = _rms(x_ref[...], g2_ref[...]).astype(BF16)

    pt = lax.dot_general(wt_ref[...], hb, _NT, preferred_element_type=F32)
    qn_ref[...] = pt[:Q_WIDTH].astype(BF16)
    cos_t, sin_t = ct_ref[...], st_ref[...]
    for h in range(N_HEADS):
        x1 = pt[h * HEAD_DIM:h * HEAD_DIM + ROPE_HALF]
        x2 = pt[h * HEAD_DIM + ROPE_HALF:h * HEAD_DIM + ROPE_DIM]
        rot = jnp.concatenate([x1 * cos_t - x2 * sin_t, x1 * sin_t + x2 * cos_t], axis=0)
        qr_ref[h * ROPE_DIM:(h + 1) * ROPE_DIM, :] = rot.astype(BF16)
    ones_rows = jnp.ones((V_ROWS - HEAD_DIM, SEQ_TILE), F32)
    for g in range(N_KV_GROUPS):
        vs_ref[g] = jnp.concatenate(
            [pt[_ROW_VS + g * HEAD_DIM:_ROW_VS + (g + 1) * HEAD_DIM], ones_rows], axis=0).astype(BF16)
        vw_ref[g] = jnp.concatenate(
            [pt[_ROW_VW + g * HEAD_DIM:_ROW_VW + (g + 1) * HEAD_DIM], ones_rows], axis=0).astype(BF16)
        gate_ref[g] = jax.nn.sigmoid(pt[_ROW_GATE + g * GATE_ROWS:_ROW_GATE + (g + 1) * GATE_ROWS])

    pn = jnp.dot(hb, wn_ref[...], preferred_element_type=F32)
    cos, sin_a, sin_b = c_ref[...], sa_ref[...], sb_ref[...]

    def slab(off, i):
        return pn[:, off + i * LANES: off + (i + 1) * LANES]

    def rope(t):
        return (t * cos + pltpu.roll(t, LANES - ROPE_HALF, axis=1) * sin_a
                + pltpu.roll(t, ROPE_HALF, axis=1) * sin_b)

    pos = si * SEQ_TILE + lax.broadcasted_iota(jnp.int32, (SEQ_TILE, LANES), 0)
    lane = lax.broadcasted_iota(jnp.int32, (SEQ_TILE, LANES), 1)
    blk_onehot = jnp.where(lane - HEAD_DIM == pos // SEL_BLOCK, 1.0, 0.0)
    for g in range(N_KV_GROUPS):
        ks_ref[g] = (rope(slab(0, g)) + blk_onehot).astype(BF16)
        kw_ref[g] = rope(slab(_OFF_KW, g)).astype(BF16)
        kvc_ref[g] = slab(_OFF_KVC, g)


def _nsa_proj_weights(w_in):
    d = w_in.shape[0]
    wq = w_in[:, :Q_WIDTH] * (HEAD_DIM ** -0.5)
    kv = w_in[:, Q_WIDTH:Q_WIDTH + 6 * KV_WIDTH].reshape(d, 6, N_KV_GROUPS, HEAD_DIM)
    wkc, wvc, wks, wvs, wkw, wvw = [kv[:, i] for i in range(6)]
    wgate = w_in[:, Q_WIDTH + 6 * KV_WIDTH:].reshape(d, 3, N_KV_GROUPS, HEADS_PER_GROUP)
    wgate = wgate.transpose(0, 2, 1, 3).reshape(d, N_KV_GROUPS, 3 * HEADS_PER_GROUP)
    wgate = jnp.pad(wgate, ((0, 0), (0, 0), (0, GATE_ROWS - 3 * HEADS_PER_GROUP)))
    w_t = jnp.concatenate([wq, wvs.reshape(d, KV_WIDTH), wvw.reshape(d, KV_WIDTH),
                           wgate.reshape(d, N_KV_GROUPS * GATE_ROWS)], axis=1).T.astype(BF16)
    zero = jnp.zeros_like(wkc)

    def pair(a, b):
        return jnp.concatenate([a, b], axis=-1).reshape(d, _G_COLS)

    w_n = jnp.concatenate([pair(wks * LOG2_E, zero), pair(wkw * LOG2_E, zero), pair(wkc, wvc)],
                          axis=1).astype(BF16)
    return w_t, w_n


def _rope_tables(seq):
    inv = 1.0 / (ROPE_THETA ** (jnp.arange(0, ROPE_DIM, 2, dtype=F32) / ROPE_DIM))
    ang = jnp.arange(seq, dtype=F32)[:, None] * inv[None, :]
    cos, sin = jnp.cos(ang), jnp.sin(ang)
    pad = LANES - ROPE_DIM
    c = jnp.concatenate([cos, cos, jnp.ones((seq, pad), F32)], axis=1)
    sa = jnp.concatenate([-sin, jnp.zeros((seq, ROPE_HALF + pad), F32)], axis=1)
    sb = jnp.concatenate([jnp.zeros((seq, ROPE_HALF), F32), sin, jnp.zeros((seq, pad), F32)], axis=1)
    return c, sa, sb, cos.T, sin.T


def _nsa_proj(x, g2, w_t, w_n, tables):
    bsz, seq, _ = x.shape
    feat = lambda rows, dt: jax.ShapeDtypeStruct((bsz, N_KV_GROUPS, rows, seq), dt)
    feat_spec = lambda rows: pl.BlockSpec((None, N_KV_GROUPS, rows, SEQ_TILE), lambda bi, si: (bi, 0, 0, si))
    tok = lambda dt: jax.ShapeDtypeStruct((bsz, N_KV_GROUPS, seq, LANES), dt)
    tok_spec = pl.BlockSpec((None, N_KV_GROUPS, SEQ_TILE, LANES), lambda bi, si: (bi, 0, si, 0))
    tab_spec = pl.BlockSpec((SEQ_TILE, LANES), lambda bi, si: (si, 0))
    tab_t_spec = pl.BlockSpec((ROPE_HALF, SEQ_TILE), lambda bi, si: (0, si))
    n_rot = N_HEADS * ROPE_DIM
    return pl.pallas_call(
        _proj_body,
        grid=(bsz, seq // SEQ_TILE),
        in_specs=[pl.BlockSpec((None, SEQ_TILE, D_MODEL), lambda bi, si: (bi, si, 0)),
                  _const_spec((1, D_MODEL)), _const_spec((_PROJ_ROWS, D_MODEL)),
                  _const_spec((D_MODEL, _PROJ_COLS)),
                  tab_spec, tab_spec, tab_spec, tab_t_spec, tab_t_spec],
        out_specs=[pl.BlockSpec((None, Q_WIDTH, SEQ_TILE), lambda bi, si: (bi, 0, si)),
                   pl.BlockSpec((None, n_rot, SEQ_TILE), lambda bi, si: (bi, 0, si)),
                   feat_spec(V_ROWS), feat_spec(V_ROWS), feat_spec(GATE_ROWS),
                   tok_spec, tok_spec, tok_spec],
        out_shape=[jax.ShapeDtypeStruct((bsz, Q_WIDTH, seq), BF16),
                   jax.ShapeDtypeStruct((bsz, n_rot, seq), BF16),
                   feat(V_ROWS, BF16), feat(V_ROWS, BF16), feat(GATE_ROWS, F32),
                   tok(BF16), tok(BF16), tok(F32)],
        compiler_params=pltpu.CompilerParams(
            dimension_semantics=("parallel", "parallel"), vmem_limit_bytes=VMEM_LIMIT),
        name="nsa_proj",
    )(x, g2.reshape(1, -1), w_t, w_n, *tables)


def _cmp_body(kvc_ref, pos_ref, w1_ref, w2k_ref, w2vt_ref, kc_ref, vct_ref):
    za, zb = [], []
    for t in range(CMP_STRIDE):
        xt = kvc_ref[pl.ds(t, N_CMP_PAD, stride=CMP_STRIDE), :]
        za.append((xt + pos_ref[t:t + 1, :]).astype(BF16))
        zb.append((xt + pos_ref[CMP_STRIDE + t:CMP_STRIDE + t + 1, :]).astype(BF16))
    first = jnp.dot(jnp.concatenate(za, axis=1), w1_ref[0], preferred_element_type=F32)
    second = jnp.dot(jnp.concatenate(zb, axis=1), w1_ref[1], preferred_element_type=F32)
    row = lax.broadcasted_iota(jnp.int32, second.shape, 0)
    second = jnp.where(row < N_CMP_PAD - 1, pltpu.roll(second, N_CMP_PAD - 1, axis=0), 0.0)
    pre = first + second
    hid = (pre * jax.nn.sigmoid(pre)).astype(BF16)
    kc_ref[...] = jnp.dot(hid[:, :CMP_HIDDEN], w2k_ref[...], preferred_element_type=F32).astype(BF16)
    vct_ref[...] = lax.dot_general(w2vt_ref[...], hid[:, CMP_HIDDEN:], _NT,
                                   preferred_element_type=F32).astype(BF16)


def _compress(kvc, pos_k, pos_v, wk1, wk2, wv1, wv2):
    bsz, _, seq, _ = kvc.shape
    assert seq == N_CMP_PAD * CMP_STRIDE
    pos = jnp.concatenate([pos_k, pos_v], axis=1)
    k1 = wk1.reshape(CMP_BLOCK, HEAD_DIM, CMP_HIDDEN)
    v1 = wv1.reshape(CMP_BLOCK, HEAD_DIM, CMP_HIDDEN)
    z1 = jnp.zeros_like(k1)
    w1 = jnp.concatenate([jnp.concatenate([k1, z1], axis=2),
                          jnp.concatenate([z1, v1], axis=2)], axis=1).astype(BF16)
    w1 = w1.reshape(2, CMP_STRIDE * LANES, 2 * CMP_HIDDEN)
    w2k = (wk2 * LOG2_E).astype(BF16)
    w2vt = wv2.T.astype(BF16)
    return pl.pallas_call(
        _cmp_body,
        grid=(bsz, N_KV_GROUPS),
        in_specs=[pl.BlockSpec((None, None, seq, LANES), lambda bi, gi: (bi, gi, 0, 0)),
                  _const_spec(pos.shape), _const_spec(w1.shape),
                  _const_spec(w2k.shape), _const_spec(w2vt.shape)],
        out_specs=[pl.BlockSpec((None, None, N_CMP_PAD, HEAD_DIM), lambda bi, gi: (bi, gi, 0, 0)),
                   pl.BlockSpec((None, None, HEAD_DIM, N_CMP_PAD), lambda bi, gi: (bi, gi, 0, 0))],
        out_shape=[jax.ShapeDtypeStruct((bsz, N_KV_GROUPS, N_CMP_PAD, HEAD_DIM), BF16),
                   jax.ShapeDtypeStruct((bsz, N_KV_GROUPS, HEAD_DIM, N_CMP_PAD), BF16)],
        compiler_params=pltpu.CompilerParams(dimension_semantics=("parallel", "parallel")),
        name="nsa_compress",
    )(kvc, pos, w1, w2k, w2vt)


_HEAD_COLS = [slice(r * ATT_Q, (r + 1) * ATT_Q) for r in range(HEADS_PER_GROUP)]


def _score_stage(k_ref, k0, lhs_ref, s_out, t_out, mask=None):
    k_tile = k_ref[pl.ds(k0, ATT_K), :]
    t_max = []
    for cs in _HEAD_COLS:
        s = jnp.dot(k_tile, lhs_ref[:, cs], preferred_element_type=F32)
        if mask is not None:
            s = jnp.where(mask, s, NEG_INF)
        s_out[:, cs] = s
        t_max.append(jnp.max(s, axis=0, keepdims=True))
    t_out[...] = jnp.concatenate(t_max, axis=1)


def _softmax_stage(s_in, t_in, m_ref, p_out, a_out):
    m_all = m_ref[...]
    t_all = t_in[...]
    m_new, alpha = [], []
    for cs in _HEAD_COLS:
        s = s_in[:, cs]
        m_c = jnp.maximum(m_all[:, cs], t_all[:, cs])
        p_out[:, cs] = jnp.exp2((s - m_c).astype(BF16))
        alpha.append(jnp.exp2(m_all[:, cs] - m_c))
        m_new.append(m_c)
    m_ref[...] = jnp.concatenate(m_new, axis=1)
    a_out[...] = jnp.concatenate(alpha, axis=1)


def _value_stage(vt_ref, k0, p_in, a_in, acc_ref):
    vt = vt_ref[:, pl.ds(k0, ATT_K)]
    acc_all, a_all = acc_ref[...], a_in[...]
    acc_ref[...] = jnp.concatenate(
        [a_all[:, cs] * acc_all[:, cs] + jnp.dot(vt, p_in[:, cs], preferred_element_type=F32)
         for cs in _HEAD_COLS], axis=1)


def _tree_sum(terms):
    while len(terms) > 1:
        terms = [a + b for a, b in zip(terms[::2], terms[1::2])] + terms[len(terms) & ~1:]
    return terms[0]


def _select_body(qn_ref, kc_ref, vct_ref, ov_ref, bias_ref, ocmp_ref):
    q0 = pl.program_id(1) * ATT_Q
    heads = [(g, r) for g in range(N_KV_GROUPS) for r in range(HEADS_PER_GROUP)]
    rows = lambda g, r: slice((g * HEADS_PER_GROUP + r) * HEAD_DIM, (g * HEADS_PER_GROUP + r + 1) * HEAD_DIM)

    blk_end = lax.broadcasted_iota(jnp.int32, (N_CMP_PAD, ATT_Q), 0) * CMP_STRIDE + CMP_BLOCK - 1
    qpos = q0 + lax.broadcasted_iota(jnp.int32, (N_CMP_PAD, ATT_Q), 1)
    cmp_mask = blk_end <= qpos
    any_cmp = q0 + lax.broadcasted_iota(jnp.int32, (1, ATT_Q), 1) >= CMP_BLOCK - 1
    scores = [jnp.dot(kc_ref[g], qn_ref[rows(g, r), :], preferred_element_type=F32) for g, r in heads]
    probs = []
    for s in scores:
        s = jnp.where(cmp_mask, s, NEG_INF)
        e = jnp.exp2(s - jnp.max(s, axis=0, keepdims=True))
        probs.append(e * jnp.where(any_cmp, 1.0 / jnp.sum(e, axis=0, keepdims=True), 0.0))
    for (g, r), p in zip(heads, probs):
        ocmp_ref[rows(g, r), :] = jnp.dot(vct_ref[g], p.astype(BF16), preferred_element_type=F32)

    p_sum = jnp.concatenate(
        [_tree_sum(probs[g * HEADS_PER_GROUP:(g + 1) * HEADS_PER_GROUP]) for g in range(N_KV_GROUPS)], axis=1)
    hi = p_sum.astype(BF16)
    rem = p_sum - hi.astype(F32)
    mid = rem.astype(BF16)
    lo = (rem - mid.astype(F32)).astype(BF16)
    ov = ov_ref[...]
    imp = (jnp.dot(ov, hi, preferred_element_type=F32) + jnp.dot(ov, mid, preferred_element_type=F32)
           + jnp.dot(ov, lo, preferred_element_type=F32))
    cols = N_KV_GROUPS * ATT_Q
    j = lax.broadcasted_iota(jnp.int32, (N_SEL_BLOCKS, cols), 0)
    q_in = lax.broadcasted_iota(jnp.int32, (N_SEL_BLOCKS, cols), 1) & (ATT_Q - 1)
    q_blk = (q0 + q_in) // SEL_BLOCK
    forced = (j == 0) | (j == q_blk) | (j == q_blk - 1)
    imp = jnp.where(forced, jnp.inf, jnp.where(j > q_blk, -jnp.inf, imp))
    sub = F32_SUBLANES
    row_in = lax.broadcasted_iota(jnp.int32, (sub, cols), 0)
    bias = []
    for b0 in range(0, N_SEL_BLOCKS, sub):
        mine = imp[b0:b0 + sub, :]
        beats = []
        for jp in range(N_SEL_BLOCKS):
            other = imp[jp:jp + 1, :]
            ge = jnp.where(other >= mine, 1.0, 0.0)
            gt = jnp.where(other > mine, 1.0, 0.0)
            if jp < b0:
                beats.append(ge)
            elif jp >= b0 + sub:
                beats.append(gt)
            else:
                beats.append(jnp.where(row_in > jp - b0, ge, gt))
        bias.append(jnp.where(_tree_sum(beats) < N_SELECT, 0.0, NEG_INF))
    bias = jnp.concatenate(bias, axis=0).astype(BF16)
    for g in range(N_KV_GROUPS):
        bias_ref[g] = bias[:, g * ATT_Q:(g + 1) * ATT_Q]


def _select(qn, kc, vct):
    bsz, _, seq = qn.shape
    assert seq // SEL_BLOCK == N_SEL_BLOCKS
    grp = lambda *blk: pl.BlockSpec((None, N_KV_GROUPS) + blk, lambda bi, qi: (bi, 0, 0, 0))
    q_spec = pl.BlockSpec((None, Q_WIDTH, ATT_Q), lambda bi, qi: (bi, 0, qi))
    return pl.pallas_call(
        _select_body,
        grid=(bsz, seq // ATT_Q),
        in_specs=[q_spec, grp(N_CMP_PAD, HEAD_DIM), grp(HEAD_DIM, N_CMP_PAD),
                  _const_spec((N_SEL_BLOCKS, N_CMP_PAD))],
        out_specs=[pl.BlockSpec((None, N_KV_GROUPS, N_SEL_BLOCKS, ATT_Q), lambda bi, qi: (bi, 0, 0, qi)),
                   q_spec],
        out_shape=[jax.ShapeDtypeStruct((bsz, N_KV_GROUPS, N_SEL_BLOCKS, seq), BF16),
                   jax.ShapeDtypeStruct((bsz, Q_WIDTH, seq), F32)],
        compiler_params=pltpu.CompilerParams(dimension_semantics=("parallel", "parallel")),
        name="nsa_select",
    )(qn, kc, vct, _overlap(seq))


def _window_branch(q0, prev0, vwt_ref, sw_ref, tw_ref, pw_ref, accw_ref):
    m_all = jnp.maximum(tw_ref[0], tw_ref[1])
    for cs in _HEAD_COLS:
        m_w = m_all[:, cs]
        pw_ref[0, :, cs] = jnp.exp2((sw_ref[0, :, cs] - m_w).astype(BF16))
        pw_ref[1, :, cs] = jnp.exp2((sw_ref[1, :, cs] - m_w).astype(BF16))
    vw_diag = vwt_ref[:, pl.ds(q0, ATT_K)]
    vw_prev = vwt_ref[:, pl.ds(prev0, ATT_K)]
    for cs in _HEAD_COLS:
        accw_ref[:, cs] = (jnp.dot(vw_diag, pw_ref[0, :, cs], preferred_element_type=F32)
                           + jnp.dot(vw_prev, pw_ref[1, :, cs], preferred_element_type=F32))


def _attn_body(qn_ref, qr_ref, bias_ref, ocmp_ref, ks_ref, kw_ref, vst_ref, vwt_ref, gate_ref,
               o_ref, lhs_ref, s_ref, sw_ref, p_ref, pw_ref, t_ref, tw_ref, a_ref, m_ref, acc_ref, accw_ref):
    qi = pl.program_id(2)
    q0 = pl.multiple_of(qi * ATT_Q, ATT_Q)
    prev0 = pl.multiple_of(jnp.maximum(qi - 1, 0) * ATT_K, ATT_K)
    groups = range(GROUPS_PER_STEP)
    gh = HEADS_PER_GROUP * HEAD_DIM

    pad = jnp.zeros((LANES - HEAD_DIM - N_SEL_BLOCKS, ATT_Q), BF16)
    for g in groups:
        for r in range(HEADS_PER_GROUP):
            h = g * HEADS_PER_GROUP + r
            lhs_ref[g, :, r * ATT_Q:(r + 1) * ATT_Q] = jnp.concatenate(
                [qr_ref[h * ROPE_DIM:(h + 1) * ROPE_DIM, :],
                 qn_ref[h * HEAD_DIM + ROPE_DIM:(h + 1) * HEAD_DIM, :], bias_ref[g], pad], axis=0)


    key_in = lax.broadcasted_iota(jnp.int32, (ATT_K, ATT_Q), 0)
    col_in = lax.broadcasted_iota(jnp.int32, (ATT_K, ATT_Q), 1)
    causal = key_in <= col_in
    prev = (key_in > col_in) & (qi > 0)

    def window_scores(g):
        _score_stage(kw_ref.at[g], q0, lhs_ref.at[g], sw_ref.at[g, 0], tw_ref.at[g, 0], mask=causal)
        _score_stage(kw_ref.at[g], prev0, lhs_ref.at[g], sw_ref.at[g, 1], tw_ref.at[g, 1], mask=prev)

    def window_rest(g):
        _window_branch(q0, prev0, vwt_ref.at[g], sw_ref.at[g], tw_ref.at[g], pw_ref.at[g], accw_ref.at[g])

    for g in groups:
        m_ref[g] = jnp.full(m_ref.shape[1:], NEG_INF, F32)
        acc_ref[g] = jnp.zeros(acc_ref.shape[1:], F32)
    first_mask = causal | (qi > 0)
    for g in groups:
        window_scores(g)
        if g > 0:
            window_rest(g - 1)
        _score_stage(ks_ref.at[g], 0, lhs_ref.at[g], s_ref.at[g, 0], t_ref.at[g, 0], mask=first_mask)
    window_rest(groups[-1])

    def finish(g, k0, slot):
        _softmax_stage(s_ref.at[g, slot], t_ref.at[g, slot], m_ref.at[g], p_ref.at[g, slot], a_ref.at[g, slot])
        _value_stage(vst_ref.at[g], k0, p_ref.at[g, slot], a_ref.at[g, slot], acc_ref.at[g])

    def trip(j, cur, last):
        for g in groups:
            _score_stage(ks_ref.at[g], pl.multiple_of((j + 1) * ATT_K, ATT_K), lhs_ref.at[g],
                         s_ref.at[g, 1 - cur], t_ref.at[g, 1 - cur], mask=causal if last else None)
            finish(g, pl.multiple_of(j * ATT_K, ATT_K), cur)
        if last:
            for g in groups:
                finish(g, q0, 1 - cur)

    def trip_any(j, carry):
        pl.when(j % 2 == 0)(lambda: trip(j, 0, False))
        pl.when(j % 2 == 1)(lambda: trip(j, 1, False))
        return carry

    lax.fori_loop(0, qi - 1, trip_any, 0)
    pl.when((qi > 0) & (qi % 2 == 1))(lambda: trip(qi - 1, 0, True))
    pl.when((qi > 0) & (qi % 2 == 0))(lambda: trip(qi - 1, 1, True))

    @pl.when(qi == 0)
    def _():
        for g in groups:
            finish(g, q0, 0)

    for g in groups:
        gate = gate_ref[g]
        heads = []
        for r, cs in enumerate(_HEAD_COLS):
            g_cmp = gate[r:r + 1, :]
            g_sel = gate[HEADS_PER_GROUP + r:HEADS_PER_GROUP + r + 1, :]
            g_win = gate[2 * HEADS_PER_GROUP + r:2 * HEADS_PER_GROUP + r + 1, :]
            c_sel = g_sel / acc_ref[g, HEAD_DIM:HEAD_DIM + 1, cs]
            c_win = g_win / accw_ref[g, HEAD_DIM:HEAD_DIM + 1, cs]
            h = g * HEADS_PER_GROUP + r
            o_cmp = ocmp_ref[h * HEAD_DIM:(h + 1) * HEAD_DIM, :]
            heads.append(g_cmp * o_cmp + c_sel * acc_ref[g, :HEAD_DIM, cs] + c_win * accw_ref[g, :HEAD_DIM, cs])
        o_ref[:, g * gh:(g + 1) * gh] = jnp.concatenate(heads, axis=0).T.astype(BF16)


def _overlap(seq):
    ci = jnp.arange(N_CMP_PAD) * CMP_STRIDE
    sj = jnp.arange(seq // SEL_BLOCK) * SEL_BLOCK
    n_cmp = (seq - CMP_BLOCK) // CMP_STRIDE + 1
    ov = ((ci[None, :] < sj[:, None] + SEL_BLOCK) & (ci[None, :] + CMP_BLOCK > sj[:, None])
          & (jnp.arange(N_CMP_PAD)[None, :] < n_cmp))
    return ov.astype(BF16)


def _attention(qn, qr, bias, ocmp, ks, kw, vst, vwt, gate):
    bsz, _, seq = qn.shape
    assert ATT_Q == WINDOW == ATT_K
    gps = GROUPS_PER_STEP
    gh = gps * HEADS_PER_GROUP * HEAD_DIM
    grp = lambda *blk: pl.BlockSpec((None, gps) + blk, lambda bi, gi, qi: (bi, gi, 0, 0))
    q_spec = pl.BlockSpec((None, gh, ATT_Q), lambda bi, gi, qi: (bi, gi, qi))
    return pl.pallas_call(
        _attn_body,
        grid=(bsz, N_KV_GROUPS // gps, seq // ATT_Q),
        in_specs=[q_spec,
                  pl.BlockSpec((None, gps * HEADS_PER_GROUP * ROPE_DIM, ATT_Q), lambda bi, gi, qi: (bi, gi, qi)),
                  pl.BlockSpec((None, gps, N_SEL_BLOCKS, ATT_Q), lambda bi, gi, qi: (bi, gi, 0, qi)),
                  q_spec,
                  grp(seq, LANES), grp(seq, LANES), grp(V_ROWS, seq), grp(V_ROWS, seq),
                  pl.BlockSpec((None, gps, GATE_ROWS, ATT_Q), lambda bi, gi, qi: (bi, gi, 0, qi))],
        out_specs=pl.BlockSpec((None, ATT_Q, gh), lambda bi, gi, qi: (bi, qi, gi)),
        out_shape=jax.ShapeDtypeStruct((bsz, seq, Q_WIDTH), BF16),
        scratch_shapes=[pltpu.VMEM((gps, LANES, ATT_COLS), BF16),
                        pltpu.VMEM((gps, 2, ATT_K, ATT_COLS), F32),
                        pltpu.VMEM((gps, 2, ATT_K, ATT_COLS), F32),
                        pltpu.VMEM((gps, 2, ATT_K, ATT_COLS), BF16),
                        pltpu.VMEM((gps, 2, ATT_K, ATT_COLS), BF16),
                        pltpu.VMEM((gps, 2, 1, ATT_COLS), F32),
                        pltpu.VMEM((gps, 2, 1, ATT_COLS), F32),
                        pltpu.VMEM((gps, 2, 1, ATT_COLS), F32),
                        pltpu.VMEM((gps, 1, ATT_COLS), F32),
                        pltpu.VMEM((gps, V_ROWS, ATT_COLS), F32),
                        pltpu.VMEM((gps, V_ROWS, ATT_COLS), F32)],
        compiler_params=pltpu.CompilerParams(
            dimension_semantics=("parallel", "parallel", "arbitrary"), vmem_limit_bytes=VMEM_LIMIT),
        name="nsa_attention",
    )(qn, qr, bias, ocmp, ks, kw, vst, vwt, gate)


def _oproj_body(o_ref_in, x_ref, w_ref, g_ref, out_ref):
    m = jnp.dot(o_ref_in[...], w_ref[...], preferred_element_type=F32)
    out_ref[...] = x_ref[...] + _rms(m, g_ref[...])


def _out_proj(o2d, x2d, w_o, g3):
    t = x2d.shape[0]
    row = pl.BlockSpec((FFN_TOKENS, D_MODEL), lambda i: (i, 0))
    return pl.pallas_call(
        _oproj_body,
        grid=(t // FFN_TOKENS,),
        in_specs=[row, row, _const_spec((Q_WIDTH, D_MODEL)), _const_spec((1, D_MODEL))],
        out_specs=row,
        out_shape=jax.ShapeDtypeStruct((t, D_MODEL), F32),
        compiler_params=pltpu.CompilerParams(dimension_semantics=("parallel",)),
        name="nsa_out_proj",
    )(o2d, x2d, w_o.astype(BF16), g3.reshape(1, -1))


def _nsa_block(x, g2, g3, w_in, pos_k, pos_v, wk1, wk2, wv1, wv2, w_o):
    bsz, seq, d = x.shape
    w_t, w_n = _nsa_proj_weights(w_in)
    qn, qr, vst, vwt, gate, ks, kw, kvc = _nsa_proj(x, g2, w_t, w_n, _rope_tables(seq))
    kc, vct = _compress(kvc, pos_k, pos_v, wk1, wk2, wv1, wv2)
    bias, ocmp = _select(qn, kc, vct)
    o = _attention(qn, qr, bias, ocmp, ks, kw, vst, vwt, gate)
    return _out_proj(o.reshape(bsz * seq, Q_WIDTH), x.reshape(bsz * seq, d), w_o, g3).reshape(x.shape)


def kernel(x, norm_gains, ffn_w_gate, ffn_w_up, ffn_w_down, pool_w, pool_b, pool_scale, nsa_w_in, nsa_cmp_pos_k, nsa_cmp_pos_v, nsa_cmp_wk1, nsa_cmp_wk2, nsa_cmp_wv1, nsa_cmp_wv2, nsa_w_o):
    bsz, seq, d = x.shape
    depth = norm_gains.shape[0]
    wg_all, wu_all, wd_all = ffn_w_gate.astype(BF16), ffn_w_up.astype(BF16), ffn_w_down.astype(BF16)

    def ffn(x, i, half):
        g = norm_gains[i]
        y = _ffn_block(x.reshape(bsz * seq, d), g[4 * half], g[4 * half + 1], wg_all, wu_all, wd_all, i, half)
        return y.reshape(bsz, seq, d)

    for i in range(depth):
        g = norm_gains[i]
        x = ffn(x, i, 0)
        j = i // 2
        if i % 2 == 0:
            x = _pool_block(x, g[2], g[3], pool_w[j], pool_b[j], pool_scale[j])
        else:
            x = _nsa_block(x, g[2], g[3], nsa_w_in[j], nsa_cmp_pos_k[j], nsa_cmp_pos_v[j],
                           nsa_cmp_wk1[j], nsa_cmp_wk2[j], nsa_cmp_wv1[j], nsa_cmp_wv2[j], nsa_w_o[j])
        x = ffn(x, i, 1)
    return x
```

```python
import jax
import jax.numpy as jnp
from jax import lax
from jax.experimental import pallas as pl
from jax.experimental.pallas import tpu as pltpu

F32 = jnp.float32
BF16 = jnp.bfloat16

D_MODEL = 1024
EPS = 1e-6
D_FF = 2816
POOL_WINDOWS = (2, 4, 8, 16)
POOL_GROUP = D_MODEL // len(POOL_WINDOWS)
MAX_POOL_WINDOW = max(POOL_WINDOWS)
N_HEADS = 16
HEAD_DIM = 64
N_KV_GROUPS = 4
HEADS_PER_GROUP = N_HEADS // N_KV_GROUPS
ROPE_DIM = HEAD_DIM // 4
ROPE_THETA = 500000.0
CMP_BLOCK = 32
CMP_STRIDE = 16
CMP_HIDDEN = 256
SEL_BLOCK = 64
N_SELECT = 8
WINDOW = 256
Q_WIDTH = N_HEADS * HEAD_DIM
KV_WIDTH = N_KV_GROUPS * HEAD_DIM
NEG_INF = -1e30

LANES = 128
F32_SUBLANES = 8
BF16_SUBLANES = 16
VMEM_LIMIT = 56 * 1024 * 1024

FFN_TOKENS = 512
SEQ_TILE = 512
ATT_Q = 256
ATT_K = 256
N_CMP_PAD = 128
GROUPS_PER_STEP = 4

N_SEL_BLOCKS = 32
GATE_ROWS = 2 * F32_SUBLANES
ROPE_HALF = ROPE_DIM // 2
V_ROWS = HEAD_DIM + BF16_SUBLANES
LOG2_E = 1.4426950408889634
ATT_COLS = HEADS_PER_GROUP * ATT_Q

_ROW_VS = Q_WIDTH
_ROW_VW = _ROW_VS + KV_WIDTH
_ROW_GATE = _ROW_VW + KV_WIDTH
_PROJ_ROWS = _ROW_GATE + N_KV_GROUPS * GATE_ROWS
_G_COLS = N_KV_GROUPS * LANES
_OFF_KW = _G_COLS
_OFF_KVC = 2 * _G_COLS
_PROJ_COLS = 3 * _G_COLS

_NT = (((1,), (1,)), ((), ()))


def _rms(x, g):
    return x * lax.rsqrt(jnp.mean(x * x, axis=-1, keepdims=True) + EPS) * g


def _const_spec(shape):
    nd = len(shape)
    return pl.BlockSpec(shape, lambda *_: (0,) * nd, pipeline_mode=pl.Buffered(1))


def _ffn_body(x_ref, gpre_ref, gpost_ref, wg_ref, wu_ref, wd_ref, o_ref):
    x = x_ref[...]
    xb = _rms(x, gpre_ref[...]).astype(BF16)
    hg = jnp.dot(xb, wg_ref[...], preferred_element_type=F32)
    hu = jnp.dot(xb, wu_ref[...], preferred_element_type=F32)
    act = (hg * jax.nn.sigmoid(hg) * hu).astype(BF16)
    f = jnp.dot(act, wd_ref[...], preferred_element_type=F32)
    o_ref[...] = x + _rms(f, gpost_ref[...])


def _ffn_block(x2d, g_pre, g_post, wg_all, wu_all, wd_all, layer, half):
    t = x2d.shape[0]
    row = pl.BlockSpec((FFN_TOKENS, D_MODEL), lambda i: (i, 0))
    pick = lambda *blk: pl.BlockSpec((None, None) + blk, lambda i: (layer, half, 0, 0),
                                     pipeline_mode=pl.Buffered(1))
    return pl.pallas_call(
        _ffn_body,
        grid=(t // FFN_TOKENS,),
        in_specs=[row, _const_spec((1, D_MODEL)), _const_spec((1, D_MODEL)),
                  pick(D_MODEL, D_FF), pick(D_MODEL, D_FF), pick(D_FF, D_MODEL)],
        out_specs=row,
        out_shape=jax.ShapeDtypeStruct((t, D_MODEL), F32),
        compiler_params=pltpu.CompilerParams(
            dimension_semantics=("parallel",), vmem_limit_bytes=VMEM_LIMIT,
            allow_input_fusion=[False, False, False, True, True, True]),
        name="ffn_block",
    )(x2d, g_pre.reshape(1, -1), (0.5 * g_post).reshape(1, -1),
      wg_all, wu_all, wd_all)


def _pool_body(x_ref, g2_ref, g3_ref, w_ref, b_ref, sc_ref, o_ref, carry_ref):
    si = pl.program_id(1)

    @pl.when(si == 0)
    def _():
        carry_ref[...] = jnp.zeros_like(carry_ref)

    x = x_ref[...]
    h = _rms(x, g2_ref[...])
    hp = jnp.concatenate([carry_ref[...], h], axis=0)
    carry_ref[...] = h[SEQ_TILE - MAX_POOL_WINDOW:, :]
    pos = si * SEQ_TILE + lax.broadcasted_iota(jnp.int32, (SEQ_TILE, 1), 0)
    ys = []
    for g, w in enumerate(POOL_WINDOWS):
        cols = slice(g * POOL_GROUP, (g + 1) * POOL_GROUP)
        acc = hp[:, cols]
        k = 1
        while k < w:
            acc = acc + pltpu.roll(acc, k, axis=0)
            k *= 2
        cnt = jnp.minimum(pos + 1, w).astype(F32)
        d = acc[MAX_POOL_WINDOW:, :] / cnt - h[:, cols]
        y = jnp.dot(d.astype(BF16), w_ref[g], preferred_element_type=F32) + b_ref[g:g + 1, :]
        ys.append(y)
    m = jnp.concatenate(ys, axis=1) * sc_ref[...]
    o_ref[...] = x + _rms(m, g3_ref[...])


def _pool_block(x, g2, g3, w, b, scale):
    bsz, seq, _ = x.shape
    row = pl.BlockSpec((None, SEQ_TILE, D_MODEL), lambda bi, si: (bi, si, 0))
    return pl.pallas_call(
        _pool_body,
        grid=(bsz, seq // SEQ_TILE),
        in_specs=[row, _const_spec((1, D_MODEL)), _const_spec((1, D_MODEL)),
                  _const_spec((len(POOL_WINDOWS), POOL_GROUP, POOL_GROUP)),
                  _const_spec((len(POOL_WINDOWS), POOL_GROUP)), _const_spec((1, D_MODEL))],
        out_specs=row,
        out_shape=jax.ShapeDtypeStruct(x.shape, F32),
        scratch_shapes=[pltpu.VMEM((MAX_POOL_WINDOW, D_MODEL), F32)],
        compiler_params=pltpu.CompilerParams(
            dimension_semantics=("parallel", "arbitrary"), vmem_limit_bytes=VMEM_LIMIT),
        name="pool_block",
    )(x, g2.reshape(1, -1), g3.reshape(1, -1), w.astype(BF16), b, scale.reshape(1, -1))


def _proj_body(x_ref, g2_ref, wt_ref, wn_ref, c_ref, sa_ref, sb_ref, ct_ref, st_ref,
               qn_ref, qr_ref, vs_ref, vw_ref, gate_ref, ks_ref, kw_ref, kvc_ref):
    si = pl.program_id(1)
    hb = _rms(x_ref[...], g2_ref[...]).astype(BF16)

    pt = lax.dot_general(wt_ref[...], hb, _NT, preferred_element_type=F32)
    qn_ref[...] = pt[:Q_WIDTH].astype(BF16)
    cos_t, sin_t = ct_ref[...], st_ref[...]
    for h in range(N_HEADS):
        x1 = pt[h * HEAD_DIM:h * HEAD_DIM + ROPE_HALF]
        x2 = pt[h * HEAD_DIM + ROPE_HALF:h * HEAD_DIM + ROPE_DIM]
        rot = jnp.concatenate([x1 * cos_t - x2 * sin_t, x1 * sin_t + x2 * cos_t], axis=0)
        qr_ref[h * ROPE_DIM:(h + 1) * ROPE_DIM, :] = rot.astype(BF16)
    ones_rows = jnp.ones((V_ROWS - HEAD_DIM, SEQ_TILE), F32)
    for g in range(N_KV_GROUPS):
        vs_ref[g] = jnp.concatenate(
            [pt[_ROW_VS + g * HEAD_DIM:_ROW_VS + (g + 1) * HEAD_DIM], ones_rows], axis=0).astype(BF16)
        vw_ref[g] = jnp.concatenate(
            [pt[_ROW_VW + g * HEAD_DIM:_ROW_VW + (g + 1) * HEAD_DIM], ones_rows], axis=0).astype(BF16)
        gate_ref[g] = jax.nn.sigmoid(pt[_ROW_GATE + g * GATE_ROWS:_ROW_GATE + (g + 1) * GATE_ROWS])

    pn = jnp.dot(hb, wn_ref[...], preferred_element_type=F32)
    cos, sin_a, sin_b = c_ref[...], sa_ref[...], sb_ref[...]

    def slab(off, i):
        return pn[:, off + i * LANES: off + (i + 1) * LANES]

    def rope(t):
        return (t * cos + pltpu.roll(t, LANES - ROPE_HALF, axis=1) * sin_a
                + pltpu.roll(t, ROPE_HALF, axis=1) * sin_b)

    pos = si * SEQ_TILE + lax.broadcasted_iota(jnp.int32, (SEQ_TILE, LANES), 0)
    lane = lax.broadcasted_iota(jnp.int32, (SEQ_TILE, LANES), 1)
    blk_onehot = jnp.where(lane - HEAD_DIM == pos // SEL_BLOCK, 1.0, 0.0)
    for g in range(N_KV_GROUPS):
        ks_ref[g] = (rope(slab(0, g)) + blk_onehot).astype(BF16)
        kw_ref[g] = rope(slab(_OFF_KW, g)).astype(BF16)
        kvc_ref[g] = slab(_OFF_KVC, g)


def _nsa_proj_weights(w_in):
    d = w_in.shape[0]
    wq = w_in[:, :Q_WIDTH] * (HEAD_DIM ** -0.5)
    kv = w_in[:, Q_WIDTH:Q_WIDTH + 6 * KV_WIDTH].reshape(d, 6, N_KV_GROUPS, HEAD_DIM)
    wkc, wvc, wks, wvs, wkw, wvw = [kv[:, i] for i in range(6)]
    wgate = w_in[:, Q_WIDTH + 6 * KV_WIDTH:].reshape(d, 3, N_KV_GROUPS, HEADS_PER_GROUP)
    wgate = wgate.transpose(0, 2, 1, 3).reshape(d, N_KV_GROUPS, 3 * HEADS_PER_GROUP)
    wgate = jnp.pad(wgate, ((0, 0), (0, 0), (0, GATE_ROWS - 3 * HEADS_PER_GROUP)))
    w_t = jnp.concatenate([wq, wvs.reshape(d, KV_WIDTH), wvw.reshape(d, KV_WIDTH),
                           wgate.reshape(d, N_KV_GROUPS * GATE_ROWS)], axis=1).T.astype(BF16)
    zero = jnp.zeros_like(wkc)

    def pair(a, b):
        return jnp.concatenate([a, b], axis=-1).reshape(d, _G_COLS)

    w_n = jnp.concatenate([pair(wks * LOG2_E, zero), pair(wkw * LOG2_E, zero), pair(wkc, wvc)],
                          axis=1).astype(BF16)
    return w_t, w_n


def _rope_tables(seq):
    inv = 1.0 / (ROPE_THETA ** (jnp.arange(0, ROPE_DIM, 2, dtype=F32) / ROPE_DIM))
    ang = jnp.arange(seq, dtype=F32)[:, None] * inv[None, :]
    cos, sin = jnp.cos(ang), jnp.sin(ang)
    pad = LANES - ROPE_DIM
    c = jnp.concatenate([cos, cos, jnp.ones((seq, pad), F32)], axis=1)
    sa = jnp.concatenate([-sin, jnp.zeros((seq, ROPE_HALF + pad), F32)], axis=1)
    sb = jnp.concatenate([jnp.zeros((seq, ROPE_HALF), F32), sin, jnp.zeros((seq, pad), F32)], axis=1)
    return c, sa, sb, cos.T, sin.T


def _nsa_proj(x, g2, w_t, w_n, tables):
    bsz, seq, _ = x.shape
    feat = lambda rows, dt: jax.ShapeDtypeStruct((bsz, N_KV_GROUPS, rows, seq), dt)
    feat_spec = lambda rows: pl.BlockSpec((None, N_KV_GROUPS, rows, SEQ_TILE), lambda bi, si: (bi, 0, 0, si))
    tok = lambda dt: jax.ShapeDtypeStruct((bsz, N_KV_GROUPS, seq, LANES), dt)
    tok_spec = pl.BlockSpec((None, N_KV_GROUPS, SEQ_TILE, LANES), lambda bi, si: (bi, 0, si, 0))
    tab_spec = pl.BlockSpec((SEQ_TILE, LANES), lambda bi, si: (si, 0))
    tab_t_spec = pl.BlockSpec((ROPE_HALF, SEQ_TILE), lambda bi, si: (0, si))
    n_rot = N_HEADS * ROPE_DIM
    return pl.pallas_call(
        _proj_body,
        grid=(bsz, seq // SEQ_TILE),
        in_specs=[pl.BlockSpec((None, SEQ_TILE, D_MODEL), lambda bi, si: (bi, si, 0)),
                  _const_spec((1, D_MODEL)), _const_spec((_PROJ_ROWS, D_MODEL)),
                  _const_spec((D_MODEL, _PROJ_COLS)),
                  tab_spec, tab_spec, tab_spec, tab_t_spec, tab_t_spec],
        out_specs=[pl.BlockSpec((None, Q_WIDTH, SEQ_TILE), lambda bi, si: (bi, 0, si)),
                   pl.BlockSpec((None, n_rot, SEQ_TILE), lambda bi, si: (bi, 0, si)),
                   feat_spec(V_ROWS), feat_spec(V_ROWS), feat_spec(GATE_ROWS),
                   tok_spec, tok_spec, tok_spec],
        out_shape=[jax.ShapeDtypeStruct((bsz, Q_WIDTH, seq), BF16),
                   jax.ShapeDtypeStruct((bsz, n_rot, seq), BF16),
                   feat(V_ROWS, BF16), feat(V_ROWS, BF16), feat(GATE_ROWS, F32),
                   tok(BF16), tok(BF16), tok(F32)],
        compiler_params=pltpu.CompilerParams(
            dimension_semantics=("parallel", "parallel"), vmem_limit_bytes=VMEM_LIMIT),
        name="nsa_proj",
    )(x, g2.reshape(1, -1), w_t, w_n, *tables)


def _cmp_body(kvc_ref, pos_ref, w1_ref, w2k_ref, w2vt_ref, kc_ref, vct_ref):
    za, zb = [], []
    for t in range(CMP_STRIDE):
        xt = kvc_ref[pl.ds(t, N_CMP_PAD, stride=CMP_STRIDE), :]
        za.append((xt + pos_ref[t:t + 1, :]).astype(BF16))
        zb.append((xt + pos_ref[CMP_STRIDE + t:CMP_STRIDE + t + 1, :]).astype(BF16))
    first = jnp.dot(jnp.concatenate(za, axis=1), w1_ref[0], preferred_element_type=F32)
    second = jnp.dot(jnp.concatenate(zb, axis=1), w1_ref[1], preferred_element_type=F32)
    row = lax.broadcasted_iota(jnp.int32, second.shape, 0)
    second = jnp.where(row < N_CMP_PAD - 1, pltpu.roll(second, N_CMP_PAD - 1, axis=0), 0.0)
    pre = first + second
    hid = (pre * jax.nn.sigmoid(pre)).astype(BF16)
    kc_ref[...] = jnp.dot(hid[:, :CMP_HIDDEN], w2k_ref[...], preferred_element_type=F32).astype(BF16)
    vct_ref[...] = lax.dot_general(w2vt_ref[...], hid[:, CMP_HIDDEN:], _NT,
                                   preferred_element_type=F32).astype(BF16)


def _compress(kvc, pos_k, pos_v, wk1, wk2, wv1, wv2):
    bsz, _, seq, _ = kvc.shape
    assert seq == N_CMP_PAD * CMP_STRIDE
    pos = jnp.concatenate([pos_k, pos_v], axis=1)
    k1 = wk1.reshape(CMP_BLOCK, HEAD_DIM, CMP_HIDDEN)
    v1 = wv1.reshape(CMP_BLOCK, HEAD_DIM, CMP_HIDDEN)
    z1 = jnp.zeros_like(k1)
    w1 = jnp.concatenate([jnp.concatenate([k1, z1], axis=2),
                          jnp.concatenate([z1, v1], axis=2)], axis=1).astype(BF16)
    w1 = w1.reshape(2, CMP_STRIDE * LANES, 2 * CMP_HIDDEN)
    w2k = (wk2 * LOG2_E).astype(BF16)
    w2vt = wv2.T.astype(BF16)
    return pl.pallas_call(
        _cmp_body,
        grid=(bsz, N_KV_GROUPS),
        in_specs=[pl.BlockSpec((None, None, seq, LANES), lambda bi, gi: (bi, gi, 0, 0)),
                  _const_spec(pos.shape), _const_spec(w1.shape),
                  _const_spec(w2k.shape), _const_spec(w2vt.shape)],
        out_specs=[pl.BlockSpec((None, None, N_CMP_PAD, HEAD_DIM), lambda bi, gi: (bi, gi, 0, 0)),
                   pl.BlockSpec((None, None, HEAD_DIM, N_CMP_PAD), lambda bi, gi: (bi, gi, 0, 0))],
        out_shape=[jax.ShapeDtypeStruct((bsz, N_KV_GROUPS, N_CMP_PAD, HEAD_DIM), BF16),
                   jax.ShapeDtypeStruct((bsz, N_KV_GROUPS, HEAD_DIM, N_CMP_PAD), BF16)],
        compiler_params=pltpu.CompilerParams(dimension_semantics=("parallel", "parallel")),
        name="nsa_compress",
    )(kvc, pos, w1, w2k, w2vt)


_HEAD_COLS = [slice(r * ATT_Q, (r + 1) * ATT_Q) for r in range(HEADS_PER_GROUP)]


def _score_stage(k_ref, k0, lhs_ref, s_out, t_out, mask=None):
    k_tile = k_ref[pl.ds(k0, ATT_K), :]
    t_max = []
    for cs in _HEAD_COLS:
        s = jnp.dot(k_tile, lhs_ref[:, cs], preferred_element_type=F32)
        if mask is not None:
            s = jnp.where(mask, s, NEG_INF)
        s_out[:, cs] = s
        t_max.append(jnp.max(s, axis=0, keepdims=True))
    t_out[...] = jnp.concatenate(t_max, axis=1)


def _softmax_stage(s_in, t_in, m_ref, p_out, a_out):
    m_all = m_ref[...]
    t_all = t_in[...]
    m_new, alpha = [], []
    for cs in _HEAD_COLS:
        s = s_in[:, cs]
        m_c = jnp.maximum(m_all[:, cs], t_all[:, cs])
        p_out[:, cs] = jnp.exp2((s - m_c).astype(BF16))
        alpha.append(jnp.exp2(m_all[:, cs] - m_c))
        m_new.append(m_c)
    m_ref[...] = jnp.concatenate(m_new, axis=1)
    a_out[...] = jnp.concatenate(alpha, axis=1)


def _value_stage(vt_ref, k0, p_in, a_in, acc_ref):
    vt = vt_ref[:, pl.ds(k0, ATT_K)]
    acc_all, a_all = acc_ref[...], a_in[...]
    acc_ref[...] = jnp.concatenate(
        [a_all[:, cs] * acc_all[:, cs] + jnp.dot(vt, p_in[:, cs], preferred_element_type=F32)
         for cs in _HEAD_COLS], axis=1)


def _tree_sum(terms):
    while len(terms) > 1:
        terms = [a + b for a, b in zip(terms[::2], terms[1::2])] + terms[len(terms) & ~1:]
    return terms[0]


def _select_body(qn_ref, kc_ref, vct_ref, ov_ref, bias_ref, ocmp_ref):
    q0 = pl.program_id(1) * ATT_Q
    heads = [(g, r) for g in range(N_KV_GROUPS) for r in range(HEADS_PER_GROUP)]
    rows = lambda g, r: slice((g * HEADS_PER_GROUP + r) * HEAD_DIM, (g * HEADS_PER_GROUP + r + 1) * HEAD_DIM)

    blk_end = lax.broadcasted_iota(jnp.int32, (N_CMP_PAD, ATT_Q), 0) * CMP_STRIDE + CMP_BLOCK - 1
    qpos = q0 + lax.broadcasted_iota(jnp.int32, (N_CMP_PAD, ATT_Q), 1)
    cmp_mask = blk_end <= qpos
    any_cmp = q0 + lax.broadcasted_iota(jnp.int32, (1, ATT_Q), 1) >= CMP_BLOCK - 1
    scores = [jnp.dot(kc_ref[g], qn_ref[rows(g, r), :], preferred_element_type=F32) for g, r in heads]
    probs = []
    for s in scores:
        s = jnp.where(cmp_mask, s, NEG_INF)
        e = jnp.exp2(s - jnp.max(s, axis=0, keepdims=True))
        probs.append(e * jnp.where(any_cmp, 1.0 / jnp.sum(e, axis=0, keepdims=True), 0.0))
    for (g, r), p in zip(heads, probs):
        ocmp_ref[rows(g, r), :] = jnp.dot(vct_ref[g], p.astype(BF16), preferred_element_type=F32)

    p_sum = jnp.concatenate(
        [_tree_sum(probs[g * HEADS_PER_GROUP:(g + 1) * HEADS_PER_GROUP]) for g in range(N_KV_GROUPS)], axis=1)
    hi = p_sum.astype(BF16)
    rem = p_sum - hi.astype(F32)
    mid = rem.astype(BF16)
    lo = (rem - mid.astype(F32)).astype(BF16)
    ov = ov_ref[...]
    imp = (jnp.dot(ov, hi, preferred_element_type=F32) + jnp.dot(ov, mid, preferred_element_type=F32)
           + jnp.dot(ov, lo, preferred_element_type=F32))
    cols = N_KV_GROUPS * ATT_Q
    j = lax.broadcasted_iota(jnp.int32, (N_SEL_BLOCKS, cols), 0)
    q_in = lax.broadcasted_iota(jnp.int32, (N_SEL_BLOCKS, cols), 1) & (ATT_Q - 1)
    q_blk = (q0 + q_in) // SEL_BLOCK
    forced = (j == 0) | (j == q_blk) | (j == q_blk - 1)
    imp = jnp.where(forced, jnp.inf, jnp.where(j > q_blk, -jnp.inf, imp))
    sub = F32_SUBLANES
    row_in = lax.broadcasted_iota(jnp.int32, (sub, cols), 0)
    bias = []
    for b0 in range(0, N_SEL_BLOCKS, sub):
        mine = imp[b0:b0 + sub, :]
        beats = []
        for jp in range(N_SEL_BLOCKS):
            other = imp[jp:jp + 1, :]
            ge = jnp.where(other >= mine, 1.0, 0.0)
            gt = jnp.where(other > mine, 1.0, 0.0)
            if jp < b0:
                beats.append(ge)
            elif jp >= b0 + sub:
                beats.append(gt)
            else:
                beats.append(jnp.where(row_in > jp - b0, ge, gt))
        bias.append(jnp.where(_tree_sum(beats) < N_SELECT, 0.0, NEG_INF))
    bias = jnp.concatenate(bias, axis=0).astype(BF16)
    for g in range(N_KV_GROUPS):
        bias_ref[g] = bias[:, g * ATT_Q:(g + 1) * ATT_Q]


def _select(qn, kc, vct):
    bsz, _, seq = qn.shape
    assert seq // SEL_BLOCK == N_SEL_BLOCKS
    grp = lambda *blk: pl.BlockSpec((None, N_KV_GROUPS) + blk, lambda bi, qi: (bi, 0, 0, 0))
    q_spec = pl.BlockSpec((None, Q_WIDTH, ATT_Q), lambda bi, qi: (bi, 0, qi))
    return pl.pallas_call(
        _select_body,
        grid=(bsz, seq // ATT_Q),
        in_specs=[q_spec, grp(N_CMP_PAD, HEAD_DIM), grp(HEAD_DIM, N_CMP_PAD),
                  _const_spec((N_SEL_BLOCKS, N_CMP_PAD))],
        out_specs=[pl.BlockSpec((None, N_KV_GROUPS, N_SEL_BLOCKS, ATT_Q), lambda bi, qi: (bi, 0, 0, qi)),
                   q_spec],
        out_shape=[jax.ShapeDtypeStruct((bsz, N_KV_GROUPS, N_SEL_BLOCKS, seq), BF16),
                   jax.ShapeDtypeStruct((bsz, Q_WIDTH, seq), F32)],
        compiler_params=pltpu.CompilerParams(dimension_semantics=("parallel", "parallel")),
        name="nsa_select",
    )(qn, kc, vct, _overlap(seq))


def _window_branch(q0, prev0, vwt_ref, sw_ref, tw_ref, pw_ref, accw_ref):
    m_all = jnp.maximum(tw_ref[0], tw_ref[1])
    for cs in _HEAD_COLS:
        m_w = m_all[:, cs]
        pw_ref[0, :, cs] = jnp.exp2((sw_ref[0, :, cs] - m_w).astype(BF16))
        pw_ref[1, :, cs] = jnp.exp2((sw_ref[1, :, cs] - m_w).astype(BF16))
    vw_diag = vwt_ref[:, pl.ds(q0, ATT_K)]
    vw_prev = vwt_ref[:, pl.ds(prev0, ATT_K)]
    for cs in _HEAD_COLS:
        accw_ref[:, cs] = (jnp.dot(vw_diag, pw_ref[0, :, cs], preferred_element_type=F32)
                           + jnp.dot(vw_prev, pw_ref[1, :, cs], preferred_element_type=F32))


def _attn_body(qn_ref, qr_ref, bias_ref, ocmp_ref, ks_ref, kw_ref, vst_ref, vwt_ref, gate_ref,
               o_ref, lhs_ref, s_ref, sw_ref, p_ref, pw_ref, t_ref, tw_ref, a_ref, m_ref, acc_ref, accw_ref):
    qi = pl.program_id(2)
    q0 = pl.multiple_of(qi * ATT_Q, ATT_Q)
    prev0 = pl.multiple_of(jnp.maximum(qi - 1, 0) * ATT_K, ATT_K)
    groups = range(GROUPS_PER_STEP)
    gh = HEADS_PER_GROUP * HEAD_DIM

    pad = jnp.zeros((LANES - HEAD_DIM - N_SEL_BLOCKS, ATT_Q), BF16)
    for g in groups:
        for r in range(HEADS_PER_GROUP):
            h = g * HEADS_PER_GROUP + r
            lhs_ref[g, :, r * ATT_Q:(r + 1) * ATT_Q] = jnp.concatenate(
                [qr_ref[h * ROPE_DIM:(h + 1) * ROPE_DIM, :],
                 qn_ref[h * HEAD_DIM + ROPE_DIM:(h + 1) * HEAD_DIM, :], bias_ref[g], pad], axis=0)


    key_in = lax.broadcasted_iota(jnp.int32, (ATT_K, ATT_Q), 0)
    col_in = lax.broadcasted_iota(jnp.int32, (ATT_K, ATT_Q), 1)
    causal = key_in <= col_in
    prev = (key_in > col_in) & (qi > 0)

    def window_scores(g):
        _score_stage(kw_ref.at[g], q0, lhs_ref.at[g], sw_ref.at[g, 0], tw_ref.at[g, 0], mask=causal)
        _score_stage(kw_ref.at[g], prev0, lhs_ref.at[g], sw_ref.at[g, 1], tw_ref.at[g, 1], mask=prev)

    def window_rest(g):
        _window_branch(q0, prev0, vwt_ref.at[g], sw_ref.at[g], tw_ref.at[g], pw_ref.at[g], accw_ref.at[g])

    for g in groups:
        m_ref[g] = jnp.full(m_ref.shape[1:], NEG_INF, F32)
        acc_ref[g] = jnp.zeros(acc_ref.shape[1:], F32)
    first_mask = causal | (qi > 0)
    for g in groups:
        window_scores(g)
        if g > 0:
            window_rest(g - 1)
        _score_stage(ks_ref.at[g], 0, lhs_ref.at[g], s_ref.at[g, 0], t_ref.at[g, 0], mask=first_mask)
    window_rest(groups[-1])

    def finish(g, k0, slot):
        _softmax_stage(s_ref.at[g, slot], t_ref.at[g, slot], m_ref.at[g], p_ref.at[g, slot], a_ref.at[g, slot])
        _value_stage(vst_ref.at[g], k0, p_ref.at[g, slot], a_ref.at[g, slot], acc_ref.at[g])

    def trip(j, cur, last):
        for g in groups:
            _score_stage(ks_ref.at[g], pl.multiple_of((j + 1) * ATT_K, ATT_K), lhs_ref.at[g],
                         s_ref.at[g, 1 - cur], t_ref.at[g, 1 - cur], mask=causal if last else None)
            finish(g, pl.multiple_of(j * ATT_K, ATT_K), cur)
        if last:
            for g in groups:
                finish(g, q0, 1 - cur)

    def trip_any(j, carry):
        pl.when(j % 2 == 0)(lambda: trip(j, 0, False))
        pl.when(j % 2 == 1)(lambda: trip(j, 1, False))
        return carry

    lax.fori_loop(0, qi - 1, trip_any, 0)
    pl.when((qi > 0) & (qi % 2 == 1))(lambda: trip(qi - 1, 0, True))
    pl.when((qi > 0) & (qi % 2 == 0))(lambda: trip(qi - 1, 1, True))

    @pl.when(qi == 0)
    def _():
        for g in groups:
            finish(g, q0, 0)

    for g in groups:
        gate = gate_ref[g]
        heads = []
        for r, cs in enumerate(_HEAD_COLS):
            g_cmp = gate[r:r + 1, :]
            g_sel = gate[HEADS_PER_GROUP + r:HEADS_PER_GROUP + r + 1, :]
            g_win = gate[2 * HEADS_PER_GROUP + r:2 * HEADS_PER_GROUP + r + 1, :]
            c_sel = g_sel / acc_ref[g, HEAD_DIM:HEAD_DIM + 1, cs]
            c_win = g_win / accw_ref[g, HEAD_DIM:HEAD_DIM + 1, cs]
            h = g * HEADS_PER_GROUP + r
            o_cmp = ocmp_ref[h * HEAD_DIM:(h + 1) * HEAD_DIM, :]
            heads.append(g_cmp * o_cmp + c_sel * acc_ref[g, :HEAD_DIM, cs] + c_win * accw_ref[g, :HEAD_DIM, cs])
        o_ref[:, g * gh:(g + 1) * gh] = jnp.concatenate(heads, axis=0).T.astype(BF16)


def _overlap(seq):
    ci = jnp.arange(N_CMP_PAD) * CMP_STRIDE
    sj = jnp.arange(seq // SEL_BLOCK) * SEL_BLOCK
    n_cmp = (seq - CMP_BLOCK) // CMP_STRIDE + 1
    ov = ((ci[None, :] < sj[:, None] + SEL_BLOCK) & (ci[None, :] + CMP_BLOCK > sj[:, None])
          & (jnp.arange(N_CMP_PAD)[None, :] < n_cmp))
    return ov.astype(BF16)


def _attention(qn, qr, bias, ocmp, ks, kw, vst, vwt, gate):
    bsz, _, seq = qn.shape
    assert ATT_Q == WINDOW == ATT_K
    gps = GROUPS_PER_STEP
    gh = gps * HEADS_PER_GROUP * HEAD_DIM
    grp = lambda *blk: pl.BlockSpec((None, gps) + blk, lambda bi, gi, qi: (bi, gi, 0, 0))
    q_spec = pl.BlockSpec((None, gh, ATT_Q), lambda bi, gi, qi: (bi, gi, qi))
    return pl.pallas_call(
        _attn_body,
        grid=(bsz, N_KV_GROUPS // gps, seq // ATT_Q),
        in_specs=[q_spec,
                  pl.BlockSpec((None, gps * HEADS_PER_GROUP * ROPE_DIM, ATT_Q), lambda bi, gi, qi: (bi, gi, qi)),
                  pl.BlockSpec((None, gps, N_SEL_BLOCKS, ATT_Q), lambda bi, gi, qi: (bi, gi, 0, qi)),
                  q_spec,
                  grp(seq, LANES), grp(seq, LANES), grp(V_ROWS, seq), grp(V_ROWS, seq),
                  pl.BlockSpec((None, gps, GATE_ROWS, ATT_Q), lambda bi, gi, qi: (bi, gi, 0, qi))],
        out_specs=pl.BlockSpec((None, ATT_Q, gh), lambda bi, gi, qi: (bi, qi, gi)),
        out_shape=jax.ShapeDtypeStruct((bsz, seq, Q_WIDTH), BF16),
        scratch_shapes=[pltpu.VMEM((gps, LANES, ATT_COLS), BF16),
                        pltpu.VMEM((gps, 2, ATT_K, ATT_COLS), F32),
                        pltpu.VMEM((gps, 2, ATT_K, ATT_COLS), F32),
                        pltpu.VMEM((gps, 2, ATT_K, ATT_COLS), BF16),
                        pltpu.VMEM((gps, 2, ATT_K, ATT_COLS), BF16),
                        pltpu.VMEM((gps, 2, 1, ATT_COLS), F32),
                        pltpu.VMEM((gps, 2, 1, ATT_COLS), F32),
                        pltpu.VMEM((gps, 2, 1, ATT_COLS), F32),
                        pltpu.VMEM((gps, 1, ATT_COLS), F32),
                        pltpu.VMEM((gps, V_ROWS, ATT_COLS), F32),
                        pltpu.VMEM((gps, V_ROWS, ATT_COLS), F32)],
        compiler_params=pltpu.CompilerParams(
            dimension_semantics=("parallel", "parallel", "arbitrary"), vmem_limit_bytes=VMEM_LIMIT),
        name="nsa_attention",
    )(qn, qr, bias, ocmp, ks, kw, vst, vwt, gate)


def _oproj_body(o_ref_in, x_ref, w_ref, g_ref, out_ref):
    m = jnp.dot(o_ref_in[...], w_ref[...], preferred_element_type=F32)
    out_ref[...] = x_ref[...] + _rms(m, g_ref[...])


def _out_proj(o2d, x2d, w_o, g3):
    t = x2d.shape[0]
    row = pl.BlockSpec((FFN_TOKENS, D_MODEL), lambda i: (i, 0))
    return pl.pallas_call(
        _oproj_body,
        grid=(t // FFN_TOKENS,),
        in_specs=[row, row, _const_spec((Q_WIDTH, D_MODEL)), _const_spec((1, D_MODEL))],
        out_specs=row,
        out_shape=jax.ShapeDtypeStruct((t, D_MODEL), F32),
        compiler_params=pltpu.CompilerParams(dimension_semantics=("parallel",)),
        name="nsa_out_proj",
    )(o2d, x2d, w_o.astype(BF16), g3.reshape(1, -1))


def _nsa_block(x, g2, g3, w_in, pos_k, pos_v, wk1, wk2, wv1, wv2, w_o):
    bsz, seq, d = x.shape
    w_t, w_n = _nsa_proj_weights(w_in)
    qn, qr, vst, vwt, gate, ks, kw, kvc = _nsa_proj(x, g2, w_t, w_n, _rope_tables(seq))
    kc, vct = _compress(kvc, pos_k, pos_v, wk1, wk2, wv1, wv2)
    bias, ocmp = _select(qn, kc, vct)
    o = _attention(qn, qr, bias, ocmp, ks, kw, vst, vwt, gate)
    return _out_proj(o.reshape(bsz * seq, Q_WIDTH), x.reshape(bsz * seq, d), w_o, g3).reshape(x.shape)


def kernel(x, norm_gains, ffn_w_gate, ffn_w_up, ffn_w_down, pool_w, pool_b, pool_scale, nsa_w_in, nsa_cmp_pos_k, nsa_cmp_pos_v, nsa_cmp_wk1, nsa_cmp_wk2, nsa_cmp_wv1, nsa_cmp_wv2, nsa_w_o):
    bsz, seq, d = x.shape
    depth = norm_gains.shape[0]
    wg_all, wu_all, wd_all = ffn_w_gate.astype(BF16), ffn_w_up.astype(BF16), ffn_w_down.astype(BF16)

    def ffn(x, i, half):
        g = norm_gains[i]
        y = _ffn_block(x.reshape(bsz * seq, d), g[4 * half], g[4 * half + 1], wg_all, wu_all, wd_all, i, half)
        return y.reshape(bsz, seq, d)

    for i in range(depth):
        g = norm_gains[i]
        x = ffn(x, i, 0)
        j = i // 2
        if i % 2 == 0:
            x = _pool_block(x, g[2], g[3], pool_w[j], pool_b[j], pool_scale[j])
        else:
            x = _nsa_block(x, g[2], g[3], nsa_w_in[j], nsa_cmp_pos_k[j], nsa_cmp_pos_v[j],
                           nsa_cmp_wk1[j], nsa_cmp_wk2[j], nsa_cmp_wv1[j], nsa_cmp_wv2[j], nsa_w_o[j])
        x = ffn(x, i, 1)
    return x
```

```python
import jax
import jax.numpy as jnp
from jax import lax
from jax.experimental import pallas as pl
from jax.experimental.pallas import tpu as pltpu

F32 = jnp.float32
BF16 = jnp.bfloat16

D_MODEL = 1024
EPS = 1e-6
D_FF = 2816
POOL_WINDOWS = (2, 4, 8, 16)
POOL_GROUP = D_MODEL // len(POOL_WINDOWS)
MAX_POOL_WINDOW = max(POOL_WINDOWS)
N_HEADS = 16
HEAD_DIM = 64
N_KV_GROUPS = 4
HEADS_PER_GROUP = N_HEADS // N_KV_GROUPS
ROPE_DIM = HEAD_DIM // 4
ROPE_THETA = 500000.0
CMP_BLOCK = 32
CMP_STRIDE = 16
CMP_HIDDEN = 256
SEL_BLOCK = 64
N_SELECT = 8
WINDOW = 256
Q_WIDTH = N_HEADS * HEAD_DIM
KV_WIDTH = N_KV_GROUPS * HEAD_DIM
NEG_INF = -1e30

LANES = 128
F32_SUBLANES = 8
BF16_SUBLANES = 16
VMEM_LIMIT = 56 * 1024 * 1024

FFN_TOKENS = 512
FFN_PARTS = 2
SEQ_TILE = 512
ATT_Q = 256
ATT_K = 256
N_CMP_PAD = 128
GROUPS_PER_STEP = 4

N_SEL_BLOCKS = 32
GATE_ROWS = 2 * F32_SUBLANES
ROPE_HALF = ROPE_DIM // 2
V_ROWS = HEAD_DIM + BF16_SUBLANES
LOG2_E = 1.4426950408889634
ATT_COLS = HEADS_PER_GROUP * ATT_Q

_ROW_VS = Q_WIDTH
_ROW_VW = _ROW_VS + KV_WIDTH
_ROW_GATE = _ROW_VW + KV_WIDTH
_PROJ_ROWS = _ROW_GATE + N_KV_GROUPS * GATE_ROWS
_G_COLS = N_KV_GROUPS * LANES
_OFF_KW = _G_COLS
_OFF_KVC = 2 * _G_COLS
_PROJ_COLS = 3 * _G_COLS

_NT = (((1,), (1,)), ((), ()))


def _rms(x, g):
    return x * lax.rsqrt(jnp.mean(x * x, axis=-1, keepdims=True) + EPS) * g


def _const_spec(shape):
    nd = len(shape)
    return pl.BlockSpec(shape, lambda *_: (0,) * nd, pipeline_mode=pl.Buffered(1))


def _ffn_body(x_ref, gpre_ref, gpost_ref, wg_ref, wu_ref, wd_ref, o_ref):
    part = FFN_TOKENS // FFN_PARTS
    halves = [slice(i * part, (i + 1) * part) for i in range(FFN_PARTS)]
    xs = [x_ref[rows, :] for rows in halves]
    hidden = []
    for x in xs:
        xb = _rms(x, gpre_ref[...]).astype(BF16)
        hidden.append((jnp.dot(xb, wg_ref[...], preferred_element_type=F32),
                       jnp.dot(xb, wu_ref[...], preferred_element_type=F32)))
    outs = []
    for hg, hu in hidden:
        act = (hg * jax.nn.sigmoid(hg) * hu).astype(BF16)
        outs.append(jnp.dot(act, wd_ref[...], preferred_element_type=F32))
    for rows, x, f in zip(halves, xs, outs):
        o_ref[rows, :] = x + _rms(f, gpost_ref[...])


def _ffn_block(x2d, g_pre, g_post, wg_all, wu_all, wd_all, layer, half):
    t = x2d.shape[0]
    row = pl.BlockSpec((FFN_TOKENS, D_MODEL), lambda i: (i, 0))
    pick = lambda *blk: pl.BlockSpec((None, None) + blk, lambda i: (layer, half, 0, 0),
                                     pipeline_mode=pl.Buffered(1))
    return pl.pallas_call(
        _ffn_body,
        grid=(t // FFN_TOKENS,),
        in_specs=[row, _const_spec((1, D_MODEL)), _const_spec((1, D_MODEL)),
                  pick(D_MODEL, D_FF), pick(D_MODEL, D_FF), pick(D_FF, D_MODEL)],
        out_specs=row,
        out_shape=jax.ShapeDtypeStruct((t, D_MODEL), F32),
        compiler_params=pltpu.CompilerParams(
            dimension_semantics=("parallel",), vmem_limit_bytes=VMEM_LIMIT),
        name="ffn_block",
    )(x2d, g_pre.reshape(1, -1), (0.5 * g_post).reshape(1, -1),
      wg_all, wu_all, wd_all)


def _pool_body(x_ref, g2_ref, g3_ref, w_ref, b_ref, sc_ref, o_ref, carry_ref):
    si = pl.program_id(1)

    @pl.when(si == 0)
    def _():
        carry_ref[...] = jnp.zeros_like(carry_ref)

    x = x_ref[...]
    h = _rms(x, g2_ref[...])
    hp = jnp.concatenate([carry_ref[...], h], axis=0)
    carry_ref[...] = h[SEQ_TILE - MAX_POOL_WINDOW:, :]
    pos = si * SEQ_TILE + lax.broadcasted_iota(jnp.int32, (SEQ_TILE, 1), 0)
    ys = []
    for g, w in enumerate(POOL_WINDOWS):
        cols = slice(g * POOL_GROUP, (g + 1) * POOL_GROUP)
        acc = hp[:, cols]
        k = 1
        while k < w:
            acc = acc + pltpu.roll(acc, k, axis=0)
            k *= 2
        cnt = jnp.minimum(pos + 1, w).astype(F32)
        d = acc[MAX_POOL_WINDOW:, :] / cnt - h[:, cols]
        y = jnp.dot(d.astype(BF16), w_ref[g], preferred_element_type=F32) + b_ref[g:g + 1, :]
        ys.append(y)
    m = jnp.concatenate(ys, axis=1) * sc_ref[...]
    o_ref[...] = x + _rms(m, g3_ref[...])


def _pool_block(x, g2, g3, w, b, scale):
    bsz, seq, _ = x.shape
    row = pl.BlockSpec((None, SEQ_TILE, D_MODEL), lambda bi, si: (bi, si, 0))
    return pl.pallas_call(
        _pool_body,
        grid=(bsz, seq // SEQ_TILE),
        in_specs=[row, _const_spec((1, D_MODEL)), _const_spec((1, D_MODEL)),
                  _const_spec((len(POOL_WINDOWS), POOL_GROUP, POOL_GROUP)),
                  _const_spec((len(POOL_WINDOWS), POOL_GROUP)), _const_spec((1, D_MODEL))],
        out_specs=row,
        out_shape=jax.ShapeDtypeStruct(x.shape, F32),
        scratch_shapes=[pltpu.VMEM((MAX_POOL_WINDOW, D_MODEL), F32)],
        compiler_params=pltpu.CompilerParams(
            dimension_semantics=("parallel", "arbitrary"), vmem_limit_bytes=VMEM_LIMIT),
        name="pool_block",
    )(x, g2.reshape(1, -1), g3.reshape(1, -1), w.astype(BF16), b, scale.reshape(1, -1))


def _proj_body(x_ref, g2_ref, wt_ref, wn_ref, c_ref, sa_ref, sb_ref, ct_ref, st_ref,
               qn_ref, qr_ref, vs_ref, vw_ref, gate_ref, ks_ref, kw_ref, kvc_ref):
    si = pl.program_id(1)
    hb = _rms(x_ref[...], g2_ref[...]).astype(BF16)

    pt = lax.dot_general(wt_ref[...], hb, _NT, preferred_element_type=F32)
    qn_ref[...] = pt[:Q_WIDTH].astype(BF16)
    cos_t, sin_t = ct_ref[...], st_ref[...]
    for h in range(N_HEADS):
        x1 = pt[h * HEAD_DIM:h * HEAD_DIM + ROPE_HALF]
        x2 = pt[h * HEAD_DIM + ROPE_HALF:h * HEAD_DIM + ROPE_DIM]
        rot = jnp.concatenate([x1 * cos_t - x2 * sin_t, x1 * sin_t + x2 * cos_t], axis=0)
        qr_ref[h * ROPE_DIM:(h + 1) * ROPE_DIM, :] = rot.astype(BF16)
    ones_rows = jnp.ones((V_ROWS - HEAD_DIM, SEQ_TILE), F32)
    for g in range(N_KV_GROUPS):
        vs_ref[g] = jnp.concatenate(
            [pt[_ROW_VS + g * HEAD_DIM:_ROW_VS + (g + 1) * HEAD_DIM], ones_rows], axis=0).astype(BF16)
        vw_ref[g] = jnp.concatenate(
            [pt[_ROW_VW + g * HEAD_DIM:_ROW_VW + (g + 1) * HEAD_DIM], ones_rows], axis=0).astype(BF16)
        gate_ref[g] = jax.nn.sigmoid(pt[_ROW_GATE + g * GATE_ROWS:_ROW_GATE + (g + 1) * GATE_ROWS])

    pn = jnp.dot(hb, wn_ref[...], preferred_element_type=F32)
    cos, sin_a, sin_b = c_ref[...], sa_ref[...], sb_ref[...]

    def slab(off, i):
        return pn[:, off + i * LANES: off + (i + 1) * LANES]

    def rope(t):
        return (t * cos + pltpu.roll(t, LANES - ROPE_HALF, axis=1) * sin_a
                + pltpu.roll(t, ROPE_HALF, axis=1) * sin_b)

    pos = si * SEQ_TILE + lax.broadcasted_iota(jnp.int32, (SEQ_TILE, LANES), 0)
    lane = lax.broadcasted_iota(jnp.int32, (SEQ_TILE, LANES), 1)
    blk_onehot = jnp.where(lane - HEAD_DIM == pos // SEL_BLOCK, 1.0, 0.0)
    for g in range(N_KV_GROUPS):
        ks_ref[g] = (rope(slab(0, g)) + blk_onehot).astype(BF16)
        kw_ref[g] = rope(slab(_OFF_KW, g)).astype(BF16)
        kvc_ref[g] = slab(_OFF_KVC, g)


def _nsa_proj_weights(w_in):
    d = w_in.shape[0]
    wq = w_in[:, :Q_WIDTH] * (HEAD_DIM ** -0.5)
    kv = w_in[:, Q_WIDTH:Q_WIDTH + 6 * KV_WIDTH].reshape(d, 6, N_KV_GROUPS, HEAD_DIM)
    wkc, wvc, wks, wvs, wkw, wvw = [kv[:, i] for i in range(6)]
    wgate = w_in[:, Q_WIDTH + 6 * KV_WIDTH:].reshape(d, 3, N_KV_GROUPS, HEADS_PER_GROUP)
    wgate = wgate.transpose(0, 2, 1, 3).reshape(d, N_KV_GROUPS, 3 * HEADS_PER_GROUP)
    wgate = jnp.pad(wgate, ((0, 0), (0, 0), (0, GATE_ROWS - 3 * HEADS_PER_GROUP)))
    w_t = jnp.concatenate([wq, wvs.reshape(d, KV_WIDTH), wvw.reshape(d, KV_WIDTH),
                           wgate.reshape(d, N_KV_GROUPS * GATE_ROWS)], axis=1).T.astype(BF16)
    zero = jnp.zeros_like(wkc)

    def pair(a, b):
        return jnp.concatenate([a, b], axis=-1).reshape(d, _G_COLS)

    w_n = jnp.concatenate([pair(wks * LOG2_E, zero), pair(wkw * LOG2_E, zero), pair(wkc, wvc)],
                          axis=1).astype(BF16)
    return w_t, w_n


def _rope_tables(seq):
    inv = 1.0 / (ROPE_THETA ** (jnp.arange(0, ROPE_DIM, 2, dtype=F32) / ROPE_DIM))
    ang = jnp.arange(seq, dtype=F32)[:, None] * inv[None, :]
    cos, sin = jnp.cos(ang), jnp.sin(ang)
    pad = LANES - ROPE_DIM
    c = jnp.concatenate([cos, cos, jnp.ones((seq, pad), F32)], axis=1)
    sa = jnp.concatenate([-sin, jnp.zeros((seq, ROPE_HALF + pad), F32)], axis=1)
    sb = jnp.concatenate([jnp.zeros((seq, ROPE_HALF), F32), sin, jnp.zeros((seq, pad), F32)], axis=1)
    return c, sa, sb, cos.T, sin.T


def _nsa_proj(x, g2, w_t, w_n, tables):
    bsz, seq, _ = x.shape
    feat = lambda rows, dt: jax.ShapeDtypeStruct((bsz, N_KV_GROUPS, rows, seq), dt)
    feat_spec = lambda rows: pl.BlockSpec((None, N_KV_GROUPS, rows, SEQ_TILE), lambda bi, si: (bi, 0, 0, si))
    tok = lambda dt: jax.ShapeDtypeStruct((bsz, N_KV_GROUPS, seq, LANES), dt)
    tok_spec = pl.BlockSpec((None, N_KV_GROUPS, SEQ_TILE, LANES), lambda bi, si: (bi, 0, si, 0))
    tab_spec = pl.BlockSpec((SEQ_TILE, LANES), lambda bi, si: (si, 0))
    tab_t_spec = pl.BlockSpec((ROPE_HALF, SEQ_TILE), lambda bi, si: (0, si))
    n_rot = N_HEADS * ROPE_DIM
    return pl.pallas_call(
        _proj_body,
        grid=(bsz, seq // SEQ_TILE),
        in_specs=[pl.BlockSpec((None, SEQ_TILE, D_MODEL), lambda bi, si: (bi, si, 0)),
                  _const_spec((1, D_MODEL)), _const_spec((_PROJ_ROWS, D_MODEL)),
                  _const_spec((D_MODEL, _PROJ_COLS)),
                  tab_spec, tab_spec, tab_spec, tab_t_spec, tab_t_spec],
        out_specs=[pl.BlockSpec((None, Q_WIDTH, SEQ_TILE), lambda bi, si: (bi, 0, si)),
                   pl.BlockSpec((None, n_rot, SEQ_TILE), lambda bi, si: (bi, 0, si)),
                   feat_spec(V_ROWS), feat_spec(V_ROWS), feat_spec(GATE_ROWS),
                   tok_spec, tok_spec, tok_spec],
        out_shape=[jax.ShapeDtypeStruct((bsz, Q_WIDTH, seq), BF16),
                   jax.ShapeDtypeStruct((bsz, n_rot, seq), BF16),
                   feat(V_ROWS, BF16), feat(V_ROWS, BF16), feat(GATE_ROWS, F32),
                   tok(BF16), tok(BF16), tok(F32)],
        compiler_params=pltpu.CompilerParams(
            dimension_semantics=("parallel", "parallel"), vmem_limit_bytes=VMEM_LIMIT),
        name="nsa_proj",
    )(x, g2.reshape(1, -1), w_t, w_n, *tables)


def _cmp_body(kvc_ref, pos_ref, w1_ref, w2k_ref, w2vt_ref, kc_ref, vct_ref):
    za, zb = [], []
    for t in range(CMP_STRIDE):
        xt = kvc_ref[pl.ds(t, N_CMP_PAD, stride=CMP_STRIDE), :]
        za.append((xt + pos_ref[t:t + 1, :]).astype(BF16))
        zb.append((xt + pos_ref[CMP_STRIDE + t:CMP_STRIDE + t + 1, :]).astype(BF16))
    first = jnp.dot(jnp.concatenate(za, axis=1), w1_ref[0], preferred_element_type=F32)
    second = jnp.dot(jnp.concatenate(zb, axis=1), w1_ref[1], preferred_element_type=F32)
    row = lax.broadcasted_iota(jnp.int32, second.shape, 0)
    second = jnp.where(row < N_CMP_PAD - 1, pltpu.roll(second, N_CMP_PAD - 1, axis=0), 0.0)
    pre = first + second
    hid = (pre * jax.nn.sigmoid(pre)).astype(BF16)
    kc_ref[...] = jnp.dot(hid[:, :CMP_HIDDEN], w2k_ref[...], preferred_element_type=F32).astype(BF16)
    vct_ref[...] = lax.dot_general(w2vt_ref[...], hid[:, CMP_HIDDEN:], _NT,
                                   preferred_element_type=F32).astype(BF16)


def _compress(kvc, pos_k, pos_v, wk1, wk2, wv1, wv2):
    bsz, _, seq, _ = kvc.shape
    assert seq == N_CMP_PAD * CMP_STRIDE
    pos = jnp.concatenate([pos_k, pos_v], axis=1)
    k1 = wk1.reshape(CMP_BLOCK, HEAD_DIM, CMP_HIDDEN)
    v1 = wv1.reshape(CMP_BLOCK, HEAD_DIM, CMP_HIDDEN)
    z1 = jnp.zeros_like(k1)
    w1 = jnp.concatenate([jnp.concatenate([k1, z1], axis=2),
                          jnp.concatenate([z1, v1], axis=2)], axis=1).astype(BF16)
    w1 = w1.reshape(2, CMP_STRIDE * LANES, 2 * CMP_HIDDEN)
    w2k = (wk2 * LOG2_E).astype(BF16)
    w2vt = wv2.T.astype(BF16)
    return pl.pallas_call(
        _cmp_body,
        grid=(bsz, N_KV_GROUPS),
        in_specs=[pl.BlockSpec((None, None, seq, LANES), lambda bi, gi: (bi, gi, 0, 0)),
                  _const_spec(pos.shape), _const_spec(w1.shape),
                  _const_spec(w2k.shape), _const_spec(w2vt.shape)],
        out_specs=[pl.BlockSpec((None, None, N_CMP_PAD, HEAD_DIM), lambda bi, gi: (bi, gi, 0, 0)),
                   pl.BlockSpec((None, None, HEAD_DIM, N_CMP_PAD), lambda bi, gi: (bi, gi, 0, 0))],
        out_shape=[jax.ShapeDtypeStruct((bsz, N_KV_GROUPS, N_CMP_PAD, HEAD_DIM), BF16),
                   jax.ShapeDtypeStruct((bsz, N_KV_GROUPS, HEAD_DIM, N_CMP_PAD), BF16)],
        compiler_params=pltpu.CompilerParams(dimension_semantics=("parallel", "parallel")),
        name="nsa_compress",
    )(kvc, pos, w1, w2k, w2vt)


_HEAD_COLS = [slice(r * ATT_Q, (r + 1) * ATT_Q) for r in range(HEADS_PER_GROUP)]


def _score_stage(k_ref, k0, lhs_ref, s_out, t_out, mask=None):
    k_tile = k_ref[pl.ds(k0, ATT_K), :]
    t_max = []
    for cs in _HEAD_COLS:
        s = jnp.dot(k_tile, lhs_ref[:, cs], preferred_element_type=F32)
        if mask is not None:
            s = jnp.where(mask, s, NEG_INF)
        s_out[:, cs] = s
        t_max.append(jnp.max(s, axis=0, keepdims=True))
    t_out[...] = jnp.concatenate(t_max, axis=1)


def _softmax_stage(s_in, t_in, m_ref, p_out, a_out):
    m_all = m_ref[...]
    t_all = t_in[...]
    m_new, alpha = [], []
    for cs in _HEAD_COLS:
        s = s_in[:, cs]
        m_c = jnp.maximum(m_all[:, cs], t_all[:, cs])
        p_out[:, cs] = jnp.exp2((s - m_c).astype(BF16))
        alpha.append(jnp.exp2(m_all[:, cs] - m_c))
        m_new.append(m_c)
    m_ref[...] = jnp.concatenate(m_new, axis=1)
    a_out[...] = jnp.concatenate(alpha, axis=1)


def _value_stage(vt_ref, k0, p_in, a_in, acc_ref):
    vt = vt_ref[:, pl.ds(k0, ATT_K)]
    acc_all, a_all = acc_ref[...], a_in[...]
    acc_ref[...] = jnp.concatenate(
        [a_all[:, cs] * acc_all[:, cs] + jnp.dot(vt, p_in[:, cs], preferred_element_type=F32)
         for cs in _HEAD_COLS], axis=1)


def _tree_sum(terms):
    while len(terms) > 1:
        terms = [a + b for a, b in zip(terms[::2], terms[1::2])] + terms[len(terms) & ~1:]
    return terms[0]


def _select_body(qn_ref, kc_ref, vct_ref, ov_ref, bias_ref, ocmp_ref):
    q0 = pl.program_id(1) * ATT_Q
    heads = [(g, r) for g in range(N_KV_GROUPS) for r in range(HEADS_PER_GROUP)]
    rows = lambda g, r: slice((g * HEADS_PER_GROUP + r) * HEAD_DIM, (g * HEADS_PER_GROUP + r + 1) * HEAD_DIM)

    blk_end = lax.broadcasted_iota(jnp.int32, (N_CMP_PAD, ATT_Q), 0) * CMP_STRIDE + CMP_BLOCK - 1
    qpos = q0 + lax.broadcasted_iota(jnp.int32, (N_CMP_PAD, ATT_Q), 1)
    cmp_mask = blk_end <= qpos
    any_cmp = q0 + lax.broadcasted_iota(jnp.int32, (1, ATT_Q), 1) >= CMP_BLOCK - 1
    scores = [jnp.dot(kc_ref[g], qn_ref[rows(g, r), :], preferred_element_type=F32) for g, r in heads]
    probs = []
    for s in scores:
        s = jnp.where(cmp_mask, s, NEG_INF)
        e = jnp.exp2(s - jnp.max(s, axis=0, keepdims=True))
        probs.append(e * jnp.where(any_cmp, 1.0 / jnp.sum(e, axis=0, keepdims=True), 0.0))
    for (g, r), p in zip(heads, probs):
        ocmp_ref[rows(g, r), :] = jnp.dot(vct_ref[g], p.astype(BF16), preferred_element_type=F32)

    p_sum = jnp.concatenate(
        [_tree_sum(probs[g * HEADS_PER_GROUP:(g + 1) * HEADS_PER_GROUP]) for g in range(N_KV_GROUPS)], axis=1)
    hi = p_sum.astype(BF16)
    rem = p_sum - hi.astype(F32)
    mid = rem.astype(BF16)
    lo = (rem - mid.astype(F32)).astype(BF16)
    ov = ov_ref[...]
    imp = (jnp.dot(ov, hi, preferred_element_type=F32) + jnp.dot(ov, mid, preferred_element_type=F32)
           + jnp.dot(ov, lo, preferred_element_type=F32))
    cols = N_KV_GROUPS * ATT_Q
    j = lax.broadcasted_iota(jnp.int32, (N_SEL_BLOCKS, cols), 0)
    q_in = lax.broadcasted_iota(jnp.int32, (N_SEL_BLOCKS, cols), 1) & (ATT_Q - 1)
    q_blk = (q0 + q_in) // SEL_BLOCK
    forced = (j == 0) | (j == q_blk) | (j == q_blk - 1)
    imp = jnp.where(forced, jnp.inf, jnp.where(j > q_blk, -jnp.inf, imp))
    sub = F32_SUBLANES
    row_in = lax.broadcasted_iota(jnp.int32, (sub, cols), 0)
    bias = []
    for b0 in range(0, N_SEL_BLOCKS, sub):
        mine = imp[b0:b0 + sub, :]
        beats = []
        for jp in range(N_SEL_BLOCKS):
            other = imp[jp:jp + 1, :]
            ge = jnp.where(other >= mine, 1.0, 0.0)
            gt = jnp.where(other > mine, 1.0, 0.0)
            if jp < b0:
                beats.append(ge)
            elif jp >= b0 + sub:
                beats.append(gt)
            else:
                beats.append(jnp.where(row_in > jp - b0, ge, gt))
        bias.append(jnp.where(_tree_sum(beats) < N_SELECT, 0.0, NEG_INF))
    bias = jnp.concatenate(bias, axis=0).astype(BF16)
    for g in range(N_KV_GROUPS):
        bias_ref[g] = bias[:, g * ATT_Q:(g + 1) * ATT_Q]


def _select(qn, kc, vct):
    bsz, _, seq = qn.shape
    assert seq // SEL_BLOCK == N_SEL_BLOCKS
    grp = lambda *blk: pl.BlockSpec((None, N_KV_GROUPS) + blk, lambda bi, qi: (bi, 0, 0, 0))
    q_spec = pl.BlockSpec((None, Q_WIDTH, ATT_Q), lambda bi, qi: (bi, 0, qi))
    return pl.pallas_call(
        _select_body,
        grid=(bsz, seq // ATT_Q),
        in_specs=[q_spec, grp(N_CMP_PAD, HEAD_DIM), grp(HEAD_DIM, N_CMP_PAD),
                  _const_spec((N_SEL_BLOCKS, N_CMP_PAD))],
        out_specs=[pl.BlockSpec((None, N_KV_GROUPS, N_SEL_BLOCKS, ATT_Q), lambda bi, qi: (bi, 0, 0, qi)),
                   q_spec],
        out_shape=[jax.ShapeDtypeStruct((bsz, N_KV_GROUPS, N_SEL_BLOCKS, seq), BF16),
                   jax.ShapeDtypeStruct((bsz, Q_WIDTH, seq), F32)],
        compiler_params=pltpu.CompilerParams(dimension_semantics=("parallel", "parallel")),
        name="nsa_select",
    )(qn, kc, vct, _overlap(seq))


def _window_branch(q0, prev0, vwt_ref, sw_ref, tw_ref, pw_ref, accw_ref):
    m_all = jnp.maximum(tw_ref[0], tw_ref[1])
    for cs in _HEAD_COLS:
        m_w = m_all[:, cs]
        pw_ref[0, :, cs] = jnp.exp2((sw_ref[0, :, cs] - m_w).astype(BF16))
        pw_ref[1, :, cs] = jnp.exp2((sw_ref[1, :, cs] - m_w).astype(BF16))
    vw_diag = vwt_ref[:, pl.ds(q0, ATT_K)]
    vw_prev = vwt_ref[:, pl.ds(prev0, ATT_K)]
    for cs in _HEAD_COLS:
        accw_ref[:, cs] = (jnp.dot(vw_diag, pw_ref[0, :, cs], preferred_element_type=F32)
                           + jnp.dot(vw_prev, pw_ref[1, :, cs], preferred_element_type=F32))


def _attn_body(qn_ref, qr_ref, bias_ref, ocmp_ref, ks_ref, kw_ref, vst_ref, vwt_ref, gate_ref,
               o_ref, lhs_ref, s_ref, sw_ref, p_ref, pw_ref, t_ref, tw_ref, a_ref, m_ref, acc_ref, accw_ref):
    qi = pl.program_id(2)
    q0 = pl.multiple_of(qi * ATT_Q, ATT_Q)
    prev0 = pl.multiple_of(jnp.maximum(qi - 1, 0) * ATT_K, ATT_K)
    groups = range(GROUPS_PER_STEP)
    gh = HEADS_PER_GROUP * HEAD_DIM

    pad = jnp.zeros((LANES - HEAD_DIM - N_SEL_BLOCKS, ATT_Q), BF16)
    for g in groups:
        for r in range(HEADS_PER_GROUP):
            h = g * HEADS_PER_GROUP + r
            lhs_ref[g, :, r * ATT_Q:(r + 1) * ATT_Q] = jnp.concatenate(
                [qr_ref[h * ROPE_DIM:(h + 1) * ROPE_DIM, :],
                 qn_ref[h * HEAD_DIM + ROPE_DIM:(h + 1) * HEAD_DIM, :], bias_ref[g], pad], axis=0)


    key_in = lax.broadcasted_iota(jnp.int32, (ATT_K, ATT_Q), 0)
    col_in = lax.broadcasted_iota(jnp.int32, (ATT_K, ATT_Q), 1)
    causal = key_in <= col_in
    prev = (key_in > col_in) & (qi > 0)

    def window_scores(g):
        _score_stage(kw_ref.at[g], q0, lhs_ref.at[g], sw_ref.at[g, 0], tw_ref.at[g, 0], mask=causal)
        _score_stage(kw_ref.at[g], prev0, lhs_ref.at[g], sw_ref.at[g, 1], tw_ref.at[g, 1], mask=prev)

    def window_rest(g):
        _window_branch(q0, prev0, vwt_ref.at[g], sw_ref.at[g], tw_ref.at[g], pw_ref.at[g], accw_ref.at[g])

    for g in groups:
        m_ref[g] = jnp.full(m_ref.shape[1:], NEG_INF, F32)
        acc_ref[g] = jnp.zeros(acc_ref.shape[1:], F32)
    first_mask = causal | (qi > 0)
    for g in groups:
        window_scores(g)
        if g > 0:
            window_rest(g - 1)
        _score_stage(ks_ref.at[g], 0, lhs_ref.at[g], s_ref.at[g, 0], t_ref.at[g, 0], mask=first_mask)
    window_rest(groups[-1])

    def finish(g, k0, slot):
        _softmax_stage(s_ref.at[g, slot], t_ref.at[g, slot], m_ref.at[g], p_ref.at[g, slot], a_ref.at[g, slot])
        _value_stage(vst_ref.at[g], k0, p_ref.at[g, slot], a_ref.at[g, slot], acc_ref.at[g])

    def trip(j, cur, last):
        for g in groups:
            _score_stage(ks_ref.at[g], pl.multiple_of((j + 1) * ATT_K, ATT_K), lhs_ref.at[g],
                         s_ref.at[g, 1 - cur], t_ref.at[g, 1 - cur], mask=causal if last else None)
            finish(g, pl.multiple_of(j * ATT_K, ATT_K), cur)
        if last:
            for g in groups:
                finish(g, q0, 1 - cur)

    def trip_any(j, carry):
        pl.when(j % 2 == 0)(lambda: trip(j, 0, False))
        pl.when(j % 2 == 1)(lambda: trip(j, 1, False))
        return carry

    lax.fori_loop(0, qi - 1, trip_any, 0)
    pl.when((qi > 0) & (qi % 2 == 1))(lambda: trip(qi - 1, 0, True))
    pl.when((qi > 0) & (qi % 2 == 0))(lambda: trip(qi - 1, 1, True))

    @pl.when(qi == 0)
    def _():
        for g in groups:
            finish(g, q0, 0)

    for g in groups:
        gate = gate_ref[g]
        heads = []
        for r, cs in enumerate(_HEAD_COLS):
            g_cmp = gate[r:r + 1, :]
            g_sel = gate[HEADS_PER_GROUP + r:HEADS_PER_GROUP + r + 1, :]
            g_win = gate[2 * HEADS_PER_GROUP + r:2 * HEADS_PER_GROUP + r + 1, :]
            c_sel = g_sel / acc_ref[g, HEAD_DIM:HEAD_DIM + 1, cs]
            c_win = g_win / accw_ref[g, HEAD_DIM:HEAD_DIM + 1, cs]
            h = g * HEADS_PER_GROUP + r
            o_cmp = ocmp_ref[h * HEAD_DIM:(h + 1) * HEAD_DIM, :]
            heads.append(g_cmp * o_cmp + c_sel * acc_ref[g, :HEAD_DIM, cs] + c_win * accw_ref[g, :HEAD_DIM, cs])
        o_ref[:, g * gh:(g + 1) * gh] = jnp.concatenate(heads, axis=0).T.astype(BF16)


def _overlap(seq):
    ci = jnp.arange(N_CMP_PAD) * CMP_STRIDE
    sj = jnp.arange(seq // SEL_BLOCK) * SEL_BLOCK
    n_cmp = (seq - CMP_BLOCK) // CMP_STRIDE + 1
    ov = ((ci[None, :] < sj[:, None] + SEL_BLOCK) & (ci[None, :] + CMP_BLOCK > sj[:, None])
          & (jnp.arange(N_CMP_PAD)[None, :] < n_cmp))
    return ov.astype(BF16)


def _attention(qn, qr, bias, ocmp, ks, kw, vst, vwt, gate):
    bsz, _, seq = qn.shape
    assert ATT_Q == WINDOW == ATT_K
    gps = GROUPS_PER_STEP
    gh = gps * HEADS_PER_GROUP * HEAD_DIM
    grp = lambda *blk: pl.BlockSpec((None, gps) + blk, lambda bi, gi, qi: (bi, gi, 0, 0))
    q_spec = pl.BlockSpec((None, gh, ATT_Q), lambda bi, gi, qi: (bi, gi, qi))
    return pl.pallas_call(
        _attn_body,
        grid=(bsz, N_KV_GROUPS // gps, seq // ATT_Q),
        in_specs=[q_spec,
                  pl.BlockSpec((None, gps * HEADS_PER_GROUP * ROPE_DIM, ATT_Q), lambda bi, gi, qi: (bi, gi, qi)),
                  pl.BlockSpec((None, gps, N_SEL_BLOCKS, ATT_Q), lambda bi, gi, qi: (bi, gi, 0, qi)),
                  q_spec,
                  grp(seq, LANES), grp(seq, LANES), grp(V_ROWS, seq), grp(V_ROWS, seq),
                  pl.BlockSpec((None, gps, GATE_ROWS, ATT_Q), lambda bi, gi, qi: (bi, gi, 0, qi))],
        out_specs=pl.BlockSpec((None, ATT_Q, gh), lambda bi, gi, qi: (bi, qi, gi)),
        out_shape=jax.ShapeDtypeStruct((bsz, seq, Q_WIDTH), BF16),
        scratch_shapes=[pltpu.VMEM((gps, LANES, ATT_COLS), BF16),
                        pltpu.VMEM((gps, 2, ATT_K, ATT_COLS), F32),
                        pltpu.VMEM((gps, 2, ATT_K, ATT_COLS), F32),
                        pltpu.VMEM((gps, 2, ATT_K, ATT_COLS), BF16),
                        pltpu.VMEM((gps, 2, ATT_K, ATT_COLS), BF16),
                        pltpu.VMEM((gps, 2, 1, ATT_COLS), F32),
                        pltpu.VMEM((gps, 2, 1, ATT_COLS), F32),
                        pltpu.VMEM((gps, 2, 1, ATT_COLS), F32),
                        pltpu.VMEM((gps, 1, ATT_COLS), F32),
                        pltpu.VMEM((gps, V_ROWS, ATT_COLS), F32),
                        pltpu.VMEM((gps, V_ROWS, ATT_COLS), F32)],
        compiler_params=pltpu.CompilerParams(
            dimension_semantics=("parallel", "parallel", "arbitrary"), vmem_limit_bytes=VMEM_LIMIT),
        name="nsa_attention",
    )(qn, qr, bias, ocmp, ks, kw, vst, vwt, gate)


def _oproj_body(o_ref_in, x_ref, w_ref, g_ref, out_ref):
    m = jnp.dot(o_ref_in[...], w_ref[...], preferred_element_type=F32)
    out_ref[...] = x_ref[...] + _rms(m, g_ref[...])


def _out_proj(o2d, x2d, w_o, g3):
    t = x2d.shape[0]
    row = pl.BlockSpec((FFN_TOKENS, D_MODEL), lambda i: (i, 0))
    return pl.pallas_call(
        _oproj_body,
        grid=(t // FFN_TOKENS,),
        in_specs=[row, row, _const_spec((Q_WIDTH, D_MODEL)), _const_spec((1, D_MODEL))],
        out_specs=row,
        out_shape=jax.ShapeDtypeStruct((t, D_MODEL), F32),
        compiler_params=pltpu.CompilerParams(dimension_semantics=("parallel",)),
        name="nsa_out_proj",
    )(o2d, x2d, w_o.astype(BF16), g3.reshape(1, -1))


def _nsa_block(x, g2, g3, w_in, pos_k, pos_v, wk1, wk2, wv1, wv2, w_o):
    bsz, seq, d = x.shape
    w_t, w_n = _nsa_proj_weights(w_in)
    qn, qr, vst, vwt, gate, ks, kw, kvc = _nsa_proj(x, g2, w_t, w_n, _rope_tables(seq))
    kc, vct = _compress(kvc, pos_k, pos_v, wk1, wk2, wv1, wv2)
    bias, ocmp = _select(qn, kc, vct)
    o = _attention(qn, qr, bias, ocmp, ks, kw, vst, vwt, gate)
    return _out_proj(o.reshape(bsz * seq, Q_WIDTH), x.reshape(bsz * seq, d), w_o, g3).reshape(x.shape)


def kernel(x, norm_gains, ffn_w_gate, ffn_w_up, ffn_w_down, pool_w, pool_b, pool_scale, nsa_w_in, nsa_cmp_pos_k, nsa_cmp_pos_v, nsa_cmp_wk1, nsa_cmp_wk2, nsa_cmp_wv1, nsa_cmp_wv2, nsa_w_o):
    bsz, seq, d = x.shape
    depth = norm_gains.shape[0]
    wg_all, wu_all, wd_all = ffn_w_gate.astype(BF16), ffn_w_up.astype(BF16), ffn_w_down.astype(BF16)

    def ffn(x, i, half):
        g = norm_gains[i]
        y = _ffn_block(x.reshape(bsz * seq, d), g[4 * half], g[4 * half + 1], wg_all, wu_all, wd_all, i, half)
        return y.reshape(bsz, seq, d)

    for i in range(depth):
        g = norm_gains[i]
        x = ffn(x, i, 0)
        j = i // 2
        if i % 2 == 0:
            x = _pool_block(x, g[2], g[3], pool_w[j], pool_b[j], pool_scale[j])
        else:
            x = _nsa_block(x, g[2], g[3], nsa_w_in[j], nsa_cmp_pos_k[j], nsa_cmp_pos_v[j],
                           nsa_cmp_wk1[j], nsa_cmp_wk2[j], nsa_cmp_wv1[j], nsa_cmp_wv2[j], nsa_w_o[j])
        x = ffn(x, i, 1)
    return x
```

```python
import jax
import jax.numpy as jnp
from jax import lax
from jax.experimental import pallas as pl
from jax.experimental.pallas import tpu as pltpu

F32 = jnp.float32
BF16 = jnp.bfloat16

D_MODEL = 1024
EPS = 1e-6
D_FF = 2816
POOL_WINDOWS = (2, 4, 8, 16)
POOL_GROUP = D_MODEL // len(POOL_WINDOWS)
MAX_POOL_WINDOW = max(POOL_WINDOWS)
N_HEADS = 16
HEAD_DIM = 64
N_KV_GROUPS = 4
HEADS_PER_GROUP = N_HEADS // N_KV_GROUPS
ROPE_DIM = HEAD_DIM // 4
ROPE_THETA = 500000.0
CMP_BLOCK = 32
CMP_STRIDE = 16
CMP_HIDDEN = 256
SEL_BLOCK = 64
N_SELECT = 8
WINDOW = 256
Q_WIDTH = N_HEADS * HEAD_DIM
KV_WIDTH = N_KV_GROUPS * HEAD_DIM
NEG_INF = -1e30

LANES = 128
F32_SUBLANES = 8
BF16_SUBLANES = 16
VMEM_LIMIT = 56 * 1024 * 1024

FFN_TOKENS = 1024
FFN_PARTS = 4
SEQ_TILE = 512
ATT_Q = 256
ATT_K = 256
N_CMP_PAD = 128
GROUPS_PER_STEP = 4

N_SEL_BLOCKS = 32
GATE_ROWS = 2 * F32_SUBLANES
ROPE_HALF = ROPE_DIM // 2
V_ROWS = HEAD_DIM + BF16_SUBLANES
LOG2_E = 1.4426950408889634
ATT_COLS = HEADS_PER_GROUP * ATT_Q

_ROW_VS = Q_WIDTH
_ROW_VW = _ROW_VS + KV_WIDTH
_ROW_GATE = _ROW_VW + KV_WIDTH
_PROJ_ROWS = _ROW_GATE + N_KV_GROUPS * GATE_ROWS
_G_COLS = N_KV_GROUPS * LANES
_OFF_KW = _G_COLS
_OFF_KVC = 2 * _G_COLS
_PROJ_COLS = 3 * _G_COLS

_NT = (((1,), (1,)), ((), ()))


def _rms(x, g):
    return x * lax.rsqrt(jnp.mean(x * x, axis=-1, keepdims=True) + EPS) * g


def _const_spec(shape):
    nd = len(shape)
    return pl.BlockSpec(shape, lambda *_: (0,) * nd, pipeline_mode=pl.Buffered(1))


def _ffn_body(x_ref, gpre_ref, gpost_ref, wg_ref, wu_ref, wd_ref, o_ref):
    part = FFN_TOKENS // FFN_PARTS
    halves = [slice(i * part, (i + 1) * part) for i in range(FFN_PARTS)]
    def hidden(rows):
        xb = _rms(x_ref[rows, :], gpre_ref[...]).astype(BF16)
        return (jnp.dot(xb, wg_ref[...], preferred_element_type=F32),
                jnp.dot(xb, wu_ref[...], preferred_element_type=F32))

    def down(rows, h):
        hg, hu = h
        act = (hg * jax.nn.sigmoid(hg) * hu).astype(BF16)
        f = jnp.dot(act, wd_ref[...], preferred_element_type=F32)
        o_ref[rows, :] = x_ref[rows, :] + _rms(f, gpost_ref[...])

    pending = None
    for rows in halves:
        h = hidden(rows)
        if pending is not None:
            down(*pending)
        pending = (rows, h)
    down(*pending)


def _ffn_block(x2d, g_pre, g_post, wg_all, wu_all, wd_all, layer, half):
    t = x2d.shape[0]
    row = pl.BlockSpec((FFN_TOKENS, D_MODEL), lambda i: (i, 0))
    pick = lambda *blk: pl.BlockSpec((None, None) + blk, lambda i: (layer, half, 0, 0),
                                     pipeline_mode=pl.Buffered(1))
    return pl.pallas_call(
        _ffn_body,
        grid=(t // FFN_TOKENS,),
        in_specs=[row, _const_spec((1, D_MODEL)), _const_spec((1, D_MODEL)),
                  pick(D_MODEL, D_FF), pick(D_MODEL, D_FF), pick(D_FF, D_MODEL)],
        out_specs=row,
        out_shape=jax.ShapeDtypeStruct((t, D_MODEL), F32),
        compiler_params=pltpu.CompilerParams(
            dimension_semantics=("parallel",), vmem_limit_bytes=VMEM_LIMIT),
        name="ffn_block",
    )(x2d, g_pre.reshape(1, -1), (0.5 * g_post).reshape(1, -1),
      wg_all, wu_all, wd_all)


def _pool_body(x_ref, g2_ref, g3_ref, w_ref, b_ref, sc_ref, o_ref, carry_ref):
    si = pl.program_id(1)

    @pl.when(si == 0)
    def _():
        carry_ref[...] = jnp.zeros_like(carry_ref)

    x = x_ref[...]
    h = _rms(x, g2_ref[...])
    hp = jnp.concatenate([carry_ref[...], h], axis=0)
    carry_ref[...] = h[SEQ_TILE - MAX_POOL_WINDOW:, :]
    pos = si * SEQ_TILE + lax.broadcasted_iota(jnp.int32, (SEQ_TILE, 1), 0)
    ys = []
    for g, w in enumerate(POOL_WINDOWS):
        cols = slice(g * POOL_GROUP, (g + 1) * POOL_GROUP)
        acc = hp[:, cols]
        k = 1
        while k < w:
            acc = acc + pltpu.roll(acc, k, axis=0)
            k *= 2
        cnt = jnp.minimum(pos + 1, w).astype(F32)
        d = acc[MAX_POOL_WINDOW:, :] / cnt - h[:, cols]
        y = jnp.dot(d.astype(BF16), w_ref[g], preferred_element_type=F32) + b_ref[g:g + 1, :]
        ys.append(y)
    m = jnp.concatenate(ys, axis=1) * sc_ref[...]
    o_ref[...] = x + _rms(m, g3_ref[...])


def _pool_block(x, g2, g3, w, b, scale):
    bsz, seq, _ = x.shape
    row = pl.BlockSpec((None, SEQ_TILE, D_MODEL), lambda bi, si: (bi, si, 0))
    return pl.pallas_call(
        _pool_body,
        grid=(bsz, seq // SEQ_TILE),
        in_specs=[row, _const_spec((1, D_MODEL)), _const_spec((1, D_MODEL)),
                  _const_spec((len(POOL_WINDOWS), POOL_GROUP, POOL_GROUP)),
                  _const_spec((len(POOL_WINDOWS), POOL_GROUP)), _const_spec((1, D_MODEL))],
        out_specs=row,
        out_shape=jax.ShapeDtypeStruct(x.shape, F32),
        scratch_shapes=[pltpu.VMEM((MAX_POOL_WINDOW, D_MODEL), F32)],
        compiler_params=pltpu.CompilerParams(
            dimension_semantics=("parallel", "arbitrary"), vmem_limit_bytes=VMEM_LIMIT),
        name="pool_block",
    )(x, g2.reshape(1, -1), g3.reshape(1, -1), w.astype(BF16), b, scale.reshape(1, -1))


def _proj_body(x_ref, g2_ref, wt_ref, wn_ref, c_ref, sa_ref, sb_ref, ct_ref, st_ref,
               qn_ref, qr_ref, vs_ref, vw_ref, gate_ref, ks_ref, kw_ref, kvc_ref):
    si = pl.program_id(1)
    hb = _rms(x_ref[...], g2_ref[...]).astype(BF16)

    pt = lax.dot_general(wt_ref[...], hb, _NT, preferred_element_type=F32)
    qn_ref[...] = pt[:Q_WIDTH].astype(BF16)
    cos_t, sin_t = ct_ref[...], st_ref[...]
    for h in range(N_HEADS):
        x1 = pt[h * HEAD_DIM:h * HEAD_DIM + ROPE_HALF]
        x2 = pt[h * HEAD_DIM + ROPE_HALF:h * HEAD_DIM + ROPE_DIM]
        rot = jnp.concatenate([x1 * cos_t - x2 * sin_t, x1 * sin_t + x2 * cos_t], axis=0)
        qr_ref[h * ROPE_DIM:(h + 1) * ROPE_DIM, :] = rot.astype(BF16)
    ones_rows = jnp.ones((V_ROWS - HEAD_DIM, SEQ_TILE), F32)
    for g in range(N_KV_GROUPS):
        vs_ref[g] = jnp.concatenate(
            [pt[_ROW_VS + g * HEAD_DIM:_ROW_VS + (g + 1) * HEAD_DIM], ones_rows], axis=0).astype(BF16)
        vw_ref[g] = jnp.concatenate(
            [pt[_ROW_VW + g * HEAD_DIM:_ROW_VW + (g + 1) * HEAD_DIM], ones_rows], axis=0).astype(BF16)
        gate_ref[g] = jax.nn.sigmoid(pt[_ROW_GATE + g * GATE_ROWS:_ROW_GATE + (g + 1) * GATE_ROWS])

    pn = jnp.dot(hb, wn_ref[...], preferred_element_type=F32)
    cos, sin_a, sin_b = c_ref[...], sa_ref[...], sb_ref[...]

    def slab(off, i):
        return pn[:, off + i * LANES: off + (i + 1) * LANES]

    def rope(t):
        return (t * cos + pltpu.roll(t, LANES - ROPE_HALF, axis=1) * sin_a
                + pltpu.roll(t, ROPE_HALF, axis=1) * sin_b)

    pos = si * SEQ_TILE + lax.broadcasted_iota(jnp.int32, (SEQ_TILE, LANES), 0)
    lane = lax.broadcasted_iota(jnp.int32, (SEQ_TILE, LANES), 1)
    blk_onehot = jnp.where(lane - HEAD_DIM == pos // SEL_BLOCK, 1.0, 0.0)
    for g in range(N_KV_GROUPS):
        ks_ref[g] = (rope(slab(0, g)) + blk_onehot).astype(BF16)
        kw_ref[g] = rope(slab(_OFF_KW, g)).astype(BF16)
        kvc_ref[g] = slab(_OFF_KVC, g)


def _nsa_proj_weights(w_in):
    d = w_in.shape[0]
    wq = w_in[:, :Q_WIDTH] * (HEAD_DIM ** -0.5)
    kv = w_in[:, Q_WIDTH:Q_WIDTH + 6 * KV_WIDTH].reshape(d, 6, N_KV_GROUPS, HEAD_DIM)
    wkc, wvc, wks, wvs, wkw, wvw = [kv[:, i] for i in range(6)]
    wgate = w_in[:, Q_WIDTH + 6 * KV_WIDTH:].reshape(d, 3, N_KV_GROUPS, HEADS_PER_GROUP)
    wgate = wgate.transpose(0, 2, 1, 3).reshape(d, N_KV_GROUPS, 3 * HEADS_PER_GROUP)
    wgate = jnp.pad(wgate, ((0, 0), (0, 0), (0, GATE_ROWS - 3 * HEADS_PER_GROUP)))
    w_t = jnp.concatenate([wq, wvs.reshape(d, KV_WIDTH), wvw.reshape(d, KV_WIDTH),
                           wgate.reshape(d, N_KV_GROUPS * GATE_ROWS)], axis=1).T.astype(BF16)
    zero = jnp.zeros_like(wkc)

    def pair(a, b):
        return jnp.concatenate([a, b], axis=-1).reshape(d, _G_COLS)

    w_n = jnp.concatenate([pair(wks * LOG2_E, zero), pair(wkw * LOG2_E, zero), pair(wkc, wvc)],
                          axis=1).astype(BF16)
    return w_t, w_n


def _rope_tables(seq):
    inv = 1.0 / (ROPE_THETA ** (jnp.arange(0, ROPE_DIM, 2, dtype=F32) / ROPE_DIM))
    ang = jnp.arange(seq, dtype=F32)[:, None] * inv[None, :]
    cos, sin = jnp.cos(ang), jnp.sin(ang)
    pad = LANES - ROPE_DIM
    c = jnp.concatenate([cos, cos, jnp.ones((seq, pad), F32)], axis=1)
    sa = jnp.concatenate([-sin, jnp.zeros((seq, ROPE_HALF + pad), F32)], axis=1)
    sb = jnp.concatenate([jnp.zeros((seq, ROPE_HALF), F32), sin, jnp.zeros((seq, pad), F32)], axis=1)
    return c, sa, sb, cos.T, sin.T


def _nsa_proj(x, g2, w_t, w_n, tables):
    bsz, seq, _ = x.shape
    feat = lambda rows, dt: jax.ShapeDtypeStruct((bsz, N_KV_GROUPS, rows, seq), dt)
    feat_spec = lambda rows: pl.BlockSpec((None, N_KV_GROUPS, rows, SEQ_TILE), lambda bi, si: (bi, 0, 0, si))
    tok = lambda dt: jax.ShapeDtypeStruct((bsz, N_KV_GROUPS, seq, LANES), dt)
    tok_spec = pl.BlockSpec((None, N_KV_GROUPS, SEQ_TILE, LANES), lambda bi, si: (bi, 0, si, 0))
    tab_spec = pl.BlockSpec((SEQ_TILE, LANES), lambda bi, si: (si, 0))
    tab_t_spec = pl.BlockSpec((ROPE_HALF, SEQ_TILE), lambda bi, si: (0, si))
    n_rot = N_HEADS * ROPE_DIM
    return pl.pallas_call(
        _proj_body,
        grid=(bsz, seq // SEQ_TILE),
        in_specs=[pl.BlockSpec((None, SEQ_TILE, D_MODEL), lambda bi, si: (bi, si, 0)),
                  _const_spec((1, D_MODEL)), _const_spec((_PROJ_ROWS, D_MODEL)),
                  _const_spec((D_MODEL, _PROJ_COLS)),
                  tab_spec, tab_spec, tab_spec, tab_t_spec, tab_t_spec],
        out_specs=[pl.BlockSpec((None, Q_WIDTH, SEQ_TILE), lambda bi, si: (bi, 0, si)),
                   pl.BlockSpec((None, n_rot, SEQ_TILE), lambda bi, si: (bi, 0, si)),
                   feat_spec(V_ROWS), feat_spec(V_ROWS), feat_spec(GATE_ROWS),
                   tok_spec, tok_spec, tok_spec],
        out_shape=[jax.ShapeDtypeStruct((bsz, Q_WIDTH, seq), BF16),
                   jax.ShapeDtypeStruct((bsz, n_rot, seq), BF16),
                   feat(V_ROWS, BF16), feat(V_ROWS, BF16), feat(GATE_ROWS, F32),
                   tok(BF16), tok(BF16), tok(F32)],
        compiler_params=pltpu.CompilerParams(
            dimension_semantics=("parallel", "parallel"), vmem_limit_bytes=VMEM_LIMIT),
        name="nsa_proj",
    )(x, g2.reshape(1, -1), w_t, w_n, *tables)


def _cmp_body(kvc_ref, pos_ref, w1_ref, w2k_ref, w2vt_ref, kc_ref, vct_ref):
    za, zb = [], []
    for t in range(CMP_STRIDE):
        xt = kvc_ref[pl.ds(t, N_CMP_PAD, stride=CMP_STRIDE), :]
        za.append((xt + pos_ref[t:t + 1, :]).astype(BF16))
        zb.append((xt + pos_ref[CMP_STRIDE + t:CMP_STRIDE + t + 1, :]).astype(BF16))
    first = jnp.dot(jnp.concatenate(za, axis=1), w1_ref[0], preferred_element_type=F32)
    second = jnp.dot(jnp.concatenate(zb, axis=1), w1_ref[1], preferred_element_type=F32)
    row = lax.broadcasted_iota(jnp.int32, second.shape, 0)
    second = jnp.where(row < N_CMP_PAD - 1, pltpu.roll(second, N_CMP_PAD - 1, axis=0), 0.0)
    pre = first + second
    hid = (pre * jax.nn.sigmoid(pre)).astype(BF16)
    kc_ref[...] = jnp.dot(hid[:, :CMP_HIDDEN], w2k_ref[...], preferred_element_type=F32).astype(BF16)
    vct_ref[...] = lax.dot_general(w2vt_ref[...], hid[:, CMP_HIDDEN:], _NT,
                                   preferred_element_type=F32).astype(BF16)


def _compress(kvc, pos_k, pos_v, wk1, wk2, wv1, wv2):
    bsz, _, seq, _ = kvc.shape
    assert seq == N_CMP_PAD * CMP_STRIDE
    pos = jnp.concatenate([pos_k, pos_v], axis=1)
    k1 = wk1.reshape(CMP_BLOCK, HEAD_DIM, CMP_HIDDEN)
    v1 = wv1.reshape(CMP_BLOCK, HEAD_DIM, CMP_HIDDEN)
    z1 = jnp.zeros_like(k1)
    w1 = jnp.concatenate([jnp.concatenate([k1, z1], axis=2),
                          jnp.concatenate([z1, v1], axis=2)], axis=1).astype(BF16)
    w1 = w1.reshape(2, CMP_STRIDE * LANES, 2 * CMP_HIDDEN)
    w2k = (wk2 * LOG2_E).astype(BF16)
    w2vt = wv2.T.astype(BF16)
    return pl.pallas_call(
        _cmp_body,
        grid=(bsz, N_KV_GROUPS),
        in_specs=[pl.BlockSpec((None, None, seq, LANES), lambda bi, gi: (bi, gi, 0, 0)),
                  _const_spec(pos.shape), _const_spec(w1.shape),
                  _const_spec(w2k.shape), _const_spec(w2vt.shape)],
        out_specs=[pl.BlockSpec((None, None, N_CMP_PAD, HEAD_DIM), lambda bi, gi: (bi, gi, 0, 0)),
                   pl.BlockSpec((None, None, HEAD_DIM, N_CMP_PAD), lambda bi, gi: (bi, gi, 0, 0))],
        out_shape=[jax.ShapeDtypeStruct((bsz, N_KV_GROUPS, N_CMP_PAD, HEAD_DIM), BF16),
                   jax.ShapeDtypeStruct((bsz, N_KV_GROUPS, HEAD_DIM, N_CMP_PAD), BF16)],
        compiler_params=pltpu.CompilerParams(dimension_semantics=("parallel", "parallel")),
        name="nsa_compress",
    )(kvc, pos, w1, w2k, w2vt)


_HEAD_COLS = [slice(r * ATT_Q, (r + 1) * ATT_Q) for r in range(HEADS_PER_GROUP)]


def _score_stage(k_ref, k0, lhs_ref, s_out, t_out, mask=None):
    k_tile = k_ref[pl.ds(k0, ATT_K), :]
    t_max = []
    for cs in _HEAD_COLS:
        s = jnp.dot(k_tile, lhs_ref[:, cs], preferred_element_type=F32)
        if mask is not None:
            s = jnp.where(mask, s, NEG_INF)
        s_out[:, cs] = s
        t_max.append(jnp.max(s, axis=0, keepdims=True))
    t_out[...] = jnp.concatenate(t_max, axis=1)


def _softmax_stage(s_in, t_in, m_ref, p_out, a_out):
    m_all = m_ref[...]
    t_all = t_in[...]
    m_new, alpha = [], []
    for cs in _HEAD_COLS:
        s = s_in[:, cs]
        m_c = jnp.maximum(m_all[:, cs], t_all[:, cs])
        p_out[:, cs] = jnp.exp2((s - m_c).astype(BF16))
        alpha.append(jnp.exp2(m_all[:, cs] - m_c))
        m_new.append(m_c)
    m_ref[...] = jnp.concatenate(m_new, axis=1)
    a_out[...] = jnp.concatenate(alpha, axis=1)


def _value_stage(vt_ref, k0, p_in, a_in, acc_ref):
    vt = vt_ref[:, pl.ds(k0, ATT_K)]
    acc_all, a_all = acc_ref[...], a_in[...]
    acc_ref[...] = jnp.concatenate(
        [a_all[:, cs] * acc_all[:, cs] + jnp.dot(vt, p_in[:, cs], preferred_element_type=F32)
         for cs in _HEAD_COLS], axis=1)


def _tree_sum(terms):
    while len(terms) > 1:
        terms = [a + b for a, b in zip(terms[::2], terms[1::2])] + terms[len(terms) & ~1:]
    return terms[0]


def _select_body(qn_ref, kc_ref, vct_ref, ov_ref, bias_ref, ocmp_ref):
    q0 = pl.program_id(1) * ATT_Q
    heads = [(g, r) for g in range(N_KV_GROUPS) for r in range(HEADS_PER_GROUP)]
    rows = lambda g, r: slice((g * HEADS_PER_GROUP + r) * HEAD_DIM, (g * HEADS_PER_GROUP + r + 1) * HEAD_DIM)

    blk_end = lax.broadcasted_iota(jnp.int32, (N_CMP_PAD, ATT_Q), 0) * CMP_STRIDE + CMP_BLOCK - 1
    qpos = q0 + lax.broadcasted_iota(jnp.int32, (N_CMP_PAD, ATT_Q), 1)
    cmp_mask = blk_end <= qpos
    any_cmp = q0 + lax.broadcasted_iota(jnp.int32, (1, ATT_Q), 1) >= CMP_BLOCK - 1
    scores = [jnp.dot(kc_ref[g], qn_ref[rows(g, r), :], preferred_element_type=F32) for g, r in heads]
    probs = []
    for s in scores:
        s = jnp.where(cmp_mask, s, NEG_INF)
        e = jnp.exp2(s - jnp.max(s, axis=0, keepdims=True))
        probs.append(e * jnp.where(any_cmp, 1.0 / jnp.sum(e, axis=0, keepdims=True), 0.0))
    for (g, r), p in zip(heads, probs):
        ocmp_ref[rows(g, r), :] = jnp.dot(vct_ref[g], p.astype(BF16), preferred_element_type=F32)

    p_sum = jnp.concatenate(
        [_tree_sum(probs[g * HEADS_PER_GROUP:(g + 1) * HEADS_PER_GROUP]) for g in range(N_KV_GROUPS)], axis=1)
    hi = p_sum.astype(BF16)
    rem = p_sum - hi.astype(F32)
    mid = rem.astype(BF16)
    lo = (rem - mid.astype(F32)).astype(BF16)
    ov = ov_ref[...]
    imp = (jnp.dot(ov, hi, preferred_element_type=F32) + jnp.dot(ov, mid, preferred_element_type=F32)
           + jnp.dot(ov, lo, preferred_element_type=F32))
    cols = N_KV_GROUPS * ATT_Q
    j = lax.broadcasted_iota(jnp.int32, (N_SEL_BLOCKS, cols), 0)
    q_in = lax.broadcasted_iota(jnp.int32, (N_SEL_BLOCKS, cols), 1) & (ATT_Q - 1)
    q_blk = (q0 + q_in) // SEL_BLOCK
    forced = (j == 0) | (j == q_blk) | (j == q_blk - 1)
    imp = jnp.where(forced, jnp.inf, jnp.where(j > q_blk, -jnp.inf, imp))
    sub = F32_SUBLANES
    row_in = lax.broadcasted_iota(jnp.int32, (sub, cols), 0)
    bias = []
    for b0 in range(0, N_SEL_BLOCKS, sub):
        mine = imp[b0:b0 + sub, :]
        beats = []
        for jp in range(N_SEL_BLOCKS):
            other = imp[jp:jp + 1, :]
            ge = jnp.where(other >= mine, 1.0, 0.0)
            gt = jnp.where(other > mine, 1.0, 0.0)
            if jp < b0:
                beats.append(ge)
            elif jp >= b0 + sub:
                beats.append(gt)
            else:
                beats.append(jnp.where(row_in > jp - b0, ge, gt))
        bias.append(jnp.where(_tree_sum(beats) < N_SELECT, 0.0, NEG_INF))
    bias = jnp.concatenate(bias, axis=0).astype(BF16)
    for g in range(N_KV_GROUPS):
        bias_ref[g] = bias[:, g * ATT_Q:(g + 1) * ATT_Q]


def _select(qn, kc, vct):
    bsz, _, seq = qn.shape
    assert seq // SEL_BLOCK == N_SEL_BLOCKS
    grp = lambda *blk: pl.BlockSpec((None, N_KV_GROUPS) + blk, lambda bi, qi: (bi, 0, 0, 0))
    q_spec = pl.BlockSpec((None, Q_WIDTH, ATT_Q), lambda bi, qi: (bi, 0, qi))
    return pl.pallas_call(
        _select_body,
        grid=(bsz, seq // ATT_Q),
        in_specs=[q_spec, grp(N_CMP_PAD, HEAD_DIM), grp(HEAD_DIM, N_CMP_PAD),
                  _const_spec((N_SEL_BLOCKS, N_CMP_PAD))],
        out_specs=[pl.BlockSpec((None, N_KV_GROUPS, N_SEL_BLOCKS, ATT_Q), lambda bi, qi: (bi, 0, 0, qi)),
                   q_spec],
        out_shape=[jax.ShapeDtypeStruct((bsz, N_KV_GROUPS, N_SEL_BLOCKS, seq), BF16),
                   jax.ShapeDtypeStruct((bsz, Q_WIDTH, seq), F32)],
        compiler_params=pltpu.CompilerParams(dimension_semantics=("parallel", "parallel")),
        name="nsa_select",
    )(qn, kc, vct, _overlap(seq))


def _window_branch(q0, prev0, vwt_ref, sw_ref, tw_ref, pw_ref, accw_ref):
    m_all = jnp.maximum(tw_ref[0], tw_ref[1])
    for cs in _HEAD_COLS:
        m_w = m_all[:, cs]
        pw_ref[0, :, cs] = jnp.exp2((sw_ref[0, :, cs] - m_w).astype(BF16))
        pw_ref[1, :, cs] = jnp.exp2((sw_ref[1, :, cs] - m_w).astype(BF16))
    vw_diag = vwt_ref[:, pl.ds(q0, ATT_K)]
    vw_prev = vwt_ref[:, pl.ds(prev0, ATT_K)]
    for cs in _HEAD_COLS:
        accw_ref[:, cs] = (jnp.dot(vw_diag, pw_ref[0, :, cs], preferred_element_type=F32)
                           + jnp.dot(vw_prev, pw_ref[1, :, cs], preferred_element_type=F32))


def _attn_body(qn_ref, qr_ref, bias_ref, ocmp_ref, ks_ref, kw_ref, vst_ref, vwt_ref, gate_ref,
               o_ref, lhs_ref, s_ref, sw_ref, p_ref, pw_ref, t_ref, tw_ref, a_ref, m_ref, acc_ref, accw_ref):
    qi = pl.program_id(2)
    q0 = pl.multiple_of(qi * ATT_Q, ATT_Q)
    prev0 = pl.multiple_of(jnp.maximum(qi - 1, 0) * ATT_K, ATT_K)
    groups = range(GROUPS_PER_STEP)
    gh = HEADS_PER_GROUP * HEAD_DIM

    pad = jnp.zeros((LANES - HEAD_DIM - N_SEL_BLOCKS, ATT_Q), BF16)
    for g in groups:
        for r in range(HEADS_PER_GROUP):
            h = g * HEADS_PER_GROUP + r
            lhs_ref[g, :, r * ATT_Q:(r + 1) * ATT_Q] = jnp.concatenate(
                [qr_ref[h * ROPE_DIM:(h + 1) * ROPE_DIM, :],
                 qn_ref[h * HEAD_DIM + ROPE_DIM:(h + 1) * HEAD_DIM, :], bias_ref[g], pad], axis=0)


    key_in = lax.broadcasted_iota(jnp.int32, (ATT_K, ATT_Q), 0)
    col_in = lax.broadcasted_iota(jnp.int32, (ATT_K, ATT_Q), 1)
    causal = key_in <= col_in
    prev = (key_in > col_in) & (qi > 0)

    def window_scores(g):
        _score_stage(kw_ref.at[g], q0, lhs_ref.at[g], sw_ref.at[g, 0], tw_ref.at[g, 0], mask=causal)
        _score_stage(kw_ref.at[g], prev0, lhs_ref.at[g], sw_ref.at[g, 1], tw_ref.at[g, 1], mask=prev)

    def window_rest(g):
        _window_branch(q0, prev0, vwt_ref.at[g], sw_ref.at[g], tw_ref.at[g], pw_ref.at[g], accw_ref.at[g])

    for g in groups:
        m_ref[g] = jnp.full(m_ref.shape[1:], NEG_INF, F32)
        acc_ref[g] = jnp.zeros(acc_ref.shape[1:], F32)
    first_mask = causal | (qi > 0)
    for g in groups:
        window_scores(g)
        if g > 0:
            window_rest(g - 1)
        _score_stage(ks_ref.at[g], 0, lhs_ref.at[g], s_ref.at[g, 0], t_ref.at[g, 0], mask=first_mask)
    window_rest(groups[-1])

    def finish(g, k0, slot):
        _softmax_stage(s_ref.at[g, slot], t_ref.at[g, slot], m_ref.at[g], p_ref.at[g, slot], a_ref.at[g, slot])
        _value_stage(vst_ref.at[g], k0, p_ref.at[g, slot], a_ref.at[g, slot], acc_ref.at[g])

    def trip(j, cur, last):
        for g in groups:
            _score_stage(ks_ref.at[g], pl.multiple_of((j + 1) * ATT_K, ATT_K), lhs_ref.at[g],
                         s_ref.at[g, 1 - cur], t_ref.at[g, 1 - cur], mask=causal if last else None)
            finish(g, pl.multiple_of(j * ATT_K, ATT_K), cur)
        if last:
            for g in groups:
                finish(g, q0, 1 - cur)

    def trip_any(j, carry):
        pl.when(j % 2 == 0)(lambda: trip(j, 0, False))
        pl.when(j % 2 == 1)(lambda: trip(j, 1, False))
        return carry

    lax.fori_loop(0, qi - 1, trip_any, 0)
    pl.when((qi > 0) & (qi % 2 == 1))(lambda: trip(qi - 1, 0, True))
    pl.when((qi > 0) & (qi % 2 == 0))(lambda: trip(qi - 1, 1, True))

    @pl.when(qi == 0)
    def _():
        for g in groups:
            finish(g, q0, 0)

    for g in groups:
        gate = gate_ref[g]
        heads = []
        for r, cs in enumerate(_HEAD_COLS):
            g_cmp = gate[r:r + 1, :]
            g_sel = gate[HEADS_PER_GROUP + r:HEADS_PER_GROUP + r + 1, :]
            g_win = gate[2 * HEADS_PER_GROUP + r:2 * HEADS_PER_GROUP + r + 1, :]
            c_sel = g_sel / acc_ref[g, HEAD_DIM:HEAD_DIM + 1, cs]
            c_win = g_win / accw_ref[g, HEAD_DIM:HEAD_DIM + 1, cs]
            h = g * HEADS_PER_GROUP + r
            o_cmp = ocmp_ref[h * HEAD_DIM:(h + 1) * HEAD_DIM, :]
            heads.append(g_cmp * o_cmp + c_sel * acc_ref[g, :HEAD_DIM, cs] + c_win * accw_ref[g, :HEAD_DIM, cs])
        o_ref[:, g * gh:(g + 1) * gh] = jnp.concatenate(heads, axis=0).T.astype(BF16)


def _overlap(seq):
    ci = jnp.arange(N_CMP_PAD) * CMP_STRIDE
    sj = jnp.arange(seq // SEL_BLOCK) * SEL_BLOCK
    n_cmp = (seq - CMP_BLOCK) // CMP_STRIDE + 1
    ov = ((ci[None, :] < sj[:, None] + SEL_BLOCK) & (ci[None, :] + CMP_BLOCK > sj[:, None])
          & (jnp.arange(N_CMP_PAD)[None, :] < n_cmp))
    return ov.astype(BF16)


def _attention(qn, qr, bias, ocmp, ks, kw, vst, vwt, gate):
    bsz, _, seq = qn.shape
    assert ATT_Q == WINDOW == ATT_K
    gps = GROUPS_PER_STEP
    gh = gps * HEADS_PER_GROUP * HEAD_DIM
    grp = lambda *blk: pl.BlockSpec((None, gps) + blk, lambda bi, gi, qi: (bi, gi, 0, 0))
    q_spec = pl.BlockSpec((None, gh, ATT_Q), lambda bi, gi, qi: (bi, gi, qi))
    return pl.pallas_call(
        _attn_body,
        grid=(bsz, N_KV_GROUPS // gps, seq // ATT_Q),
        in_specs=[q_spec,
                  pl.BlockSpec((None, gps * HEADS_PER_GROUP * ROPE_DIM, ATT_Q), lambda bi, gi, qi: (bi, gi, qi)),
                  pl.BlockSpec((None, gps, N_SEL_BLOCKS, ATT_Q), lambda bi, gi, qi: (bi, gi, 0, qi)),
                  q_spec,
                  grp(seq, LANES), grp(seq, LANES), grp(V_ROWS, seq), grp(V_ROWS, seq),
                  pl.BlockSpec((None, gps, GATE_ROWS, ATT_Q), lambda bi, gi, qi: (bi, gi, 0, qi))],
        out_specs=pl.BlockSpec((None, ATT_Q, gh), lambda bi, gi, qi: (bi, qi, gi)),
        out_shape=jax.ShapeDtypeStruct((bsz, seq, Q_WIDTH), BF16),
        scratch_shapes=[pltpu.VMEM((gps, LANES, ATT_COLS), BF16),
                        pltpu.VMEM((gps, 2, ATT_K, ATT_COLS), F32),
                        pltpu.VMEM((gps, 2, ATT_K, ATT_COLS), F32),
                        pltpu.VMEM((gps, 2, ATT_K, ATT_COLS), BF16),
                        pltpu.VMEM((gps, 2, ATT_K, ATT_COLS), BF16),
                        pltpu.VMEM((gps, 2, 1, ATT_COLS), F32),
                        pltpu.VMEM((gps, 2, 1, ATT_COLS), F32),
                        pltpu.VMEM((gps, 2, 1, ATT_COLS), F32),
                        pltpu.VMEM((gps, 1, ATT_COLS), F32),
                        pltpu.VMEM((gps, V_ROWS, ATT_COLS), F32),
                        pltpu.VMEM((gps, V_ROWS, ATT_COLS), F32)],
        compiler_params=pltpu.CompilerParams(
            dimension_semantics=("parallel", "parallel", "arbitrary"), vmem_limit_bytes=VMEM_LIMIT),
        name="nsa_attention",
    )(qn, qr, bias, ocmp, ks, kw, vst, vwt, gate)


def _oproj_body(o_ref_in, x_ref, w_ref, g_ref, out_ref):
    m = jnp.dot(o_ref_in[...], w_ref[...], preferred_element_type=F32)
    out_ref[...] = x_ref[...] + _rms(m, g_ref[...])


def _out_proj(o2d, x2d, w_o, g3):
    t = x2d.shape[0]
    row = pl.BlockSpec((SEQ_TILE, D_MODEL), lambda i: (i, 0))
    return pl.pallas_call(
        _oproj_body,
        grid=(t // SEQ_TILE,),
        in_specs=[row, row, _const_spec((Q_WIDTH, D_MODEL)), _const_spec((1, D_MODEL))],
        out_specs=row,
        out_shape=jax.ShapeDtypeStruct((t, D_MODEL), F32),
        compiler_params=pltpu.CompilerParams(dimension_semantics=("parallel",)),
        name="nsa_out_proj",
    )(o2d, x2d, w_o.astype(BF16), g3.reshape(1, -1))


def _nsa_block(x, g2, g3, w_in, pos_k, pos_v, wk1, wk2, wv1, wv2, w_o):
    bsz, seq, d = x.shape
    w_t, w_n = _nsa_proj_weights(w_in)
    qn, qr, vst, vwt, gate, ks, kw, kvc = _nsa_proj(x, g2, w_t, w_n, _rope_tables(seq))
    kc, vct = _compress(kvc, pos_k, pos_v, wk1, wk2, wv1, wv2)
    bias, ocmp = _select(qn, kc, vct)
    o = _attention(qn, qr, bias, ocmp, ks, kw, vst, vwt, gate)
    return _out_proj(o.reshape(bsz * seq, Q_WIDTH), x.reshape(bsz * seq, d), w_o, g3).reshape(x.shape)


def kernel(x, norm_gains, ffn_w_gate, ffn_w_up, ffn_w_down, pool_w, pool_b, pool_scale, nsa_w_in, nsa_cmp_pos_k, nsa_cmp_pos_v, nsa_cmp_wk1, nsa_cmp_wk2, nsa_cmp_wv1, nsa_cmp_wv2, nsa_w_o):
    bsz, seq, d = x.shape
    depth = norm_gains.shape[0]
    wg_all, wu_all, wd_all = ffn_w_gate.astype(BF16), ffn_w_up.astype(BF16), ffn_w_down.astype(BF16)

    def ffn(x, i, half):
        g = norm_gains[i]
        y = _ffn_block(x.reshape(bsz * seq, d), g[4 * half], g[4 * half + 1], wg_all, wu_all, wd_all, i, half)
        return y.reshape(bsz, seq, d)

    for i in range(depth):
        g = norm_gains[i]
        x = ffn(x, i, 0)
        j = i // 2
        if i % 2 == 0:
            x = _pool_block(x, g[2], g[3], pool_w[j], pool_b[j], pool_scale[j])
        else:
            x = _nsa_block(x, g[2], g[3], nsa_w_in[j], nsa_cmp_pos_k[j], nsa_cmp_pos_v[j],
                           nsa_cmp_wk1[j], nsa_cmp_wk2[j], nsa_cmp_wv1[j], nsa_cmp_wv2[j], nsa_w_o[j])
        x = ffn(x, i, 1)
    return x
```
